```python
import jax, jax.numpy as jnp
from jax import lax
import numpy as np

D_MODEL = 1024
BATCH = 16
SEQ = 2048
DEPTH = 2
DEC_BATCH = 128
DEC_SEQ = 1
PAST_LEN = 16384
PAGE_SIZE = 128

N_META = 16
NH_M = 4
DH_M = 128
W_M = NH_M * DH_M
MLSTM_CHUNK = 128
C_CONV = 512
CONV_W = 31
H_A = 8
KV_A = 2
DH_A = 64
ROT_DIM = DH_A // 4
ROPE_THETA = 500000.0
WINDOW = 128
ATTN_BLOCK = WINDOW
D_FF = 2816
FFN_CONV_W = 3
IN_SIZES = (W_M, W_M, W_M, NH_M, NH_M, W_M, C_CONV, C_CONV, H_A * DH_A, KV_A * DH_A, KV_A * DH_A, 3 * D_MODEL)
IN_COLS = sum(IN_SIZES)
RMS_EPS = 1e-6
LN_EPS = 1e-5

kernel_name = "hybrid_mlstm_conformer_swa_step"


def rmsnorm(x, g):
    x32 = x.astype(jnp.float32)
    y = x32 * lax.rsqrt(jnp.mean(x32 * x32, axis=-1, keepdims=True) + RMS_EPS)
    return (y * g.astype(jnp.float32)).astype(x.dtype)


def layernorm(x, g, b):
    x32 = x.astype(jnp.float32)
    xc = x32 - jnp.mean(x32, axis=-1, keepdims=True)
    var = jnp.mean(xc * xc, axis=-1, keepdims=True)
    return (xc * lax.rsqrt(var + LN_EPS) * g.astype(jnp.float32) + b.astype(jnp.float32)).astype(x.dtype)


def split_cols(z):
    out, start = [], 0
    for size in IN_SIZES:
        out.append(z[..., start:start + size])
        start += size
    return out


def causal_dwconv(u, buf, w, b):
    width, ch = w.shape
    xp = jnp.concatenate([buf.astype(u.dtype), u], axis=1)
    y = lax.conv_general_dilated(xp, w.astype(u.dtype)[:, None, :], (1,), 'VALID',
                                 dimension_numbers=('NWC', 'WIO', 'NWC'), feature_group_count=ch)
    return y + b.astype(u.dtype), xp[:, xp.shape[1] - (width - 1):]


def rope(x, pos):
    half = ROT_DIM // 2
    inv = jnp.power(ROPE_THETA, -jnp.arange(half, dtype=jnp.float32) / half)
    ang = pos.astype(jnp.float32)[:, None] * inv[None, :]
    cos = jnp.cos(ang)[:, None, :]
    sin = jnp.sin(ang)[:, None, :]
    x1 = x[..., :half].astype(jnp.float32)
    x2 = x[..., half:ROT_DIM].astype(jnp.float32)
    r = jnp.concatenate([x1 * cos - x2 * sin, x2 * cos + x1 * sin], axis=-1).astype(x.dtype)
    return jnp.concatenate([r, x[..., ROT_DIM:]], axis=-1)


def window_mask(qpos, kpos):
    return (kpos >= 0) & (kpos <= qpos) & (qpos - kpos <= WINDOW)


def sink_attention(q, k, v, mask, sinks):
    bsz, nblk, nq, nh, d = q.shape
    g = k.shape[3]
    r = nh // g
    qg = q.reshape(bsz, nblk, nq, g, r, d)
    s = jnp.einsum('bnqgrd,bnkgd->bngrqk', qg, k).astype(jnp.float32) * (d ** -0.5)
    s = jnp.where(mask[None, :, None, None], s, -jnp.inf)
    sk = sinks.astype(jnp.float32).reshape(1, 1, g, r, 1)
    mx = jnp.maximum(s.max(axis=-1), sk)
    p = jnp.exp(s - mx[..., None])
    den = p.sum(axis=-1) + jnp.exp(sk - mx)
    o = jnp.einsum('bngrqk,bnkgd->bnqgrd', (p / den[..., None]).astype(v.dtype), v)
    return o.reshape(bsz, nblk, nq, nh, d)


def swa_prompt(q, k, v, sinks):
    bsz, t, nh, d = q.shape
    g = k.shape[2]
    lead = (-t) % ATTN_BLOCK
    nb = (t + lead) // ATTN_BLOCK
    qp = jnp.pad(q, ((0, 0), (lead, 0), (0, 0), (0, 0))).reshape(bsz, nb, ATTN_BLOCK, nh, d)
    kp = jnp.pad(k, ((0, 0), (lead + ATTN_BLOCK, 0), (0, 0), (0, 0)))
    vp = jnp.pad(v, ((0, 0), (lead + ATTN_BLOCK, 0), (0, 0), (0, 0)))
    def band(a):
        return jnp.concatenate([a[:, :-ATTN_BLOCK].reshape(bsz, nb, ATTN_BLOCK, g, d),
                                a[:, ATTN_BLOCK:].reshape(bsz, nb, ATTN_BLOCK, g, d)], axis=2)
    qpos = (jnp.arange(nb * ATTN_BLOCK) - lead).reshape(nb, ATTN_BLOCK)
    kpos = (jnp.arange(nb)[:, None] * ATTN_BLOCK - ATTN_BLOCK - lead) + jnp.arange(2 * ATTN_BLOCK)[None, :]
    mask = window_mask(qpos[:, :, None], kpos[:, None, :])
    o = sink_attention(qp, band(kp), band(vp), mask, sinks).reshape(bsz, nb * ATTN_BLOCK, nh, d)[:, lead:]
    return o, k[:, t - WINDOW:], v[:, t - WINDOW:]


def swa_sample(q, k, v, k_buf, v_buf, pos, sinks):
    wb = k_buf.shape[1]
    kc = jnp.concatenate([k_buf.astype(k.dtype), k], axis=1)
    vc = jnp.concatenate([v_buf.astype(v.dtype), v], axis=1)
    kpos = PAST_LEN - wb + jnp.arange(kc.shape[1])
    mask = window_mask(pos[:, None], kpos[None, :])[None]
    o = sink_attention(q[:, None], kc[:, None], vc[:, None], mask, sinks)[:, 0]
    return o, kc[:, kc.shape[1] - wb:], vc[:, vc.shape[1] - wb:]


def mlstm_chunk(carry, xs):
    c, n, m = carry
    q, k, v, ig, lf = xs
    q = q.transpose(0, 2, 1, 3)
    k = k.transpose(0, 2, 1, 3)
    v = v.transpose(0, 2, 1, 3)
    ig = ig.transpose(0, 2, 1)
    b = jnp.cumsum(lf.transpose(0, 2, 1), axis=-1)
    ln = q.shape[2]
    causal = jnp.tril(jnp.ones((ln, ln), dtype=bool))
    log_d = jnp.where(causal, b[..., :, None] - b[..., None, :] + ig[..., None, :], -jnp.inf)
    log_prev = b + m[..., None]
    m_t = jnp.maximum(log_prev, log_d.max(axis=-1))
    dmat = jnp.exp(log_d - m_t[..., None])
    w_prev = jnp.exp(log_prev - m_t)
    s = jnp.einsum('bhtd,bhsd->bhts', q, k) * dmat
    num = jnp.einsum('bhts,bhsv->bhtv', s, v) + w_prev[..., None] * jnp.einsum('bhtk,bhkv->bhtv', q, c)
    dot = s.sum(axis=-1) + w_prev * jnp.einsum('bhtk,bhk->bht', q, n)
    h = num / jnp.maximum(jnp.abs(dot), jnp.exp(-m_t))[..., None]
    b_last = b[..., -1]
    log_w = b_last[..., None] - b + ig
    m_new = jnp.maximum(b_last + m, log_w.max(axis=-1))
    wts = jnp.exp(log_w - m_new[..., None])
    decay = jnp.exp(b_last + m - m_new)
    c_new = decay[..., None, None] * c + jnp.einsum('bhs,bhsk,bhsv->bhkv', wts, k, v)
    n_new = decay[..., None] * n + jnp.einsum('bhs,bhsk->bhk', wts, k)
    return (c_new, n_new, m_new), h.transpose(0, 2, 1, 3)


def mlstm_prompt(carry, q, k, v, ig, lf):
    bsz = q.shape[0]
    carry, h0 = mlstm_chunk(carry, (q[:, :N_META], k[:, :N_META], v[:, :N_META], ig[:, :N_META], lf[:, :N_META]))
    def to_chunks(a):
        r = a[:, N_META:]
        nc = r.shape[1] // MLSTM_CHUNK
        return jnp.moveaxis(r.reshape((bsz, nc, MLSTM_CHUNK) + r.shape[2:]), 1, 0)
    carry, hs = lax.scan(mlstm_chunk, carry, (to_chunks(q), to_chunks(k), to_chunks(v), to_chunks(ig), to_chunks(lf)))
    hs = jnp.moveaxis(hs, 0, 1).reshape(bsz, -1, NH_M, DH_M)
    return jnp.concatenate([h0, hs], axis=1), carry


def layer(x, pos, st, w, prompt):
    c0, n0, m0, conv_buf, k_buf, v_buf, ffn_buf = st
    bsz, ln, _ = x.shape
    f32 = jnp.float32
    h = rmsnorm(x, w['norm1_g'])
    (q_m, k_m, v_m, i_pre, f_pre, o_pre, glu_a, glu_b, q_a, k_a, v_a, g_pre) = split_cols(h @ w['w_in'])
    qm = q_m.reshape(bsz, ln, NH_M, DH_M).astype(f32)
    km = k_m.reshape(bsz, ln, NH_M, DH_M).astype(f32) * (DH_M ** -0.5)
    vm = v_m.reshape(bsz, ln, NH_M, DH_M).astype(f32)
    ig = (i_pre + w['b_igate']).astype(f32)
    lf = jax.nn.log_sigmoid((f_pre + w['b_fgate']).astype(f32))
    carry0 = (c0.astype(f32), n0.astype(f32), m0.astype(f32))
    if prompt:
        hm, (c1, n1, m1) = mlstm_prompt(carry0, qm, km, vm, ig, lf)
    else:
        (c1, n1, m1), hm = mlstm_chunk(carry0, (qm, km, vm, ig, lf))
    y_a = (jax.nn.sigmoid(o_pre) * hm.reshape(bsz, ln, W_M).astype(x.dtype)) @ w['w_mlstm_out']
    u = glu_a * jax.nn.sigmoid(glu_b)
    uc, conv_new = causal_dwconv(u, conv_buf, w['w_dconv'], w['b_dconv'])
    y_b = jax.nn.silu(layernorm(uc, w['ln_conv_g'], w['ln_conv_b'])) @ w['w_conv_out']
    qa = rope(q_a.reshape(bsz, ln, H_A, DH_A), pos)
    ka = rope(k_a.reshape(bsz, ln, KV_A, DH_A), pos)
    va = v_a.reshape(bsz, ln, KV_A, DH_A)
    if prompt:
        oa, k_new, v_new = swa_prompt(qa, ka, va, w['attn_sinks'])
    else:
        oa, k_new, v_new = swa_sample(qa, ka, va, k_buf, v_buf, pos, w['attn_sinks'])
    y_c = oa.reshape(bsz, ln, H_A * DH_A) @ w['w_attn_out']
    gates = jax.nn.sigmoid(g_pre + w['b_merge'])
    g_a, g_b, g_c = gates[..., :D_MODEL], gates[..., D_MODEL:2 * D_MODEL], gates[..., 2 * D_MODEL:]
    x = x + (g_a * y_a + g_b * y_b + g_c * y_c) @ w['w_out']
    h2 = rmsnorm(x, w['norm2_g'])
    gp = h2 @ w['w_ffn_gate']
    up = h2 @ w['w_ffn_up']
    gc, ffn_new = causal_dwconv(gp, ffn_buf, w['w_ffn_conv'], w['b_ffn_conv'])
    x = x + (jax.nn.silu(gc) * up) @ w['w_ffn_down']
    return x, (c1, n1, m1, conv_new, k_new, v_new, ffn_new)


def setup_inputs(seed: int = 0) -> dict:
    key = jax.random.key(seed)
    ks = list(jax.random.split(key, 40))
    def nrm(shape, scale):
        return scale * jax.random.normal(ks.pop(), shape, jnp.float32)
    wb = min(WINDOW, PAST_LEN)
    d = {}
    d['x_prompt'] = nrm((BATCH, SEQ, D_MODEL), 1.0)
    d['x_sample'] = nrm((DEC_BATCH, DEC_SEQ, D_MODEL), 1.0)
    d['state_mlstm_c'] = nrm((DEPTH, DEC_BATCH, NH_M, DH_M, DH_M), 0.05)
    d['state_mlstm_n'] = nrm((DEPTH, DEC_BATCH, NH_M, DH_M), 0.1)
    d['state_mlstm_m'] = nrm((DEPTH, DEC_BATCH, NH_M), 1.0)
    d['state_conv'] = nrm((DEPTH, DEC_BATCH, CONV_W - 1, C_CONV), 0.5)
    d['cache_swa_k'] = nrm((DEPTH, DEC_BATCH, wb, KV_A, DH_A), 1.0)
    d['cache_swa_v'] = nrm((DEPTH, DEC_BATCH, wb, KV_A, DH_A), 1.0)
    d['state_ffn_conv'] = nrm((DEPTH, DEC_BATCH, FFN_CONV_W - 1, D_FF), 1.0)
    d['meta_tokens'] = nrm((N_META, D_MODEL), 1.0)
    d['norm1_g'] = 1.0 + nrm((DEPTH, D_MODEL), 0.02)
    d['w_in'] = nrm((DEPTH, D_MODEL, IN_COLS), D_MODEL ** -0.5)
    d['b_igate'] = nrm((DEPTH, NH_M), 0.1)
    d['b_fgate'] = 3.0 + nrm((DEPTH, NH_M), 0.5)
    d['w_mlstm_out'] = nrm((DEPTH, W_M, D_MODEL), W_M ** -0.5)
    d['w_dconv'] = nrm((DEPTH, CONV_W, C_CONV), CONV_W ** -0.5)
    d['b_dconv'] = nrm((DEPTH, C_CONV), 0.02)
    d['ln_conv_g'] = 1.0 + nrm((DEPTH, C_CONV), 0.02)
    d['ln_conv_b'] = nrm((DEPTH, C_CONV), 0.02)
    d['w_conv_out'] = nrm((DEPTH, C_CONV, D_MODEL), C_CONV ** -0.5)
    d['attn_sinks'] = nrm((DEPTH, H_A), 0.5)
    d['w_attn_out'] = nrm((DEPTH, H_A * DH_A, D_MODEL), (H_A * DH_A) ** -0.5)
    d['b_merge'] = nrm((DEPTH, 3 * D_MODEL), 0.02)
    d['w_out'] = nrm((DEPTH, D_MODEL, D_MODEL), D_MODEL ** -0.5)
    d['norm2_g'] = 1.0 + nrm((DEPTH, D_MODEL), 0.02)
    d['w_ffn_gate'] = nrm((DEPTH, D_MODEL, D_FF), D_MODEL ** -0.5)
    d['w_ffn_up'] = nrm((DEPTH, D_MODEL, D_FF), D_MODEL ** -0.5)
    d['w_ffn_conv'] = nrm((DEPTH, FFN_CONV_W, D_FF), FFN_CONV_W ** -0.5)
    d['b_ffn_conv'] = nrm((DEPTH, D_FF), 0.02)
    d['w_ffn_down'] = nrm((DEPTH, D_FF, D_MODEL), D_FF ** -0.5)
    d['final_norm_g'] = 1.0 + nrm((D_MODEL,), 0.02)
    return d


def reference(x_prompt, x_sample, state_mlstm_c, state_mlstm_n, state_mlstm_m, state_conv, cache_swa_k, cache_swa_v,
              state_ffn_conv, meta_tokens, norm1_g, w_in, b_igate, b_fgate, w_mlstm_out, w_dconv, b_dconv, ln_conv_g,
              ln_conv_b, w_conv_out, attn_sinks, w_attn_out, b_merge, w_out, norm2_g, w_ffn_gate, w_ffn_up, w_ffn_conv,
              b_ffn_conv, w_ffn_down, final_norm_g):
    bsz = x_prompt.shape[0]
    f32 = jnp.float32
    meta = jnp.broadcast_to(meta_tokens.astype(x_prompt.dtype)[None], (bsz, N_META, D_MODEL))
    xp = jnp.concatenate([meta, x_prompt], axis=1)
    pos_p = jnp.arange(xp.shape[1])
    xs = x_sample
    pos_s = PAST_LEN + jnp.arange(xs.shape[1])
    p_states = [[] for _ in range(7)]
    s_states = [[] for _ in range(7)]
    for l in range(DEPTH):
        w = dict(norm1_g=norm1_g[l], w_in=w_in[l], b_igate=b_igate[l], b_fgate=b_fgate[l], w_mlstm_out=w_mlstm_out[l],
                 w_dconv=w_dconv[l], b_dconv=b_dconv[l], ln_conv_g=ln_conv_g[l], ln_conv_b=ln_conv_b[l],
                 w_conv_out=w_conv_out[l], attn_sinks=attn_sinks[l], w_attn_out=w_attn_out[l], b_merge=b_merge[l],
                 w_out=w_out[l], norm2_g=norm2_g[l], w_ffn_gate=w_ffn_gate[l], w_ffn_up=w_ffn_up[l],
                 w_ffn_conv=w_ffn_conv[l], b_ffn_conv=b_ffn_conv[l], w_ffn_down=w_ffn_down[l])
        p_init = (jnp.zeros((bsz, NH_M, DH_M, DH_M), f32), jnp.zeros((bsz, NH_M, DH_M), f32),
                  jnp.zeros((bsz, NH_M), f32), jnp.zeros((bsz, CONV_W - 1, C_CONV), xp.dtype), None, None,
                  jnp.zeros((bsz, FFN_CONV_W - 1, D_FF), xp.dtype))
        xp, pst = layer(xp, pos_p, p_init, w, True)
        s_init = (state_mlstm_c[l], state_mlstm_n[l], state_mlstm_m[l], state_conv[l], cache_swa_k[l], cache_swa_v[l],
                  state_ffn_conv[l])
        xs, sst = layer(xs, pos_s, s_init, w, False)
        for i in range(7):
            p_states[i].append(pst[i])
            s_states[i].append(sst[i])
    y_prompt = rmsnorm(xp[:, N_META:], final_norm_g)
    y_sample = rmsnorm(xs, final_norm_g)
    p_c, p_n, p_m, p_conv, p_k, p_v, p_ffn = [jnp.stack(a, axis=0) for a in p_states]
    s_c, s_n, s_m, s_conv, s_k, s_v, s_ffn = [jnp.stack(a, axis=0) for a in s_states]
    return (y_prompt, y_sample, p_c, p_n, p_m, p_conv, p_k, p_v, p_ffn, s_c, s_n, s_m, s_conv, s_k, s_v, s_ffn)
```

```python
import functools

import jax
import jax.numpy as jnp
from jax import lax
from jax.experimental import pallas as pl
from jax.experimental.pallas import tpu as pltpu

F32 = jnp.float32
BF16 = jnp.bfloat16

N_META = 16
NH_M = 4
DH_M = 128
W_M = NH_M * DH_M
C_CONV = 512
CONV_W = 31
H_A = 8
KV_A = 2
DH_A = 64
REP_A = H_A // KV_A
W_Q = H_A * DH_A
W_KV = KV_A * DH_A
ROT_DIM = DH_A // 4
ROT_HALF = ROT_DIM // 2
ROPE_THETA = 500000.0
WINDOW = 128
FFN_CONV_W = 3
PAST_LEN = 16384
RMS_EPS = 1e-6
LN_EPS = 1e-5

BLK = 128
LANES = 128
SUBLANES = 8
HIST = 32
DEC_ROWS = 8
VMEM_LIMIT = 52 * 1024 * 1024

OFF_QKVM = 0
OFF_O = OFF_QKVM + 3 * W_M
OFF_GLU = OFF_O + W_M
OFF_QA = OFF_GLU + 2 * C_CONV
OFF_KA = OFF_QA + W_Q
OFF_VA = OFF_KA + W_KV
OFF_G = OFF_VA + W_KV


def _cparams(sem):
    return pltpu.CompilerParams(dimension_semantics=sem, vmem_limit_bytes=VMEM_LIMIT)


def _const_spec(shape):
    nd = len(shape)
    return pl.BlockSpec(shape, lambda *_: (0,) * nd, pipeline_mode=pl.Buffered(1))


def _rms(x, g):
    return x * lax.rsqrt(jnp.mean(x * x, axis=-1, keepdims=True) + RMS_EPS) * g


def _log_sigmoid(x):
    return jnp.minimum(x, 0.0) - jnp.log1p(jnp.exp(-jnp.abs(x)))


def _sigmoid(x):
    return 1.0 / (1.0 + jnp.exp(-x))


def _silu(x):
    return x * _sigmoid(x)


def _dot(a, b):
    return jnp.dot(a.astype(BF16), b.astype(BF16), preferred_element_type=F32)


def _dot_nt(a, b):
    return lax.dot_general(a.astype(BF16), b.astype(BF16), (((1,), (1,)), ((), ())),
                           preferred_element_type=F32)


def _in_proj_kernel(x_ref, g_ref, w_ref, bm_ref, bif_ref,
                    qkvm_ref, og_ref, u_ref, qa_ref, ka_ref, va_ref, gates_ref, ifg_ref):
    d_model = x_ref.shape[1]
    h = _rms(x_ref[...], g_ref[...]).astype(BF16)

    def proj(off, n):
        return jnp.dot(h, w_ref[:, off:off + n], preferred_element_type=F32)

    qkvm_ref[...] = proj(OFF_QKVM, 3 * W_M)
    og_ref[...] = _sigmoid(proj(OFF_O, W_M))
    glu = proj(OFF_GLU, 2 * C_CONV)
    u_ref[...] = glu[:, :C_CONV] * _sigmoid(glu[:, C_CONV:])
    qa_ref[...] = proj(OFF_QA, W_Q)
    ka_ref[...] = proj(OFF_KA, W_KV)
    va_ref[...] = proj(OFF_VA, W_KV)
    gates_ref[...] = _sigmoid(proj(OFF_G, 3 * d_model) + bm_ref[...])
    z = proj(OFF_G + 3 * d_model, LANES) + bif_ref[...]
    lane = lax.broadcasted_iota(jnp.int32, z.shape, 1)
    ifg_ref[...] = jnp.where(lane < NH_M, z, _log_sigmoid(z))


def _in_proj(x, g, w, bm, bif, tm):
    rows, d_model = x.shape
    widths = (3 * W_M, W_M, C_CONV, W_Q, W_KV, W_KV, 3 * d_model, LANES)
    row_spec = lambda n: pl.BlockSpec((tm, n), lambda i: (i, 0))
    return pl.pallas_call(
        _in_proj_kernel,
        out_shape=[jax.ShapeDtypeStruct((rows, n), F32) for n in widths],
        grid=(rows // tm,),
        in_specs=[row_spec(d_model), _const_spec(g.shape), _const_spec(w.shape),
                  _const_spec(bm.shape), _const_spec(bif.shape)],
        out_specs=[row_spec(n) for n in widths],
        compiler_params=_cparams(("arbitrary",)),
        name="in_proj",
    )(x, g, w, bm, bif)


def _mlstm_kernel(nvalid, qkv_ref, ifg_ref, og_ref, c0_ref, n0_ref, m0_ref,
                  hg_ref, c_ref, n_ref, m_ref):
    @pl.when(pl.program_id(1) == 0)
    def _():
        c_ref[...] = c0_ref[...]
        n_ref[...] = n0_ref[...]
        m_ref[...] = m0_ref[...]

    row = lax.broadcasted_iota(jnp.int32, (BLK, LANES), 0)
    lane = lax.broadcasted_iota(jnp.int32, (BLK, LANES), 1)
    gate = ifg_ref[...]
    if nvalid < BLK:
        gate = jnp.where(row >= nvalid, jnp.where(lane < NH_M, -jnp.inf, 0.0), gate)
    csum = jnp.where(lane < NH_M, 0.0, gate)
    shift = 1
    while shift < BLK:
        csum = csum + jnp.where(row >= shift, pltpu.roll(csum, shift, axis=0), 0.0)
        shift *= 2
    cols = jnp.where(lane < NH_M, gate, csum)
    rows_t = cols.T
    causal = (lax.broadcasted_iota(jnp.int32, (BLK, BLK), 0)
              >= lax.broadcasted_iota(jnp.int32, (BLK, BLK), 1))
    scale = DH_M ** -0.5

    for hd in range(NH_M):
        sl = slice(hd * DH_M, (hd + 1) * DH_M)
        q = qkv_ref[:, sl]
        k = qkv_ref[:, W_M + hd * DH_M:W_M + (hd + 1) * DH_M] * scale
        v = qkv_ref[:, 2 * W_M + hd * DH_M:2 * W_M + (hd + 1) * DH_M]
        c = c_ref[hd]
        n_row = n_ref[hd:hd + 1, :]
        m_prev = m_ref[hd:hd + 1, 0:1]
        ig_col = cols[:, hd:hd + 1]
        b_col = cols[:, NH_M + hd:NH_M + hd + 1]
        ig_row = rows_t[hd:hd + 1, :]
        b_row = rows_t[NH_M + hd:NH_M + hd + 1, :]
        b_last = b_col[BLK - 1:BLK, :]

        log_d = jnp.where(causal, b_col - b_row + ig_row, -jnp.inf)
        log_prev = b_col + m_prev
        m_t = jnp.maximum(log_prev, jnp.max(log_d, axis=-1, keepdims=True))
        dmat = jnp.exp(log_d - m_t)
        w_prev = jnp.exp(log_prev - m_t)
        s = _dot_nt(q, k) * dmat
        num = _dot(s, v) + w_prev * _dot(q, c)
        den = jnp.sum(s, axis=-1, keepdims=True) + w_prev * jnp.sum(q * n_row, axis=-1, keepdims=True)
        h = num / jnp.maximum(jnp.abs(den), jnp.exp(-m_t))
        hg_ref[:, sl] = og_ref[:, sl] * h

        log_w_row = b_last - b_row + ig_row
        m_new = jnp.maximum(b_last + m_prev, jnp.max(log_w_row, axis=-1, keepdims=True))
        wts_col = jnp.exp(b_last - b_col + ig_col - m_new)
        decay = jnp.exp(b_last + m_prev - m_new)
        kw = k * wts_col
        c_ref[hd] = decay * c + _dot(kw.T, v)
        n_ref[hd:hd + 1, :] = decay * n_row + jnp.sum(kw, axis=0, keepdims=True)
        m_ref[hd:hd + 1, :] = jnp.broadcast_to(m_new, (1, LANES))


def _mlstm(qkvm, ifg, og, c0, n0, m0, nseq, nblk, nvalid):
    rows = qkvm.shape[0]
    per_seq = c0.shape[0] > 1
    blk_spec = lambda n: pl.BlockSpec((BLK, n), lambda b, j: (b * nblk + j, 0))
    init_idx = (lambda b, j: (b, 0, 0, 0)) if per_seq else (lambda b, j: (0, 0, 0, 0))
    init_idx3 = (lambda b, j: (b, 0, 0)) if per_seq else (lambda b, j: (0, 0, 0))
    return pl.pallas_call(
        functools.partial(_mlstm_kernel, nvalid),
        out_shape=[jax.ShapeDtypeStruct((rows, W_M), F32),
                   jax.ShapeDtypeStruct((nseq, NH_M, DH_M, DH_M), F32),
                   jax.ShapeDtypeStruct((nseq, SUBLANES, LANES), F32),
                   jax.ShapeDtypeStruct((nseq, SUBLANES, LANES), F32)],
        grid=(nseq, nblk),
        in_specs=[blk_spec(3 * W_M), blk_spec(LANES), blk_spec(W_M),
                  pl.BlockSpec((None, NH_M, DH_M, DH_M), init_idx),
                  pl.BlockSpec((None, SUBLANES, LANES), init_idx3),
                  pl.BlockSpec((None, SUBLANES, LANES), init_idx3)],
        out_specs=[blk_spec(W_M),
                   pl.BlockSpec((None, NH_M, DH_M, DH_M), lambda b, j: (b, 0, 0, 0)),
                   pl.BlockSpec((None, SUBLANES, LANES), lambda b, j: (b, 0, 0)),
                   pl.BlockSpec((None, SUBLANES, LANES), lambda b, j: (b, 0, 0))],
        compiler_params=_cparams(("arbitrary", "arbitrary")),
        name="mlstm",
    )(qkvm, ifg, og, c0, n0, m0)


def _mlstm_step_kernel(qkv_ref, ifg_ref, og_ref, c_ref, n_ref, m_ref,
                       hg_ref, c_out, n_out, m_out):
    gate = ifg_ref[...]
    ig = gate
    lf = pltpu.roll(gate, LANES - NH_M, axis=1)
    log_prev = lf + m_ref[...]
    m_t = jnp.maximum(log_prev, ig)
    dmat = jnp.exp(ig - m_t)
    w_prev = jnp.exp(log_prev - m_t)
    floor = jnp.exp(-m_t)
    m_out[...] = m_t
    scale = DH_M ** -0.5

    qs, kds, vs, ss = [], [], [], []
    for hd in range(NH_M):
        q = qkv_ref[:, hd * DH_M:(hd + 1) * DH_M]
        k = qkv_ref[:, W_M + hd * DH_M:W_M + (hd + 1) * DH_M] * scale
        v = qkv_ref[:, 2 * W_M + hd * DH_M:2 * W_M + (hd + 1) * DH_M]
        d_h = dmat[:, hd:hd + 1]
        qs.append(q)
        kds.append(k * d_h)
        vs.append(v)
        ss.append(jnp.sum(q * k, axis=-1, keepdims=True) * d_h)
    stacked = jnp.concatenate(qs + kds + [jnp.zeros((BLK - 2 * NH_M * DEC_ROWS, DH_M), F32)], axis=0)
    cols = stacked.T

    for r in range(DEC_ROWS):
        for hd in range(NH_M):
            q_col = cols[:, hd * DEC_ROWS + r:hd * DEC_ROWS + r + 1]
            k_col = cols[:, (NH_M + hd) * DEC_ROWS + r:(NH_M + hd) * DEC_ROWS + r + 1]
            c = c_ref[r, hd]
            n_row = n_ref[r, hd:hd + 1, :]
            v_row = vs[hd][r:r + 1, :]
            wp = w_prev[r:r + 1, hd:hd + 1]
            s = ss[hd][r:r + 1, :]
            qc = jnp.sum(q_col * c, axis=0, keepdims=True)
            qn = jnp.sum(qs[hd][r:r + 1, :] * n_row, axis=-1, keepdims=True)
            num = s * v_row + wp * qc
            den = s + wp * qn
            h = num / jnp.maximum(jnp.abs(den), floor[r:r + 1, hd:hd + 1])
            hg_ref[r:r + 1, hd * DH_M:(hd + 1) * DH_M] = og_ref[r:r + 1, hd * DH_M:(hd + 1) * DH_M] * h
            c_out[r, hd] = wp * c + k_col * v_row
            n_out[r, hd:hd + 1, :] = wp * n_row + kds[hd][r:r + 1, :]


def _mlstm_step(qkvm, ifg, og, c, n, m):
    nseq = qkvm.shape[0]
    row_spec = lambda w: pl.BlockSpec((DEC_ROWS, w), lambda i: (i, 0))
    c_spec = pl.BlockSpec((DEC_ROWS, NH_M, DH_M, DH_M), lambda i: (i, 0, 0, 0))
    n_spec = pl.BlockSpec((DEC_ROWS, NH_M, DH_M), lambda i: (i, 0, 0))
    return pl.pallas_call(
        _mlstm_step_kernel,
        out_shape=[jax.ShapeDtypeStruct((nseq, W_M), F32),
                   jax.ShapeDtypeStruct(c.shape, F32),
                   jax.ShapeDtypeStruct(n.shape, F32),
                   jax.ShapeDtypeStruct((nseq, LANES), F32)],
        grid=(nseq // DEC_ROWS,),
        in_specs=[row_spec(3 * W_M), row_spec(LANES), row_spec(W_M), c_spec, n_spec, row_spec(LANES)],
        out_specs=[row_spec(W_M), c_spec, n_spec, row_spec(LANES)],
        compiler_params=_cparams(("arbitrary",)),
        name="mlstm_step",
    )(qkvm, ifg, og, c, n, m)


def _ln_silu(y, g, b):
    yc = y - jnp.mean(y, axis=-1, keepdims=True)
    var = jnp.mean(yc * yc, axis=-1, keepdims=True)
    return _silu(yc * lax.rsqrt(var + LN_EPS) * g + b)


def _conv_kernel(carry, u_ref, hist0_ref, w_ref, b_ref, g_ref, beta_ref, out_ref, buf):
    @pl.when(pl.program_id(1) == 0)
    def _():
        buf[0:HIST, :] = hist0_ref[...]

    buf[HIST:HIST + BLK, :] = u_ref[...]
    acc = jnp.broadcast_to(b_ref[...], (BLK, C_CONV))
    for tap in range(CONV_W):
        acc = acc + w_ref[tap:tap + 1, :] * buf[pl.ds(tap + HIST - (CONV_W - 1), BLK), :]
    out_ref[...] = _ln_silu(acc, g_ref[...], beta_ref[...])
    if carry:
        buf[0:HIST, :] = buf[BLK:BLK + HIST, :]


def _conv(u, hist0, w, b, g, beta, nseq, nblk):
    blk_spec = pl.BlockSpec((BLK, C_CONV), lambda s, j: (s * nblk + j, 0))
    return pl.pallas_call(
        functools.partial(_conv_kernel, nblk > 1),
        out_shape=jax.ShapeDtypeStruct(u.shape, F32),
        grid=(nseq, nblk),
        in_specs=[blk_spec, _const_spec(hist0.shape), _const_spec(w.shape), _const_spec(b.shape),
                  _const_spec(g.shape), _const_spec(beta.shape)],
        out_specs=blk_spec,
        scratch_shapes=[pltpu.VMEM((HIST + BLK, C_CONV), F32)],
        compiler_params=_cparams(("arbitrary", "arbitrary")),
        name="dwconv",
    )(u, hist0, w, b, g, beta)


def _conv_step_kernel(st_ref, u_ref, w_ref, b_ref, g_ref, beta_ref, out_ref):
    w_hist = w_ref[0:CONV_W - 1, :]
    rows = []
    for r in range(DEC_ROWS):
        rows.append(jnp.sum(st_ref[r] * w_hist, axis=0, keepdims=True))
    acc = jnp.concatenate(rows, axis=0) + w_ref[CONV_W - 1:CONV_W, :] * u_ref[...] + b_ref[...]
    out_ref[...] = _ln_silu(acc, g_ref[...], beta_ref[...])


def _conv_step(state, u, w, b, g, beta):
    nseq = u.shape[0]
    return pl.pallas_call(
        _conv_step_kernel,
        out_shape=jax.ShapeDtypeStruct(u.shape, F32),
        grid=(nseq // DEC_ROWS,),
        in_specs=[pl.BlockSpec((DEC_ROWS, CONV_W - 1, C_CONV), lambda i: (i, 0, 0)),
                  pl.BlockSpec((DEC_ROWS, C_CONV), lambda i: (i, 0)),
                  _const_spec(w.shape), _const_spec(b.shape), _const_spec(g.shape), _const_spec(beta.shape)],
        out_specs=pl.BlockSpec((DEC_ROWS, C_CONV), lambda i: (i, 0)),
        compiler_params=_cparams(("arbitrary",)),
        name="dwconv_step",
    )(state, u, w, b, g, beta)


def _rope(x, cos, sin_lo, sin_hi):
    width = x.shape[1]
    reps = width // LANES
    if reps > 1:
        cos, sin_lo, sin_hi = (jnp.concatenate([t] * reps, axis=1) for t in (cos, sin_lo, sin_hi))
    return (x * cos + pltpu.roll(x, width - ROT_HALF, axis=1) * sin_lo
            + pltpu.roll(x, ROT_HALF, axis=1) * sin_hi)


def _attn_kernel(base, carry, q_ref, k_ref, v_ref, qc_ref, qs1_ref, qs2_ref, kc_ref, ks1_ref, ks2_ref,
                 k0_ref, v0_ref, pos0_ref, sink_ref, o_ref, krot_ref, kprev, vprev):
    j = pl.program_id(1)

    @pl.when(j == 0)
    def _():
        kprev[...] = k0_ref[...]
        vprev[...] = v0_ref[...]

    lq = q_ref.shape[0]
    lk = k_ref.shape[0]
    q = _rope(q_ref[...], qc_ref[...], qs1_ref[...], qs2_ref[...])
    k = _rope(k_ref[...], kc_ref[...], ks1_ref[...], ks2_ref[...])
    v = v_ref[...]
    krot_ref[...] = k

    qpos = base + lk * j + lax.broadcasted_iota(jnp.int32, (lq, 1), 0)
    kpos_cur = base + lk * j + lax.broadcasted_iota(jnp.int32, (1, lk), 1)
    kpos_prev = jnp.where(j == 0, pos0_ref[...],
                          base + lk * j - BLK + lax.broadcasted_iota(jnp.int32, (1, BLK), 1))

    def visible(kpos):
        return (kpos >= 0) & (kpos <= qpos) & (qpos - kpos <= WINDOW)

    mask_cur = visible(kpos_cur)
    mask_prev = visible(kpos_prev)
    scale = DH_A ** -0.5
    kp_all = kprev[...]
    vp_all = vprev[...]
    outs = []
    for g in range(KV_A):
        gs = slice(g * DH_A, (g + 1) * DH_A)
        kp, vp, kc, vc = kp_all[:, gs], vp_all[:, gs], k[:, gs], v[:, gs]
        for r in range(REP_A):
            hd = g * REP_A + r
            qh = q[:, hd * DH_A:(hd + 1) * DH_A]
            sp = jnp.where(mask_prev, _dot_nt(qh, kp) * scale, -jnp.inf)
            sc = jnp.where(mask_cur, _dot_nt(qh, kc) * scale, -jnp.inf)
            sk = sink_ref[0:1, hd:hd + 1]
            mx = jnp.maximum(jnp.maximum(jnp.max(sp, axis=-1, keepdims=True),
                                         jnp.max(sc, axis=-1, keepdims=True)), sk)
            pp = jnp.exp(sp - mx)
            pc = jnp.exp(sc - mx)
            den = (jnp.sum(pp, axis=-1, keepdims=True) + jnp.sum(pc, axis=-1, keepdims=True)
                   + jnp.exp(sk - mx))
            inv = 1.0 / den
            outs.append(_dot(pp * inv, vp) + _dot(pc * inv, vc))
    o_ref[...] = jnp.concatenate(outs, axis=1)
    if carry:
        kprev[...] = k
        vprev[...] = v


def _attn(q, k, v, qtab, ktab, k0, v0, pos0, sinks, nseq, nblk, lq, lk, base):
    per_seq = k0.shape[0] > 1
    q_spec = lambda n: pl.BlockSpec((lq, n), lambda s, j: (s * nblk + j, 0))
    k_spec = lambda n: pl.BlockSpec((lk, n), lambda s, j: (s * nblk + j, 0))
    qt_spec = pl.BlockSpec((lq, LANES), lambda s, j: (j, 0))
    kt_spec = pl.BlockSpec((lk, LANES), lambda s, j: (j, 0))
    p_idx = (lambda s, j: (s, 0, 0)) if per_seq else (lambda s, j: (0, 0, 0))
    p_spec = pl.BlockSpec((None, BLK, W_KV), p_idx)
    return pl.pallas_call(
        functools.partial(_attn_kernel, base, nblk > 1),
        out_shape=[jax.ShapeDtypeStruct(q.shape, F32), jax.ShapeDtypeStruct(k.shape, F32)],
        grid=(nseq, nblk),
        in_specs=[q_spec(W_Q), k_spec(W_KV), k_spec(W_KV), qt_spec, qt_spec, qt_spec,
                  kt_spec, kt_spec, kt_spec, p_spec, p_spec,
                  _const_spec(pos0.shape), _const_spec(sinks.shape)],
        out_specs=[q_spec(W_Q), k_spec(W_KV)],
        scratch_shapes=[pltpu.VMEM((BLK, W_KV), F32), pltpu.VMEM((BLK, W_KV), F32)],
        compiler_params=_cparams(("arbitrary", "arbitrary")),
        name="swa",
    )(q, k, v, *qtab, *ktab, k0, v0, pos0, sinks)


def _rope_tables(pos):
    inv = jnp.power(ROPE_THETA, -jnp.arange(ROT_HALF, dtype=F32) / ROT_HALF)
    ang = pos.astype(F32)[:, None] * inv[None, :]
    cos, sin = jnp.cos(ang), jnp.sin(ang)
    npos = pos.shape[0]
    ones = jnp.ones((npos, DH_A - ROT_DIM), F32)
    zeros = jnp.zeros((npos, DH_A - ROT_DIM), F32)
    zh = jnp.zeros((npos, ROT_HALF), F32)
    head = lambda parts: jnp.concatenate(parts * (LANES // DH_A), axis=1)
    return (head([cos, cos, ones]), head([-sin, zh, zeros]), head([zh, sin, zeros]))


def _merge_kernel(x_ref, hg_ref, cv_ref, oa_ref, gates_ref, wm_ref, wc_ref, wa_ref, wo_ref, out_ref):
    d_model = x_ref.shape[1]
    mix = (gates_ref[:, 0:d_model] * _dot(hg_ref[...], wm_ref[...])
           + gates_ref[:, d_model:2 * d_model] * _dot(cv_ref[...], wc_ref[...])
           + gates_ref[:, 2 * d_model:3 * d_model] * _dot(oa_ref[...], wa_ref[...]))
    out_ref[...] = x_ref[...] + _dot(mix, wo_ref[...])


def _merge(x, hg, cv, oa, gates, wm, wc, wa, wo, tm):
    rows, d_model = x.shape
    row_spec = lambda n: pl.BlockSpec((tm, n), lambda i: (i, 0))
    return pl.pallas_call(
        _merge_kernel,
        out_shape=jax.ShapeDtypeStruct(x.shape, F32),
        grid=(rows // tm,),
        in_specs=[row_spec(d_model), row_spec(W_M), row_spec(C_CONV), row_spec(W_Q), row_spec(3 * d_model),
                  _const_spec(wm.shape), _const_spec(wc.shape), _const_spec(wa.shape), _const_spec(wo.shape)],
        out_specs=row_spec(d_model),
        compiler_params=_cparams(("arbitrary",)),
        name="merge",
    )(x, hg, cv, oa, gates, wm, wc, wa, wo)


def _ffn_body(x, g2_ref, wg_ref, wu_ref, wd_ref, wc_ref, bc_ref, prev_rows):
    d_ff = wg_ref.shape[1]
    nchunk = 2 if d_ff % (2 * LANES) == 0 else 1
    cw = d_ff // nchunk
    h2 = _rms(x, g2_ref[...]).astype(BF16)
    acc = x
    gps = []
    for c in range(nchunk):
        lo, hi = c * cw, (c + 1) * cw
        gp = jnp.dot(h2, wg_ref[:, lo:hi], preferred_element_type=F32)
        up = jnp.dot(h2, wu_ref[:, lo:hi], preferred_element_type=F32)
        p2, p1 = prev_rows(gp, lo, hi)
        gc = (wc_ref[0:1, lo:hi] * p2 + wc_ref[1:2, lo:hi] * p1 + wc_ref[2:3, lo:hi] * gp
              + bc_ref[:, lo:hi])
        acc = acc + _dot(_silu(gc) * up, wd_ref[lo:hi, :])
        gps.append(gp)
    return acc, gps


def _ffn_seq_kernel(tiles_per_seq, tail_end, final, x_ref, g2_ref, wg_ref, wu_ref, wd_ref, wc_ref, bc_ref,
                    init_ref, gf_ref, out_ref, tail_ref, carry):
    @pl.when(pl.program_id(0) % tiles_per_seq == 0)
    def _():
        carry[...] = init_ref[...]

    tm = x_ref.shape[0]
    row = lax.broadcasted_iota(jnp.int32, (tm, 1), 0)

    def prev_rows(gp, lo, hi):
        c2 = carry[SUBLANES - 2:SUBLANES - 1, lo:hi]
        c1 = carry[SUBLANES - 1:SUBLANES, lo:hi]
        p1 = jnp.where(row == 0, c1, pltpu.roll(gp, 1, axis=0))
        p2 = jnp.where(row == 0, c2, jnp.where(row == 1, c1, pltpu.roll(gp, 2, axis=0)))
        return p2, p1

    acc, gps = _ffn_body(x_ref[...], g2_ref, wg_ref, wu_ref, wd_ref, wc_ref, bc_ref, prev_rows)
    out_ref[...] = _rms(acc, gf_ref[...]) if final else acc
    tail = jnp.concatenate([gp[tail_end - SUBLANES:tail_end, :] for gp in gps], axis=1)
    tail_ref[...] = tail
    if tiles_per_seq > 1:
        carry[...] = tail


def _ffn_seq(x, g2, wg, wu, wd, wc, bc, init, gf, tm, tiles_per_seq, tail_end, final):
    rows, d_model = x.shape
    d_ff = wg.shape[1]
    assert tiles_per_seq == 1 or tail_end == tm
    return pl.pallas_call(
        functools.partial(_ffn_seq_kernel, tiles_per_seq, tail_end, final),
        out_shape=[jax.ShapeDtypeStruct(x.shape, F32),
                   jax.ShapeDtypeStruct((rows // tm, SUBLANES, d_ff), F32)],
        grid=(rows // tm,),
        in_specs=[pl.BlockSpec((tm, d_model), lambda i: (i, 0)), _const_spec(g2.shape),
                  _const_spec(wg.shape), _const_spec(wu.shape), _const_spec(wd.shape),
                  _const_spec(wc.shape), _const_spec(bc.shape), _const_spec(init.shape),
                  _const_spec(gf.shape)],
        out_specs=[pl.BlockSpec((tm, d_model), lambda i: (i, 0)),
                   pl.BlockSpec((None, SUBLANES, d_ff), lambda i: (i, 0, 0))],
        scratch_shapes=[pltpu.VMEM((SUBLANES, d_ff), F32)],
        compiler_params=_cparams(("arbitrary",)),
        name="convffn",
    )(x, g2, wg, wu, wd, wc, bc, init, gf)


def _ffn_step_kernel(final, x_ref, g2_ref, wg_ref, wu_ref, wd_ref, wc_ref, bc_ref,
                     p2_ref, p1_ref, gf_ref, out_ref, gp_ref):
    def prev_rows(gp, lo, hi):
        return p2_ref[:, lo:hi], p1_ref[:, lo:hi]

    acc, gps = _ffn_body(x_ref[...], g2_ref, wg_ref, wu_ref, wd_ref, wc_ref, bc_ref, prev_rows)
    out_ref[...] = _rms(acc, gf_ref[...]) if final else acc
    gp_ref[...] = jnp.concatenate(gps, axis=1)


def _ffn_step(x, g2, wg, wu, wd, wc, bc, p2, p1, gf, tm, final):
    rows, d_model = x.shape
    d_ff = wg.shape[1]
    row_spec = lambda n: pl.BlockSpec((tm, n), lambda i: (i, 0))
    return pl.pallas_call(
        functools.partial(_ffn_step_kernel, final),
        out_shape=[jax.ShapeDtypeStruct(x.shape, F32), jax.ShapeDtypeStruct((rows, d_ff), F32)],
        grid=(rows // tm,),
        in_specs=[row_spec(d_model), _const_spec(g2.shape),
                  _const_spec(wg.shape), _const_spec(wu.shape), _const_spec(wd.shape),
                  _const_spec(wc.shape), _const_spec(bc.shape), row_spec(d_ff), row_spec(d_ff),
                  _const_spec(gf.shape)],
        out_specs=[row_spec(d_model), row_spec(d_ff)],
        compiler_params=_cparams(("arbitrary",)),
        name="convffn_step",
    )(x, g2, wg, wu, wd, wc, bc, p2, p1, gf)


def _row_tile(rows, cap):
    tm = min(rows, cap)
    while rows % tm:
        tm //= 2
    return tm


def kernel(x_prompt, x_sample, state_mlstm_c, state_mlstm_n, state_mlstm_m, state_conv, cache_swa_k, cache_swa_v, state_ffn_conv, meta_tokens, norm1_g, w_in, b_igate, b_fgate, w_mlstm_out, w_dconv, b_dconv, ln_conv_g, ln_conv_b, w_conv_out, attn_sinks, w_attn_out, b_merge, w_out, norm2_g, w_ffn_gate, w_ffn_up, w_ffn_conv, b_ffn_conv, w_ffn_down, final_norm_g):
    bsz, seq, d_model = x_prompt.shape
    nseq_s = x_sample.shape[0]
    depth = w_in.shape[0]
    d_ff = w_ffn_gate.shape[2]
    nblk = seq // BLK
    assert seq % BLK == 0 and x_sample.shape[1] == 1 and nseq_s % DEC_ROWS == 0
    assert cache_swa_k.shape[2] == WINDOW

    x_main = x_prompt.reshape(bsz * seq, d_model)
    x_meta = jnp.concatenate([meta_tokens.astype(F32), jnp.zeros((BLK - N_META, d_model), F32)], axis=0)
    x_smp = x_sample.reshape(nseq_s, d_model)

    tm_main = _row_tile(bsz * seq, 256)
    tm_ffn = _row_tile(seq, 256)
    tm_smp = _row_tile(nseq_s, 128)

    tab_meta = _rope_tables(jnp.arange(BLK))
    tab_main = _rope_tables(N_META + jnp.arange(seq))
    tab_smp_q = _rope_tables(PAST_LEN + jnp.arange(DEC_ROWS))
    tab_smp_k = _rope_tables(PAST_LEN + jnp.arange(BLK))
    iota_blk = jnp.arange(BLK, dtype=jnp.int32)[None, :]
    pos0_meta = jnp.full((1, BLK), -1, jnp.int32)
    pos0_main = jnp.where(iota_blk < N_META, iota_blk, -1)
    pos0_smp = PAST_LEN - WINDOW + iota_blk

    zeros_c = jnp.zeros((1, NH_M, DH_M, DH_M), F32)
    zeros_nm = jnp.zeros((1, SUBLANES, LANES), F32)
    zeros_kv = jnp.zeros((1, BLK, W_KV), F32)
    gf = final_norm_g.reshape(1, d_model)

    p_states = [[] for _ in range(7)]
    s_states = [[] for _ in range(7)]
    for l in range(depth):
        final = l == depth - 1
        wl = w_in[l]
        w_pack = jnp.concatenate(
            [wl[:, :3 * W_M], wl[:, 3 * W_M + 2 * NH_M:], wl[:, 3 * W_M:3 * W_M + 2 * NH_M],
             jnp.zeros((d_model, LANES - 2 * NH_M), F32)], axis=1).astype(BF16)
        bif = jnp.concatenate([b_igate[l], b_fgate[l], jnp.zeros((LANES - 2 * NH_M,), F32)]).reshape(1, LANES)
        g1 = norm1_g[l].reshape(1, d_model)
        bm = b_merge[l].reshape(1, 3 * d_model)
        w_dc = jnp.concatenate([w_dconv[l], jnp.zeros((HIST - CONV_W, C_CONV), F32)], axis=0)
        b_dc = b_dconv[l].reshape(1, C_CONV)
        ln_g = ln_conv_g[l].reshape(1, C_CONV)
        ln_b = ln_conv_b[l].reshape(1, C_CONV)
        sinks = jnp.concatenate([attn_sinks[l], jnp.zeros((LANES - H_A,), F32)]).reshape(1, LANES)
        wm, wc, wa, wo = (w.astype(BF16) for w in (w_mlstm_out[l], w_conv_out[l], w_attn_out[l], w_out[l]))
        g2 = norm2_g[l].reshape(1, d_model)
        wg, wu, wd = (w.astype(BF16) for w in (w_ffn_gate[l], w_ffn_up[l], w_ffn_down[l]))
        w_fc = jnp.concatenate([w_ffn_conv[l], jnp.zeros((SUBLANES - FFN_CONV_W, d_ff), F32)], axis=0)
        b_fc = b_ffn_conv[l].reshape(1, d_ff)

        qkvm, og, u, qa, ka, va, gates, ifg = _in_proj(x_meta, g1, w_pack, bm, bif, BLK)
        hg, c_meta, n_meta, m_meta = _mlstm(qkvm, ifg, og, zeros_c, zeros_nm, zeros_nm, 1, 1, N_META)
        cv = _conv(u, jnp.zeros((HIST, C_CONV), F32), w_dc, b_dc, ln_g, ln_b, 1, 1)
        oa, krot_meta = _attn(qa, ka, va, tab_meta, tab_meta, zeros_kv, zeros_kv, pos0_meta, sinks,
                              1, 1, BLK, BLK, 0)
        x1 = _merge(x_meta, hg, cv, oa, gates, wm, wc, wa, wo, BLK)
        x_meta, tail_meta = _ffn_seq(x1, g2, wg, wu, wd, w_fc, b_fc, jnp.zeros((SUBLANES, d_ff), F32), gf,
                                     BLK, 1, N_META, False)
        u_meta, va_meta = u, va

        qkvm, og, u, qa, ka, va, gates, ifg = _in_proj(x_main, g1, w_pack, bm, bif, tm_main)
        hg, p_c, p_n, p_m = _mlstm(qkvm, ifg, og, c_meta, n_meta, m_meta, bsz, nblk, BLK)
        hist0 = jnp.concatenate([jnp.zeros((HIST - N_META, C_CONV), F32), u_meta[:N_META]], axis=0)
        cv = _conv(u, hist0, w_dc, b_dc, ln_g, ln_b, bsz, nblk)
        oa, krot = _attn(qa, ka, va, tab_main, tab_main, krot_meta[None], va_meta[None], pos0_main, sinks,
                         bsz, nblk, BLK, BLK, N_META)
        x1 = _merge(x_main, hg, cv, oa, gates, wm, wc, wa, wo, tm_main)
        x_main, tails = _ffn_seq(x1, g2, wg, wu, wd, w_fc, b_fc, tail_meta[0], gf,
                                 tm_ffn, seq // tm_ffn, tm_ffn, final)
        p_states[0].append(p_c)
        p_states[1].append(p_n[:, :NH_M, :])
        p_states[2].append(p_m[:, :NH_M, 0])
        p_states[3].append(u.reshape(bsz, seq, C_CONV)[:, seq - (CONV_W - 1):])
        p_states[4].append(krot.reshape(bsz, seq, KV_A, DH_A)[:, seq - WINDOW:])
        p_states[5].append(va.reshape(bsz, seq, KV_A, DH_A)[:, seq - WINDOW:])
        p_states[6].append(tails.reshape(bsz, seq // tm_ffn, SUBLANES, d_ff)[:, -1, SUBLANES - (FFN_CONV_W - 1):])

        qkvm, og, u, qa, ka, va, gates, ifg = _in_proj(x_smp, g1, w_pack, bm, bif, tm_smp)
        m_in = jnp.concatenate([state_mlstm_m[l], jnp.zeros((nseq_s, LANES - NH_M), F32)], axis=1)
        hg, s_c, s_n, s_m = _mlstm_step(qkvm, ifg, og, state_mlstm_c[l], state_mlstm_n[l], m_in)
        cv = _conv_step(state_conv[l], u, w_dc, b_dc, ln_g, ln_b)
        pad_q = jnp.zeros((nseq_s, DEC_ROWS - 1, W_Q), F32)
        pad_kv = jnp.zeros((nseq_s, BLK - 1, W_KV), F32)
        qa_p = jnp.concatenate([qa[:, None], pad_q], axis=1).reshape(nseq_s * DEC_ROWS, W_Q)
        ka_p = jnp.concatenate([ka[:, None], pad_kv], axis=1).reshape(nseq_s * BLK, W_KV)
        va_p = jnp.concatenate([va[:, None], pad_kv], axis=1).reshape(nseq_s * BLK, W_KV)
        oa_p, krot_p = _attn(qa_p, ka_p, va_p, tab_smp_q, tab_smp_k,
                             cache_swa_k[l].reshape(nseq_s, WINDOW, W_KV),
                             cache_swa_v[l].reshape(nseq_s, WINDOW, W_KV), pos0_smp, sinks,
                             nseq_s, 1, DEC_ROWS, BLK, PAST_LEN)
        oa = oa_p.reshape(nseq_s, DEC_ROWS, W_Q)[:, 0]
        krot_s = krot_p.reshape(nseq_s, BLK, W_KV)[:, 0]
        x1 = _merge(x_smp, hg, cv, oa, gates, wm, wc, wa, wo, tm_smp)
        x_smp, gp_s = _ffn_step(x1, g2, wg, wu, wd, w_fc, b_fc, state_ffn_conv[l][:, 0], state_ffn_conv[l][:, 1],
                                gf, tm_smp, final)
        s_states[0].append(s_c)
        s_states[1].append(s_n)
        s_states[2].append(s_m[:, :NH_M])
        s_states[3].append(jnp.concatenate([state_conv[l][:, 1:], u[:, None]], axis=1))
        s_states[4].append(jnp.concatenate(
            [cache_swa_k[l][:, 1:], krot_s.reshape(nseq_s, 1, KV_A, DH_A)], axis=1))
        s_states[5].append(jnp.concatenate(
            [cache_swa_v[l][:, 1:], va.reshape(nseq_s, 1, KV_A, DH_A)], axis=1))
        s_states[6].append(jnp.concatenate([state_ffn_conv[l][:, 1:], gp_s[:, None]], axis=1))

    y_prompt = x_main.reshape(bsz, seq, d_model)
    y_sample = x_smp.reshape(nseq_s, 1, d_model)
    return (y_prompt, y_sample, *(jnp.stack(a, axis=0) for a in p_states),
            *(jnp.stack(a, axis=0) for a in s_states))
```

```python
import functools

import jax
import jax.numpy as jnp
from jax import lax
from jax.experimental import pallas as pl
from jax.experimental.pallas import tpu as pltpu

F32 = jnp.float32
BF16 = jnp.bfloat16

N_META = 16
NH_M = 4
DH_M = 128
W_M = NH_M * DH_M
C_CONV = 512
CONV_W = 31
H_A = 8
KV_A = 2
DH_A = 64
REP_A = H_A // KV_A
W_Q = H_A * DH_A
W_KV = KV_A * DH_A
ROT_DIM = DH_A // 4
ROT_HALF = ROT_DIM // 2
ROPE_THETA = 500000.0
WINDOW = 128
FFN_CONV_W = 3
PAST_LEN = 16384
RMS_EPS = 1e-6
LN_EPS = 1e-5

BLK = 128
LANES = 128
SUBLANES = 8
HIST = 32
DEC_ROWS = 8
VMEM_LIMIT = 52 * 1024 * 1024

OFF_QKVM = 0
OFF_O = OFF_QKVM + 3 * W_M
OFF_GLU = OFF_O + W_M
OFF_QA = OFF_GLU + 2 * C_CONV
OFF_KA = OFF_QA + W_Q
OFF_VA = OFF_KA + W_KV
OFF_G = OFF_VA + W_KV


def _cparams(sem):
    return pltpu.CompilerParams(dimension_semantics=sem, vmem_limit_bytes=VMEM_LIMIT)


def _const_spec(shape):
    nd = len(shape)
    return pl.BlockSpec(shape, lambda *_: (0,) * nd, pipeline_mode=pl.Buffered(1))


def _rms(x, g):
    return x * lax.rsqrt(jnp.mean(x * x, axis=-1, keepdims=True) + RMS_EPS) * g


def _log_sigmoid(x):
    return jnp.minimum(x, 0.0) - jnp.log1p(jnp.exp(-jnp.abs(x)))


def _sigmoid(x):
    return 1.0 / (1.0 + jnp.exp(-x))


def _silu(x):
    return x * _sigmoid(x)


def _dot(a, b):
    return jnp.dot(a.astype(BF16), b.astype(BF16), preferred_element_type=F32)


def _dot_nt(a, b):
    return lax.dot_general(a.astype(BF16), b.astype(BF16), (((1,), (1,)), ((), ())),
                           preferred_element_type=F32)


def _in_proj_kernel(x_ref, g_ref, w_ref, bm_ref, bif_ref,
                    qkvm_ref, og_ref, u_ref, qa_ref, ka_ref, va_ref, gates_ref, ifg_ref):
    d_model = x_ref.shape[1]
    h = _rms(x_ref[...], g_ref[...]).astype(BF16)

    def proj(off, n):
        return jnp.dot(h, w_ref[:, off:off + n], preferred_element_type=F32)

    qkvm_ref[...] = proj(OFF_QKVM, 3 * W_M)
    og_ref[...] = _sigmoid(proj(OFF_O, W_M))
    glu = proj(OFF_GLU, 2 * C_CONV)
    u_ref[...] = glu[:, :C_CONV] * _sigmoid(glu[:, C_CONV:])
    qa_ref[...] = proj(OFF_QA, W_Q)
    ka_ref[...] = proj(OFF_KA, W_KV)
    va_ref[...] = proj(OFF_VA, W_KV)
    gates_ref[...] = _sigmoid(proj(OFF_G, 3 * d_model) + bm_ref[...])
    z = proj(OFF_G + 3 * d_model, LANES) + bif_ref[...]
    lane = lax.broadcasted_iota(jnp.int32, z.shape, 1)
    ifg_ref[...] = jnp.where(lane < NH_M, z, _log_sigmoid(z))


def _in_proj(x, g, w, bm, bif, tm):
    rows, d_model = x.shape
    widths = (3 * W_M, W_M, C_CONV, W_Q, W_KV, W_KV, 3 * d_model, LANES)
    row_spec = lambda n: pl.BlockSpec((tm, n), lambda i: (i, 0))
    return pl.pallas_call(
        _in_proj_kernel,
        out_shape=[jax.ShapeDtypeStruct((rows, n), F32) for n in widths],
        grid=(rows // tm,),
        in_specs=[row_spec(d_model), _const_spec(g.shape), _const_spec(w.shape),
                  _const_spec(bm.shape), _const_spec(bif.shape)],
        out_specs=[row_spec(n) for n in widths],
        compiler_params=_cparams(("arbitrary",)),
        name="in_proj",
    )(x, g, w, bm, bif)


def _mlstm_kernel(nvalid, qkv_ref, ifg_ref, og_ref, c0_ref, n0_ref, m0_ref,
                  hg_ref, c_ref, n_ref, m_ref):
    @pl.when(pl.program_id(1) == 0)
    def _():
        c_ref[...] = c0_ref[...]
        n_ref[...] = n0_ref[...]
        m_ref[...] = m0_ref[...]

    row = lax.broadcasted_iota(jnp.int32, (BLK, LANES), 0)
    lane = lax.broadcasted_iota(jnp.int32, (BLK, LANES), 1)
    gate = ifg_ref[...]
    if nvalid < BLK:
        gate = jnp.where(row >= nvalid, jnp.where(lane < NH_M, -jnp.inf, 0.0), gate)
    csum = jnp.where(lane < NH_M, 0.0, gate)
    shift = 1
    while shift < BLK:
        csum = csum + jnp.where(row >= shift, pltpu.roll(csum, shift, axis=0), 0.0)
        shift *= 2
    cols = jnp.where(lane < NH_M, gate, csum)
    rows_t = cols.T
    causal = (lax.broadcasted_iota(jnp.int32, (BLK, BLK), 0)
              >= lax.broadcasted_iota(jnp.int32, (BLK, BLK), 1))
    scale = DH_M ** -0.5

    heads = []
    for hd in range(NH_M):
        sl = slice(hd * DH_M, (hd + 1) * DH_M)
        q = qkv_ref[:, sl]
        k = qkv_ref[:, W_M + hd * DH_M:W_M + (hd + 1) * DH_M] * scale
        v = qkv_ref[:, 2 * W_M + hd * DH_M:2 * W_M + (hd + 1) * DH_M].astype(BF16)
        c = c_ref[hd]
        n_row = n_ref[hd:hd + 1, :]
        m_prev = m_ref[hd:hd + 1, 0:1]
        ig_col = cols[:, hd:hd + 1]
        b_col = cols[:, NH_M + hd:NH_M + hd + 1]
        ig_row = rows_t[hd:hd + 1, :]
        b_row = rows_t[NH_M + hd:NH_M + hd + 1, :]
        b_last = b_col[BLK - 1:BLK, :]

        qk = _dot_nt(q, k)
        qc = _dot(q, c)
        qn = jnp.sum(q * n_row, axis=-1, keepdims=True)

        log_d = jnp.where(causal, b_col - b_row + ig_row, -jnp.inf)
        log_prev = b_col + m_prev
        m_t = jnp.maximum(log_prev, jnp.max(log_d, axis=-1, keepdims=True))
        dmat = jnp.exp(log_d - m_t)
        w_prev = jnp.exp(log_prev - m_t)

        log_w_row = b_last - b_row + ig_row
        m_new = jnp.maximum(b_last + m_prev, jnp.max(log_w_row, axis=-1, keepdims=True))
        wts_col = jnp.exp(b_last - b_col + ig_col - m_new)
        decay = jnp.exp(b_last + m_prev - m_new)
        kw = k * wts_col
        c_ref[hd] = decay * c + _dot(kw.T, v)
        n_ref[hd:hd + 1, :] = decay * n_row + jnp.sum(kw, axis=0, keepdims=True)
        m_ref[hd:hd + 1, :] = jnp.broadcast_to(m_new, (1, LANES))
        heads.append((sl, v, qk * dmat, w_prev, qc, qn, jnp.exp(-m_t)))

    for sl, v, s, w_prev, qc, qn, floor in heads:
        num = _dot(s, v) + w_prev * qc
        den = jnp.sum(s, axis=-1, keepdims=True) + w_prev * qn
        hg_ref[:, sl] = og_ref[:, sl] * (num / jnp.maximum(jnp.abs(den), floor))


def _mlstm(qkvm, ifg, og, c0, n0, m0, nseq, nblk, nvalid):
    rows = qkvm.shape[0]
    per_seq = c0.shape[0] > 1
    blk_spec = lambda n: pl.BlockSpec((BLK, n), lambda b, j: (b * nblk + j, 0))
    init_idx = (lambda b, j: (b, 0, 0, 0)) if per_seq else (lambda b, j: (0, 0, 0, 0))
    init_idx3 = (lambda b, j: (b, 0, 0)) if per_seq else (lambda b, j: (0, 0, 0))
    return pl.pallas_call(
        functools.partial(_mlstm_kernel, nvalid),
        out_shape=[jax.ShapeDtypeStruct((rows, W_M), F32),
                   jax.ShapeDtypeStruct((nseq, NH_M, DH_M, DH_M), F32),
                   jax.ShapeDtypeStruct((nseq, SUBLANES, LANES), F32),
                   jax.ShapeDtypeStruct((nseq, SUBLANES, LANES), F32)],
        grid=(nseq, nblk),
        in_specs=[blk_spec(3 * W_M), blk_spec(LANES), blk_spec(W_M),
                  pl.BlockSpec((None, NH_M, DH_M, DH_M), init_idx),
                  pl.BlockSpec((None, SUBLANES, LANES), init_idx3),
                  pl.BlockSpec((None, SUBLANES, LANES), init_idx3)],
        out_specs=[blk_spec(W_M),
                   pl.BlockSpec((None, NH_M, DH_M, DH_M), lambda b, j: (b, 0, 0, 0)),
                   pl.BlockSpec((None, SUBLANES, LANES), lambda b, j: (b, 0, 0)),
                   pl.BlockSpec((None, SUBLANES, LANES), lambda b, j: (b, 0, 0))],
        compiler_params=_cparams(("arbitrary", "arbitrary")),
        name="mlstm",
    )(qkvm, ifg, og, c0, n0, m0)


def _mlstm_step_kernel(qkv_ref, ifg_ref, og_ref, c_ref, n_ref, m_ref,
                       hg_ref, c_out, n_out, m_out):
    gate = ifg_ref[...]
    ig = gate
    lf = pltpu.roll(gate, LANES - NH_M, axis=1)
    log_prev = lf + m_ref[...]
    m_t = jnp.maximum(log_prev, ig)
    dmat = jnp.exp(ig - m_t)
    w_prev = jnp.exp(log_prev - m_t)
    floor = jnp.exp(-m_t)
    m_out[...] = m_t
    scale = DH_M ** -0.5

    qs, kds, vs, ss = [], [], [], []
    for hd in range(NH_M):
        q = qkv_ref[:, hd * DH_M:(hd + 1) * DH_M]
        k = qkv_ref[:, W_M + hd * DH_M:W_M + (hd + 1) * DH_M] * scale
        v = qkv_ref[:, 2 * W_M + hd * DH_M:2 * W_M + (hd + 1) * DH_M]
        d_h = dmat[:, hd:hd + 1]
        qs.append(q)
        kds.append(k * d_h)
        vs.append(v)
        ss.append(jnp.sum(q * k, axis=-1, keepdims=True) * d_h)
    stacked = jnp.concatenate(qs + kds + [jnp.zeros((BLK - 2 * NH_M * DEC_ROWS, DH_M), F32)], axis=0)
    cols = stacked.T

    for r in range(DEC_ROWS):
        for hd in range(NH_M):
            q_col = cols[:, hd * DEC_ROWS + r:hd * DEC_ROWS + r + 1]
            k_col = cols[:, (NH_M + hd) * DEC_ROWS + r:(NH_M + hd) * DEC_ROWS + r + 1]
            c = c_ref[r, hd]
            n_row = n_ref[r, hd:hd + 1, :]
            v_row = vs[hd][r:r + 1, :]
            wp = w_prev[r:r + 1, hd:hd + 1]
            s = ss[hd][r:r + 1, :]
            qc = jnp.sum(q_col * c, axis=0, keepdims=True)
            qn = jnp.sum(qs[hd][r:r + 1, :] * n_row, axis=-1, keepdims=True)
            num = s * v_row + wp * qc
            den = s + wp * qn
            h = num / jnp.maximum(jnp.abs(den), floor[r:r + 1, hd:hd + 1])
            hg_ref[r:r + 1, hd * DH_M:(hd + 1) * DH_M] = og_ref[r:r + 1, hd * DH_M:(hd + 1) * DH_M] * h
            c_out[r, hd] = wp * c + k_col * v_row
            n_out[r, hd:hd + 1, :] = wp * n_row + kds[hd][r:r + 1, :]


def _mlstm_step(qkvm, ifg, og, c, n, m):
    nseq = qkvm.shape[0]
    row_spec = lambda w: pl.BlockSpec((DEC_ROWS, w), lambda i: (i, 0))
    c_spec = pl.BlockSpec((DEC_ROWS, NH_M, DH_M, DH_M), lambda i: (i, 0, 0, 0))
    n_spec = pl.BlockSpec((DEC_ROWS, NH_M, DH_M), lambda i: (i, 0, 0))
    return pl.pallas_call(
        _mlstm_step_kernel,
        out_shape=[jax.ShapeDtypeStruct((nseq, W_M), F32),
                   jax.ShapeDtypeStruct(c.shape, F32),
                   jax.ShapeDtypeStruct(n.shape, F32),
                   jax.ShapeDtypeStruct((nseq, LANES), F32)],
        grid=(nseq // DEC_ROWS,),
        in_specs=[row_spec(3 * W_M), row_spec(LANES), row_spec(W_M), c_spec, n_spec, row_spec(LANES)],
        out_specs=[row_spec(W_M), c_spec, n_spec, row_spec(LANES)],
        compiler_params=_cparams(("arbitrary",)),
        name="mlstm_step",
    )(qkvm, ifg, og, c, n, m)


def _ln_silu(y, g, b):
    yc = y - jnp.mean(y, axis=-1, keepdims=True)
    var = jnp.mean(yc * yc, axis=-1, keepdims=True)
    return _silu(yc * lax.rsqrt(var + LN_EPS) * g + b)


def _conv_kernel(carry, u_ref, hist0_ref, w_ref, b_ref, g_ref, beta_ref, out_ref, buf):
    @pl.when(pl.program_id(1) == 0)
    def _():
        buf[0:HIST, :] = hist0_ref[...]

    buf[HIST:HIST + BLK, :] = u_ref[...]
    acc = jnp.broadcast_to(b_ref[...], (BLK, C_CONV))
    for tap in range(CONV_W):
        acc = acc + w_ref[tap:tap + 1, :] * buf[pl.ds(tap + HIST - (CONV_W - 1), BLK), :]
    out_ref[...] = _ln_silu(acc, g_ref[...], beta_ref[...])
    if carry:
        buf[0:HIST, :] = buf[BLK:BLK + HIST, :]


def _conv(u, hist0, w, b, g, beta, nseq, nblk):
    blk_spec = pl.BlockSpec((BLK, C_CONV), lambda s, j: (s * nblk + j, 0))
    return pl.pallas_call(
        functools.partial(_conv_kernel, nblk > 1),
        out_shape=jax.ShapeDtypeStruct(u.shape, F32),
        grid=(nseq, nblk),
        in_specs=[blk_spec, _const_spec(hist0.shape), _const_spec(w.shape), _const_spec(b.shape),
                  _const_spec(g.shape), _const_spec(beta.shape)],
        out_specs=blk_spec,
        scratch_shapes=[pltpu.VMEM((HIST + BLK, C_CONV), F32)],
        compiler_params=_cparams(("arbitrary", "arbitrary")),
        name="dwconv",
    )(u, hist0, w, b, g, beta)


def _conv_step_kernel(st_ref, u_ref, w_ref, b_ref, g_ref, beta_ref, out_ref):
    w_hist = w_ref[0:CONV_W - 1, :]
    rows = []
    for r in range(DEC_ROWS):
        rows.append(jnp.sum(st_ref[r] * w_hist, axis=0, keepdims=True))
    acc = jnp.concatenate(rows, axis=0) + w_ref[CONV_W - 1:CONV_W, :] * u_ref[...] + b_ref[...]
    out_ref[...] = _ln_silu(acc, g_ref[...], beta_ref[...])


def _conv_step(state, u, w, b, g, beta):
    nseq = u.shape[0]
    return pl.pallas_call(
        _conv_step_kernel,
        out_shape=jax.ShapeDtypeStruct(u.shape, F32),
        grid=(nseq // DEC_ROWS,),
        in_specs=[pl.BlockSpec((DEC_ROWS, CONV_W - 1, C_CONV), lambda i: (i, 0, 0)),
                  pl.BlockSpec((DEC_ROWS, C_CONV), lambda i: (i, 0)),
                  _const_spec(w.shape), _const_spec(b.shape), _const_spec(g.shape), _const_spec(beta.shape)],
        out_specs=pl.BlockSpec((DEC_ROWS, C_CONV), lambda i: (i, 0)),
        compiler_params=_cparams(("arbitrary",)),
        name="dwconv_step",
    )(state, u, w, b, g, beta)


def _rope(x, cos, sin_lo, sin_hi):
    width = x.shape[1]
    reps = width // LANES
    if reps > 1:
        cos, sin_lo, sin_hi = (jnp.concatenate([t] * reps, axis=1) for t in (cos, sin_lo, sin_hi))
    return (x * cos + pltpu.roll(x, width - ROT_HALF, axis=1) * sin_lo
            + pltpu.roll(x, ROT_HALF, axis=1) * sin_hi)


def _attn_kernel(base, carry, q_ref, k_ref, v_ref, qc_ref, qs1_ref, qs2_ref, kc_ref, ks1_ref, ks2_ref,
                 k0_ref, v0_ref, pos0_ref, sink_ref, o_ref, krot_ref, kprev, vprev):
    j = pl.program_id(1)
    nsb, lq, _ = q_ref.shape
    lk = k_ref.shape[1]

    if carry:
        @pl.when(j == 0)
        def _():
            kprev[...] = k0_ref[...]
            vprev[...] = v0_ref[...]

    rows = H_A * lq
    qpos = base + lk * j + (lax.broadcasted_iota(jnp.int32, (rows, 1), 0) & (lq - 1))
    kpos_prev = jnp.where(j == 0, pos0_ref[...],
                          base + lk * j - BLK + lax.broadcasted_iota(jnp.int32, (1, BLK), 1))
    kpos = jnp.concatenate([kpos_prev, base + lk * j + lax.broadcasted_iota(jnp.int32, (1, BLK), 1)], axis=1)
    mask = (kpos >= 0) & (kpos <= qpos) & (qpos - kpos <= WINDOW)
    low_half = lax.broadcasted_iota(jnp.int32, (lq, LANES), 1) < DH_A
    sink_col = jnp.concatenate(
        [jnp.broadcast_to(sink_ref[0:1, g * REP_A + r:g * REP_A + r + 1], (lq, 1))
         for r in range(REP_A) for g in range(KV_A)], axis=0)
    scale = DH_A ** -0.5

    for sb in range(nsb):
        q = _rope(q_ref[sb], qc_ref[...], qs1_ref[...], qs2_ref[...]) * scale
        k = _rope(k_ref[sb], kc_ref[...], ks1_ref[...], ks2_ref[...])
        v = v_ref[sb]
        krot_ref[sb] = k
        if lk < BLK:
            pad = jnp.zeros((BLK - lk, W_KV), F32)
            k_cur, v_cur = jnp.concatenate([k, pad], axis=0), jnp.concatenate([v, pad], axis=0)
        else:
            k_cur, v_cur = k, v
        k_prev, v_prev = (kprev[sb], vprev[sb]) if carry else (k0_ref[sb], v0_ref[sb])
        k_cat = jnp.concatenate([k_prev, k_cur], axis=0)
        v_cat = jnp.concatenate([v_prev, v_cur], axis=0)
        heads = []
        for r in range(REP_A):
            slab = q[:, r * LANES:(r + 1) * LANES]
            heads.append(jnp.where(low_half, slab, 0.0))
            heads.append(jnp.where(low_half, 0.0, slab))
        s = jnp.where(mask, _dot_nt(jnp.concatenate(heads, axis=0), k_cat), -jnp.inf)
        mx = jnp.maximum(jnp.max(s, axis=-1, keepdims=True), sink_col)
        p = jnp.exp(s - mx)
        den = jnp.sum(p, axis=-1, keepdims=True) + jnp.exp(sink_col - mx)
        o = _dot(p * (1.0 / den), v_cat)
        for r in range(REP_A):
            o_ref[sb, :, r * LANES:(r + 1) * LANES] = jnp.where(
                low_half, o[(2 * r) * lq:(2 * r + 1) * lq], o[(2 * r + 1) * lq:(2 * r + 2) * lq])
        if carry:
            kprev[sb] = k
            vprev[sb] = v


def _attn(q, k, v, qtab, ktab, k0, v0, pos0, sinks, nsb, nblk, base):
    nq, lq, _ = q.shape
    lk = k.shape[1]
    nseq = nq // nblk
    assert nsb == 1 or nblk == 1
    assert lq & (lq - 1) == 0 and (nblk == 1 or lq == lk == BLK)
    per_seq = k0.shape[0] > 1
    q_spec = lambda n: pl.BlockSpec((nsb, lq, n), lambda s, j: (s * nblk + j, 0, 0))
    k_spec = lambda n: pl.BlockSpec((nsb, lk, n), lambda s, j: (s * nblk + j, 0, 0))
    qt_spec = pl.BlockSpec((lq, LANES), lambda s, j: (j, 0))
    kt_spec = pl.BlockSpec((lk, LANES), lambda s, j: (j, 0))
    p_idx = (lambda s, j: (s, 0, 0)) if per_seq else (lambda s, j: (0, 0, 0))
    p_spec = pl.BlockSpec((nsb, BLK, W_KV), p_idx)
    return pl.pallas_call(
        functools.partial(_attn_kernel, base, nblk > 1),
        out_shape=[jax.ShapeDtypeStruct(q.shape, F32), jax.ShapeDtypeStruct(k.shape, F32)],
        grid=(nseq // nsb, nblk),
        in_specs=[q_spec(W_Q), k_spec(W_KV), k_spec(W_KV), qt_spec, qt_spec, qt_spec,
                  kt_spec, kt_spec, kt_spec, p_spec, p_spec,
                  _const_spec(pos0.shape), _const_spec(sinks.shape)],
        out_specs=[q_spec(W_Q), k_spec(W_KV)],
        scratch_shapes=[pltpu.VMEM((nsb, BLK, W_KV), F32), pltpu.VMEM((nsb, BLK, W_KV), F32)],
        compiler_params=_cparams(("arbitrary", "arbitrary")),
        name="swa",
    )(q, k, v, *qtab, *ktab, k0, v0, pos0, sinks)


def _rope_tables(pos):
    inv = jnp.power(ROPE_THETA, -jnp.arange(ROT_HALF, dtype=F32) / ROT_HALF)
    ang = pos.astype(F32)[:, None] * inv[None, :]
    cos, sin = jnp.cos(ang), jnp.sin(ang)
    npos = pos.shape[0]
    ones = jnp.ones((npos, DH_A - ROT_DIM), F32)
    zeros = jnp.zeros((npos, DH_A - ROT_DIM), F32)
    zh = jnp.zeros((npos, ROT_HALF), F32)
    head = lambda parts: jnp.concatenate(parts * (LANES // DH_A), axis=1)
    return (head([cos, cos, ones]), head([-sin, zh, zeros]), head([zh, sin, zeros]))


def _merge_kernel(x_ref, hg_ref, cv_ref, oa_ref, gates_ref, wm_ref, wc_ref, wa_ref, wo_ref, out_ref):
    d_model = x_ref.shape[1]
    mix = (gates_ref[:, 0:d_model] * _dot(hg_ref[...], wm_ref[...])
           + gates_ref[:, d_model:2 * d_model] * _dot(cv_ref[...], wc_ref[...])
           + gates_ref[:, 2 * d_model:3 * d_model] * _dot(oa_ref[...], wa_ref[...]))
    out_ref[...] = x_ref[...] + _dot(mix, wo_ref[...])


def _merge(x, hg, cv, oa, gates, wm, wc, wa, wo, tm):
    rows, d_model = x.shape
    row_spec = lambda n: pl.BlockSpec((tm, n), lambda i: (i, 0))
    return pl.pallas_call(
        _merge_kernel,
        out_shape=jax.ShapeDtypeStruct(x.shape, F32),
        grid=(rows // tm,),
        in_specs=[row_spec(d_model), row_spec(W_M), row_spec(C_CONV), row_spec(W_Q), row_spec(3 * d_model),
                  _const_spec(wm.shape), _const_spec(wc.shape), _const_spec(wa.shape), _const_spec(wo.shape)],
        out_specs=row_spec(d_model),
        compiler_params=_cparams(("arbitrary",)),
        name="merge",
    )(x, hg, cv, oa, gates, wm, wc, wa, wo)


def _ffn_body(x, g2_ref, wg_ref, wu_ref, wd_ref, wc_ref, bc_ref, prev_rows):
    d_ff = wg_ref.shape[1]
    nchunk = 2 if d_ff % (2 * LANES) == 0 else 1
    cw = d_ff // nchunk
    h2 = _rms(x, g2_ref[...]).astype(BF16)
    acc = x
    gps = []
    for c in range(nchunk):
        lo, hi = c * cw, (c + 1) * cw
        gp = jnp.dot(h2, wg_ref[:, lo:hi], preferred_element_type=F32)
        up = jnp.dot(h2, wu_ref[:, lo:hi], preferred_element_type=F32)
        p2, p1 = prev_rows(gp, lo, hi)
        gc = (wc_ref[0:1, lo:hi] * p2 + wc_ref[1:2, lo:hi] * p1 + wc_ref[2:3, lo:hi] * gp
              + bc_ref[:, lo:hi])
        acc = acc + _dot(_silu(gc) * up, wd_ref[lo:hi, :])
        gps.append(gp)
    return acc, gps


def _ffn_seq_kernel(tiles_per_seq, tail_end, final, x_ref, g2_ref, wg_ref, wu_ref, wd_ref, wc_ref, bc_ref,
                    init_ref, gf_ref, out_ref, tail_ref, carry):
    @pl.when(pl.program_id(0) % tiles_per_seq == 0)
    def _():
        carry[...] = init_ref[...]

    tm = x_ref.shape[0]
    row = lax.broadcasted_iota(jnp.int32, (tm, 1), 0)

    def prev_rows(gp, lo, hi):
        c2 = carry[SUBLANES - 2:SUBLANES - 1, lo:hi]
        c1 = carry[SUBLANES - 1:SUBLANES, lo:hi]
        p1 = jnp.where(row == 0, c1, pltpu.roll(gp, 1, axis=0))
        p2 = jnp.where(row == 0, c2, jnp.where(row == 1, c1, pltpu.roll(gp, 2, axis=0)))
        return p2, p1

    acc, gps = _ffn_body(x_ref[...], g2_ref, wg_ref, wu_ref, wd_ref, wc_ref, bc_ref, prev_rows)
    out_ref[...] = _rms(acc, gf_ref[...]) if final else acc
    tail = jnp.concatenate([gp[tail_end - SUBLANES:tail_end, :] for gp in gps], axis=1)
    tail_ref[...] = tail
    if tiles_per_seq > 1:
        carry[...] = tail


def _ffn_seq(x, g2, wg, wu, wd, wc, bc, init, gf, tm, tiles_per_seq, tail_end, final):
    rows, d_model = x.shape
    d_ff = wg.shape[1]
    assert tiles_per_seq == 1 or tail_end == tm
    return pl.pallas_call(
        functools.partial(_ffn_seq_kernel, tiles_per_seq, tail_end, final),
        out_shape=[jax.ShapeDtypeStruct(x.shape, F32),
                   jax.ShapeDtypeStruct((rows // tm, SUBLANES, d_ff), F32)],
        grid=(rows // tm,),
        in_specs=[pl.BlockSpec((tm, d_model), lambda i: (i, 0)), _const_spec(g2.shape),
                  _const_spec(wg.shape), _const_spec(wu.shape), _const_spec(wd.shape),
                  _const_spec(wc.shape), _const_spec(bc.shape), _const_spec(init.shape),
                  _const_spec(gf.shape)],
        out_specs=[pl.BlockSpec((tm, d_model), lambda i: (i, 0)),
                   pl.BlockSpec((None, SUBLANES, d_ff), lambda i: (i, 0, 0))],
        scratch_shapes=[pltpu.VMEM((SUBLANES, d_ff), F32)],
        compiler_params=_cparams(("arbitrary",)),
        name="convffn",
    )(x, g2, wg, wu, wd, wc, bc, init, gf)


def _ffn_step_kernel(final, x_ref, g2_ref, wg_ref, wu_ref, wd_ref, wc_ref, bc_ref,
                     p2_ref, p1_ref, gf_ref, out_ref, gp_ref):
    def prev_rows(gp, lo, hi):
        return p2_ref[:, lo:hi], p1_ref[:, lo:hi]

    acc, gps = _ffn_body(x_ref[...], g2_ref, wg_ref, wu_ref, wd_ref, wc_ref, bc_ref, prev_rows)
    out_ref[...] = _rms(acc, gf_ref[...]) if final else acc
    gp_ref[...] = jnp.concatenate(gps, axis=1)


def _ffn_step(x, g2, wg, wu, wd, wc, bc, p2, p1, gf, tm, final):
    rows, d_model = x.shape
    d_ff = wg.shape[1]
    row_spec = lambda n: pl.BlockSpec((tm, n), lambda i: (i, 0))
    return pl.pallas_call(
        functools.partial(_ffn_step_kernel, final),
        out_shape=[jax.ShapeDtypeStruct(x.shape, F32), jax.ShapeDtypeStruct((rows, d_ff), F32)],
        grid=(rows // tm,),
        in_specs=[row_spec(d_model), _const_spec(g2.shape),
                  _const_spec(wg.shape), _const_spec(wu.shape), _const_spec(wd.shape),
                  _const_spec(wc.shape), _const_spec(bc.shape), row_spec(d_ff), row_spec(d_ff),
                  _const_spec(gf.shape)],
        out_specs=[row_spec(d_model), row_spec(d_ff)],
        compiler_params=_cparams(("arbitrary",)),
        name="convffn_step",
    )(x, g2, wg, wu, wd, wc, bc, p2, p1, gf)


def _row_tile(rows, cap):
    tm = min(rows, cap)
    while rows % tm:
        tm //= 2
    return tm


def kernel(x_prompt, x_sample, state_mlstm_c, state_mlstm_n, state_mlstm_m, state_conv, cache_swa_k, cache_swa_v, state_ffn_conv, meta_tokens, norm1_g, w_in, b_igate, b_fgate, w_mlstm_out, w_dconv, b_dconv, ln_conv_g, ln_conv_b, w_conv_out, attn_sinks, w_attn_out, b_merge, w_out, norm2_g, w_ffn_gate, w_ffn_up, w_ffn_conv, b_ffn_conv, w_ffn_down, final_norm_g):
    bsz, seq, d_model = x_prompt.shape
    nseq_s = x_sample.shape[0]
    depth = w_in.shape[0]
    d_ff = w_ffn_gate.shape[2]
    nblk = seq // BLK
    assert seq % BLK == 0 and x_sample.shape[1] == 1 and nseq_s % DEC_ROWS == 0
    assert cache_swa_k.shape[2] == WINDOW

    x_main = x_prompt.reshape(bsz * seq, d_model)
    x_meta = jnp.concatenate([meta_tokens.astype(F32), jnp.zeros((BLK - N_META, d_model), F32)], axis=0)
    x_smp = x_sample.reshape(nseq_s, d_model)

    tm_main = _row_tile(bsz * seq, 256)
    tm_ffn = _row_tile(seq, 256)
    tm_smp = _row_tile(nseq_s, 128)

    tab_meta = _rope_tables(jnp.arange(BLK))
    tab_main = _rope_tables(N_META + jnp.arange(seq))
    tab_smp = _rope_tables(PAST_LEN + jnp.arange(1))
    q_perm = jnp.arange(W_Q).reshape(KV_A, REP_A, DH_A).transpose(1, 0, 2).reshape(W_Q)
    iota_blk = jnp.arange(BLK, dtype=jnp.int32)[None, :]
    pos0_meta = jnp.full((1, BLK), -1, jnp.int32)
    pos0_main = jnp.where(iota_blk < N_META, iota_blk, -1)
    pos0_smp = PAST_LEN - WINDOW + iota_blk

    zeros_c = jnp.zeros((1, NH_M, DH_M, DH_M), F32)
    zeros_nm = jnp.zeros((1, SUBLANES, LANES), F32)
    zeros_kv = jnp.zeros((1, BLK, W_KV), F32)
    gf = final_norm_g.reshape(1, d_model)

    p_states = [[] for _ in range(7)]
    s_states = [[] for _ in range(7)]
    for l in range(depth):
        final = l == depth - 1
        wl = w_in[l]
        col_gates = 3 * W_M
        col_o = col_gates + 2 * NH_M
        col_qa = col_o + W_M + 2 * C_CONV
        w_pack = jnp.concatenate(
            [wl[:, :col_gates], wl[:, col_o:col_qa], wl[:, col_qa:col_qa + W_Q][:, q_perm],
             wl[:, col_qa + W_Q:], wl[:, col_gates:col_o],
             jnp.zeros((d_model, LANES - 2 * NH_M), F32)], axis=1).astype(BF16)
        bif = jnp.concatenate([b_igate[l], b_fgate[l], jnp.zeros((LANES - 2 * NH_M,), F32)]).reshape(1, LANES)
        g1 = norm1_g[l].reshape(1, d_model)
        bm = b_merge[l].reshape(1, 3 * d_model)
        w_dc = jnp.concatenate([w_dconv[l], jnp.zeros((HIST - CONV_W, C_CONV), F32)], axis=0)
        b_dc = b_dconv[l].reshape(1, C_CONV)
        ln_g = ln_conv_g[l].reshape(1, C_CONV)
        ln_b = ln_conv_b[l].reshape(1, C_CONV)
        sinks = jnp.concatenate([attn_sinks[l], jnp.zeros((LANES - H_A,), F32)]).reshape(1, LANES)
        wm, wc, wa, wo = (w.astype(BF16) for w in (w_mlstm_out[l], w_conv_out[l], w_attn_out[l][q_perm], w_out[l]))
        g2 = norm2_g[l].reshape(1, d_model)
        wg, wu, wd = (w.astype(BF16) for w in (w_ffn_gate[l], w_ffn_up[l], w_ffn_down[l]))
        w_fc = jnp.concatenate([w_ffn_conv[l], jnp.zeros((SUBLANES - FFN_CONV_W, d_ff), F32)], axis=0)
        b_fc = b_ffn_conv[l].reshape(1, d_ff)

        qkvm, og, u, qa, ka, va, gates, ifg = _in_proj(x_meta, g1, w_pack, bm, bif, BLK)
        hg, c_meta, n_meta, m_meta = _mlstm(qkvm, ifg, og, zeros_c, zeros_nm, zeros_nm, 1, 1, N_META)
        cv = _conv(u, jnp.zeros((HIST, C_CONV), F32), w_dc, b_dc, ln_g, ln_b, 1, 1)
        oa, krot_meta = _attn(qa[None], ka[None], va[None], tab_meta, tab_meta, zeros_kv, zeros_kv,
                              pos0_meta, sinks, 1, 1, 0)
        x1 = _merge(x_meta, hg, cv, oa[0], gates, wm, wc, wa, wo, BLK)
        x_meta, tail_meta = _ffn_seq(x1, g2, wg, wu, wd, w_fc, b_fc, jnp.zeros((SUBLANES, d_ff), F32), gf,
                                     BLK, 1, N_META, False)
        u_meta, va_meta = u, va

        qkvm, og, u, qa, ka, va, gates, ifg = _in_proj(x_main, g1, w_pack, bm, bif, tm_main)
        hg, p_c, p_n, p_m = _mlstm(qkvm, ifg, og, c_meta, n_meta, m_meta, bsz, nblk, BLK)
        hist0 = jnp.concatenate([jnp.zeros((HIST - N_META, C_CONV), F32), u_meta[:N_META]], axis=0)
        cv = _conv(u, hist0, w_dc, b_dc, ln_g, ln_b, bsz, nblk)
        oa, krot = _attn(qa.reshape(bsz * nblk, BLK, W_Q), ka.reshape(bsz * nblk, BLK, W_KV),
                         va.reshape(bsz * nblk, BLK, W_KV), tab_main, tab_main, krot_meta, va_meta[None],
                         pos0_main, sinks, 1, nblk, N_META)
        x1 = _merge(x_main, hg, cv, oa.reshape(bsz * seq, W_Q), gates, wm, wc, wa, wo, tm_main)
        x_main, tails = _ffn_seq(x1, g2, wg, wu, wd, w_fc, b_fc, tail_meta[0], gf,
                                 tm_ffn, seq // tm_ffn, tm_ffn, final)
        p_states[0].append(p_c)
        p_states[1].append(p_n[:, :NH_M, :])
        p_states[2].append(p_m[:, :NH_M, 0])
        p_states[3].append(u.reshape(bsz, seq, C_CONV)[:, seq - (CONV_W - 1):])
        p_states[4].append(krot.reshape(bsz, seq, KV_A, DH_A)[:, seq - WINDOW:])
        p_states[5].append(va.reshape(bsz, seq, KV_A, DH_A)[:, seq - WINDOW:])
        p_states[6].append(tails.reshape(bsz, seq // tm_ffn, SUBLANES, d_ff)[:, -1, SUBLANES - (FFN_CONV_W - 1):])

        qkvm, og, u, qa, ka, va, gates, ifg = _in_proj(x_smp, g1, w_pack, bm, bif, tm_smp)
        m_in = jnp.concatenate([state_mlstm_m[l], jnp.zeros((nseq_s, LANES - NH_M), F32)], axis=1)
        hg, s_c, s_n, s_m = _mlstm_step(qkvm, ifg, og, state_mlstm_c[l], state_mlstm_n[l], m_in)
        cv = _conv_step(state_conv[l], u, w_dc, b_dc, ln_g, ln_b)
        oa, krot_s = _attn(qa[:, None], ka[:, None], va[:, None], tab_smp, tab_smp,
                           cache_swa_k[l].reshape(nseq_s, WINDOW, W_KV),
                           cache_swa_v[l].reshape(nseq_s, WINDOW, W_KV), pos0_smp, sinks,
                           DEC_ROWS, 1, PAST_LEN)
        x1 = _merge(x_smp, hg, cv, oa[:, 0], gates, wm, wc, wa, wo, tm_smp)
        x_smp, gp_s = _ffn_step(x1, g2, wg, wu, wd, w_fc, b_fc, state_ffn_conv[l][:, 0], state_ffn_conv[l][:, 1],
                                gf, tm_smp, final)
        s_states[0].append(s_c)
        s_states[1].append(s_n)
        s_states[2].append(s_m[:, :NH_M])
        s_states[3].append(jnp.concatenate([state_conv[l][:, 1:], u[:, None]], axis=1))
        s_states[4].append(jnp.concatenate(
            [cache_swa_k[l][:, 1:], krot_s.reshape(nseq_s, 1, KV_A, DH_A)], axis=1))
        s_states[5].append(jnp.concatenate(
            [cache_swa_v[l][:, 1:], va.reshape(nseq_s, 1, KV_A, DH_A)], axis=1))
        s_states[6].append(jnp.concatenate([state_ffn_conv[l][:, 1:], gp_s[:, None]], axis=1))

    y_prompt = x_main.reshape(bsz, seq, d_model)
    y_sample = x_smp.reshape(nseq_s, 1, d_model)
    return (y_prompt, y_sample, *(jnp.stack(a, axis=0) for a in p_states),
            *(jnp.stack(a, axis=0) for a in s_states))
```

```python
import functools

import jax
import jax.numpy as jnp
from jax import lax
from jax.experimental import pallas as pl
from jax.experimental.pallas import tpu as pltpu

F32 = jnp.float32
BF16 = jnp.bfloat16

N_META = 16
NH_M = 4
DH_M = 128
W_M = NH_M * DH_M
C_CONV = 512
CONV_W = 31
H_A = 8
KV_A = 2
DH_A = 64
REP_A = H_A // KV_A
W_Q = H_A * DH_A
W_KV = KV_A * DH_A
ROT_DIM = DH_A // 4
ROT_HALF = ROT_DIM // 2
ROPE_THETA = 500000.0
WINDOW = 128
FFN_CONV_W = 3
PAST_LEN = 16384
RMS_EPS = 1e-6
LN_EPS = 1e-5

BLK = 128
LANES = 128
SUBLANES = 8
HIST = 32
DEC_ROWS = 8
SEQ_PER_STEP = 2
VMEM_LIMIT = 52 * 1024 * 1024

OFF_QKVM = 0
OFF_O = OFF_QKVM + 3 * W_M
OFF_GLU = OFF_O + W_M
OFF_QA = OFF_GLU + 2 * C_CONV
OFF_KA = OFF_QA + W_Q
OFF_VA = OFF_KA + W_KV
OFF_G = OFF_VA + W_KV


def _cparams(sem):
    return pltpu.CompilerParams(dimension_semantics=sem, vmem_limit_bytes=VMEM_LIMIT)


def _const_spec(shape):
    nd = len(shape)
    return pl.BlockSpec(shape, lambda *_: (0,) * nd, pipeline_mode=pl.Buffered(1))


def _rms(x, g):
    return x * lax.rsqrt(jnp.mean(x * x, axis=-1, keepdims=True) + RMS_EPS) * g


def _log_sigmoid(x):
    return jnp.minimum(x, 0.0) - jnp.log1p(jnp.exp(-jnp.abs(x)))


def _sigmoid(x):
    return 1.0 / (1.0 + jnp.exp(-x))


def _silu(x):
    return x * _sigmoid(x)


def _dot(a, b):
    return jnp.dot(a.astype(BF16), b.astype(BF16), preferred_element_type=F32)


def _dot_nt(a, b):
    return lax.dot_general(a.astype(BF16), b.astype(BF16), (((1,), (1,)), ((), ())),
                           preferred_element_type=F32)


def _rope(x, cos, sin_lo, sin_hi):
    width = x.shape[1]
    reps = width // LANES
    if reps > 1:
        cos, sin_lo, sin_hi = (jnp.concatenate([t] * reps, axis=1) for t in (cos, sin_lo, sin_hi))
    return (x * cos + pltpu.roll(x, width - ROT_HALF, axis=1) * sin_lo
            + pltpu.roll(x, ROT_HALF, axis=1) * sin_hi)


def _rope_tables(pos):
    inv = jnp.power(ROPE_THETA, -jnp.arange(ROT_HALF, dtype=F32) / ROT_HALF)
    ang = pos.astype(F32)[:, None] * inv[None, :]
    cos, sin = jnp.cos(ang), jnp.sin(ang)
    npos = pos.shape[0]
    ones = jnp.ones((npos, DH_A - ROT_DIM), F32)
    zeros = jnp.zeros((npos, DH_A - ROT_DIM), F32)
    zh = jnp.zeros((npos, ROT_HALF), F32)
    head = lambda parts: jnp.concatenate(parts * (LANES // DH_A), axis=1)
    return (head([cos, cos, ones]), head([-sin, zh, zeros]), head([zh, sin, zeros]))


def _in_proj_kernel(x_ref, g_ref, w_ref, bm_ref, bif_ref, cos_ref, sin_lo_ref, sin_hi_ref,
                    qkvm_ref, og_ref, u_ref, qa_ref, ka_ref, va_ref, gates_ref, ifg_ref):
    d_model = x_ref.shape[1]
    h = _rms(x_ref[...], g_ref[...]).astype(BF16)
    act = qkvm_ref.dtype

    def proj(off, n):
        return jnp.dot(h, w_ref[:, off:off + n], preferred_element_type=F32)

    qkvm_ref[:, 0:W_M] = proj(OFF_QKVM, W_M).astype(act)
    qkvm_ref[:, W_M:2 * W_M] = (proj(OFF_QKVM + W_M, W_M) * DH_M ** -0.5).astype(act)
    qkvm_ref[:, 2 * W_M:3 * W_M] = proj(OFF_QKVM + 2 * W_M, W_M).astype(act)
    og_ref[...] = _sigmoid(proj(OFF_O, W_M)).astype(act)
    glu = proj(OFF_GLU, 2 * C_CONV)
    u_ref[...] = (glu[:, :C_CONV] * _sigmoid(glu[:, C_CONV:])).astype(act)
    tabs = (cos_ref[...], sin_lo_ref[...], sin_hi_ref[...])
    qa_ref[...] = _rope(proj(OFF_QA, W_Q), *tabs).astype(act)
    ka_ref[...] = _rope(proj(OFF_KA, W_KV), *tabs).astype(act)
    va_ref[...] = proj(OFF_VA, W_KV).astype(act)
    gates_ref[...] = _sigmoid(proj(OFF_G, 3 * d_model) + bm_ref[...]).astype(act)
    z = proj(OFF_G + 3 * d_model, LANES) + bif_ref[...]
    lane = lax.broadcasted_iota(jnp.int32, z.shape, 1)
    ifg_ref[...] = jnp.where(lane < NH_M, z, _log_sigmoid(z))


def _in_proj(x, g, w, bm, bif, tabs, tm, act):
    rows, d_model = x.shape
    ntab = tabs[0].shape[0] // tm
    widths = (3 * W_M, W_M, C_CONV, W_Q, W_KV, W_KV, 3 * d_model, LANES)
    dtypes = (act,) * 7 + (F32,)
    row_spec = lambda n: pl.BlockSpec((tm, n), lambda i: (i, 0))
    tab_spec = pl.BlockSpec((tm, LANES), lambda i: (i % ntab, 0))
    return pl.pallas_call(
        _in_proj_kernel,
        out_shape=[jax.ShapeDtypeStruct((rows, n), dt) for n, dt in zip(widths, dtypes)],
        grid=(rows // tm,),
        in_specs=[row_spec(d_model), _const_spec(g.shape), _const_spec(w.shape),
                  _const_spec(bm.shape), _const_spec(bif.shape), tab_spec, tab_spec, tab_spec],
        out_specs=[row_spec(n) for n in widths],
        compiler_params=_cparams(("arbitrary",)),
        name="in_proj",
    )(x, g, w, bm, bif, *tabs)


def _mlstm_kernel(nvalid, qkv_ref, ifg_ref, og_ref, c0_ref, n0_ref, m0_ref,
                  hg_ref, c_ref, n_ref, m_ref):
    @pl.when(pl.program_id(1) == 0)
    def _():
        c_ref[...] = jnp.broadcast_to(c0_ref[...], c_ref.shape)
        n_ref[...] = jnp.broadcast_to(n0_ref[...], n_ref.shape)
        m_ref[...] = jnp.broadcast_to(m0_ref[...], m_ref.shape)

    row = lax.broadcasted_iota(jnp.int32, (BLK, LANES), 0)
    lane = lax.broadcasted_iota(jnp.int32, (BLK, LANES), 1)
    causal = (lax.broadcasted_iota(jnp.int32, (BLK, BLK), 0)
              >= lax.broadcasted_iota(jnp.int32, (BLK, BLK), 1))

    for sb in range(qkv_ref.shape[0]):
        gate = jnp.where(lane < 2 * NH_M, ifg_ref[sb], 0.0)
        if nvalid < BLK:
            gate = jnp.where(row >= nvalid, jnp.where(lane < NH_M, -jnp.inf, 0.0), gate)
        csum = jnp.where(lane < NH_M, 0.0, gate)
        shift = 1
        while shift < BLK:
            csum = csum + jnp.where(row >= shift, pltpu.roll(csum, shift, axis=0), 0.0)
            shift *= 2
        b_all = pltpu.roll(csum, LANES - NH_M, axis=1)
        c_all = gate - b_all
        cmax = c_all
        shift = 1
        while shift < BLK:
            cmax = jnp.maximum(cmax, jnp.where(row >= shift, pltpu.roll(cmax, shift, axis=0), -jnp.inf))
            shift *= 2
        m_prev_all = m_ref[sb, 0:1, :]
        mm_all = jnp.maximum(cmax, m_prev_all)
        w_prev_all = jnp.exp(m_prev_all - mm_all)
        floor_all = jnp.exp(-(b_all + mm_all))
        mm_last = mm_all[BLK - 1:BLK, :]
        wts_all = jnp.exp(c_all - mm_last)
        decay_all = w_prev_all[BLK - 1:BLK, :]
        m_ref[sb] = jnp.broadcast_to(b_all[BLK - 1:BLK, :] + mm_last, (SUBLANES, LANES))
        c_rows = c_all.T

        heads = []
        for hd in range(NH_M):
            sl = slice(hd * DH_M, (hd + 1) * DH_M)
            q = qkv_ref[sb, :, sl]
            k = qkv_ref[sb, :, W_M + hd * DH_M:W_M + (hd + 1) * DH_M].astype(F32)
            v = qkv_ref[sb, :, 2 * W_M + hd * DH_M:2 * W_M + (hd + 1) * DH_M].astype(BF16)
            c = c_ref[sb, hd]
            n_row = n_ref[sb, hd:hd + 1, :]
            decay = decay_all[:, hd:hd + 1]

            qk = _dot_nt(q, k)
            qc = _dot(q, c)
            qn = jnp.sum(q.astype(F32) * n_row, axis=-1, keepdims=True)
            dmat = jnp.exp(jnp.where(causal, c_rows[hd:hd + 1, :] - mm_all[:, hd:hd + 1], -jnp.inf))
            kw = k * wts_all[:, hd:hd + 1]
            c_ref[sb, hd] = decay * c + _dot(kw.T, v)
            n_ref[sb, hd:hd + 1, :] = decay * n_row + jnp.sum(kw, axis=0, keepdims=True)
            heads.append((sl, v, qk * dmat, w_prev_all[:, hd:hd + 1], qc, qn, floor_all[:, hd:hd + 1]))

        for sl, v, s, w_prev, qc, qn, floor in heads:
            num = _dot(s, v) + w_prev * qc
            den = jnp.sum(s, axis=-1, keepdims=True) + w_prev * qn
            h = num / jnp.maximum(jnp.abs(den), floor)
            hg_ref[sb, :, sl] = (og_ref[sb, :, sl].astype(F32) * h).astype(hg_ref.dtype)


def _mlstm(qkvm, ifg, og, c0, n0, m0, nsb, nvalid):
    nseq, seq, _ = qkvm.shape
    blk_spec = lambda n: pl.BlockSpec((nsb, BLK, n), lambda b, j: (b, j, 0))
    return pl.pallas_call(
        functools.partial(_mlstm_kernel, nvalid),
        out_shape=[jax.ShapeDtypeStruct((nseq, seq, W_M), og.dtype),
                   jax.ShapeDtypeStruct((nseq, NH_M, DH_M, DH_M), F32),
                   jax.ShapeDtypeStruct((nseq, SUBLANES, LANES), F32),
                   jax.ShapeDtypeStruct((nseq, SUBLANES, LANES), F32)],
        grid=(nseq // nsb, seq // BLK),
        in_specs=[blk_spec(3 * W_M), blk_spec(LANES), blk_spec(W_M),
                  _const_spec(c0.shape), _const_spec(n0.shape), _const_spec(m0.shape)],
        out_specs=[blk_spec(W_M),
                   pl.BlockSpec((nsb, NH_M, DH_M, DH_M), lambda b, j: (b, 0, 0, 0)),
                   pl.BlockSpec((nsb, SUBLANES, LANES), lambda b, j: (b, 0, 0)),
                   pl.BlockSpec((nsb, SUBLANES, LANES), lambda b, j: (b, 0, 0))],
        compiler_params=_cparams(("arbitrary", "arbitrary")),
        name="mlstm",
    )(qkvm, ifg, og, c0, n0, m0)


def _mlstm_step_kernel(qkv_ref, ifg_ref, og_ref, c_ref, n_ref, m_ref,
                       hg_ref, c_out, n_out, m_out):
    gate = ifg_ref[...]
    ig = gate
    lf = pltpu.roll(gate, LANES - NH_M, axis=1)
    log_prev = lf + m_ref[...]
    m_t = jnp.maximum(log_prev, ig)
    dmat = jnp.exp(ig - m_t)
    w_prev = jnp.exp(log_prev - m_t)
    floor = jnp.exp(-m_t)
    m_out[...] = m_t

    qs, kds, vs, ss = [], [], [], []
    for hd in range(NH_M):
        q = qkv_ref[:, hd * DH_M:(hd + 1) * DH_M]
        k = qkv_ref[:, W_M + hd * DH_M:W_M + (hd + 1) * DH_M]
        v = qkv_ref[:, 2 * W_M + hd * DH_M:2 * W_M + (hd + 1) * DH_M]
        d_h = dmat[:, hd:hd + 1]
        qs.append(q)
        kds.append(k * d_h)
        vs.append(v)
        ss.append(jnp.sum(q * k, axis=-1, keepdims=True) * d_h)
    stacked = jnp.concatenate(qs + kds + [jnp.zeros((BLK - 2 * NH_M * DEC_ROWS, DH_M), F32)], axis=0)
    cols = stacked.T

    for r in range(DEC_ROWS):
        for hd in range(NH_M):
            q_col = cols[:, hd * DEC_ROWS + r:hd * DEC_ROWS + r + 1]
            k_col = cols[:, (NH_M + hd) * DEC_ROWS + r:(NH_M + hd) * DEC_ROWS + r + 1]
            c = c_ref[r, hd]
            n_row = n_ref[r, hd:hd + 1, :]
            v_row = vs[hd][r:r + 1, :]
            wp = w_prev[r:r + 1, hd:hd + 1]
            s = ss[hd][r:r + 1, :]
            qc = jnp.sum(q_col * c, axis=0, keepdims=True)
            qn = jnp.sum(qs[hd][r:r + 1, :] * n_row, axis=-1, keepdims=True)
            num = s * v_row + wp * qc
            den = s + wp * qn
            h = num / jnp.maximum(jnp.abs(den), floor[r:r + 1, hd:hd + 1])
            hg_ref[r:r + 1, hd * DH_M:(hd + 1) * DH_M] = og_ref[r:r + 1, hd * DH_M:(hd + 1) * DH_M] * h
            c_out[r, hd] = wp * c + k_col * v_row
            n_out[r, hd:hd + 1, :] = wp * n_row + kds[hd][r:r + 1, :]


def _mlstm_step(qkvm, ifg, og, c, n, m):
    nseq = qkvm.shape[0]
    row_spec = lambda w: pl.BlockSpec((DEC_ROWS, w), lambda i: (i, 0))
    c_spec = pl.BlockSpec((DEC_ROWS, NH_M, DH_M, DH_M), lambda i: (i, 0, 0, 0))
    n_spec = pl.BlockSpec((DEC_ROWS, NH_M, DH_M), lambda i: (i, 0, 0))
    return pl.pallas_call(
        _mlstm_step_kernel,
        out_shape=[jax.ShapeDtypeStruct((nseq, W_M), F32),
                   jax.ShapeDtypeStruct(c.shape, F32),
                   jax.ShapeDtypeStruct(n.shape, F32),
                   jax.ShapeDtypeStruct((nseq, LANES), F32)],
        grid=(nseq // DEC_ROWS,),
        in_specs=[row_spec(3 * W_M), row_spec(LANES), row_spec(W_M), c_spec, n_spec, row_spec(LANES)],
        out_specs=[row_spec(W_M), c_spec, n_spec, row_spec(LANES)],
        compiler_params=_cparams(("arbitrary",)),
        name="mlstm_step",
    )(qkvm, ifg, og, c, n, m)


def _ln_silu(y, g, b):
    yc = y - jnp.mean(y, axis=-1, keepdims=True)
    var = jnp.mean(yc * yc, axis=-1, keepdims=True)
    return _silu(yc * lax.rsqrt(var + LN_EPS) * g + b)


def _conv_kernel(carry, u_ref, hist0_ref, w_ref, b_ref, g_ref, beta_ref, out_ref, buf):
    @pl.when(pl.program_id(1) == 0)
    def _():
        buf[0:HIST, :] = hist0_ref[...]

    buf[HIST:HIST + BLK, :] = u_ref[0].astype(F32)
    window = buf[...]
    acc = jnp.broadcast_to(b_ref[...], (BLK, C_CONV))
    for sub in range(SUBLANES):
        shifted = window if sub == 0 else pltpu.roll(window, HIST + BLK - sub, axis=0)
        for grp in range(HIST // SUBLANES + 1):
            tap = grp * SUBLANES + sub - (HIST - (CONV_W - 1))
            if 0 <= tap < CONV_W:
                acc = acc + w_ref[tap:tap + 1, :] * shifted[grp * SUBLANES:grp * SUBLANES + BLK, :]
    out_ref[0] = _ln_silu(acc, g_ref[...], beta_ref[...]).astype(out_ref.dtype)
    if carry:
        buf[0:HIST, :] = window[BLK:BLK + HIST, :]


def _conv(u, hist0, w, b, g, beta):
    nseq, seq, _ = u.shape
    blk_spec = pl.BlockSpec((1, BLK, C_CONV), lambda s, j: (s, j, 0))
    return pl.pallas_call(
        functools.partial(_conv_kernel, seq > BLK),
        out_shape=jax.ShapeDtypeStruct(u.shape, u.dtype),
        grid=(nseq, seq // BLK),
        in_specs=[blk_spec, _const_spec(hist0.shape), _const_spec(w.shape), _const_spec(b.shape),
                  _const_spec(g.shape), _const_spec(beta.shape)],
        out_specs=blk_spec,
        scratch_shapes=[pltpu.VMEM((HIST + BLK, C_CONV), F32)],
        compiler_params=_cparams(("arbitrary", "arbitrary")),
        name="dwconv",
    )(u, hist0, w, b, g, beta)


def _conv_step_kernel(st_ref, u_ref, w_ref, b_ref, g_ref, beta_ref, out_ref):
    w_hist = w_ref[0:CONV_W - 1, :]
    rows = []
    for r in range(DEC_ROWS):
        rows.append(jnp.sum(st_ref[r] * w_hist, axis=0, keepdims=True))
    acc = jnp.concatenate(rows, axis=0) + w_ref[CONV_W - 1:CONV_W, :] * u_ref[...] + b_ref[...]
    out_ref[...] = _ln_silu(acc, g_ref[...], beta_ref[...])


def _conv_step(state, u, w, b, g, beta):
    nseq = u.shape[0]
    return pl.pallas_call(
        _conv_step_kernel,
        out_shape=jax.ShapeDtypeStruct(u.shape, F32),
        grid=(nseq // DEC_ROWS,),
        in_specs=[pl.BlockSpec((DEC_ROWS, CONV_W - 1, C_CONV), lambda i: (i, 0, 0)),
                  pl.BlockSpec((DEC_ROWS, C_CONV), lambda i: (i, 0)),
                  _const_spec(w.shape), _const_spec(b.shape), _const_spec(g.shape), _const_spec(beta.shape)],
        out_specs=pl.BlockSpec((DEC_ROWS, C_CONV), lambda i: (i, 0)),
        compiler_params=_cparams(("arbitrary",)),
        name="dwconv_step",
    )(state, u, w, b, g, beta)


def _stack_heads(q, low_half):
    heads = []
    for r in range(REP_A):
        slab = q[:, r * LANES:(r + 1) * LANES]
        heads.append(jnp.where(low_half, slab, 0.0))
        heads.append(jnp.where(low_half, 0.0, slab))
    return jnp.concatenate(heads, axis=0)


def _attn_block_kernel(base, carry, q_ref, k_ref, v_ref, k0_ref, v0t_ref, pos0_ref, sink_ref,
                       o_ref, kprev, vtprev):
    j = pl.program_id(1)
    nsb = q_ref.shape[0]

    if carry:
        @pl.when(j == 0)
        def _():
            kprev[...] = jnp.broadcast_to(k0_ref[...], kprev.shape).astype(kprev.dtype)
            vtprev[...] = jnp.broadcast_to(v0t_ref[...], vtprev.shape).astype(vtprev.dtype)

    kpos_prev = jnp.where(j == 0, pos0_ref[...],
                          base + BLK * j - BLK + lax.broadcasted_iota(jnp.int32, (BLK, 1), 0))
    kpos = jnp.concatenate([kpos_prev, base + BLK * j + lax.broadcasted_iota(jnp.int32, (BLK, 1), 0)], axis=0)
    qpos = base + BLK * j + lax.broadcasted_iota(jnp.int32, (1, BLK), 1)
    bias = jnp.where((kpos >= 0) & (kpos <= qpos) & (qpos - kpos <= WINDOW), 0.0, -jnp.inf)
    low_half = lax.broadcasted_iota(jnp.int32, (BLK, LANES), 1) < DH_A
    scale = DH_A ** -0.5

    for sb in range(nsb):
        k = k_ref[sb]
        vt = v_ref[sb].astype(F32).T.astype(BF16)
        k_prev, vt_prev = (kprev[sb], vtprev[sb]) if carry else (k0_ref[0], v0t_ref[0])
        k_cat = jnp.concatenate([k_prev.astype(BF16), k.astype(BF16)], axis=0)
        vt_cat = jnp.concatenate([vt_prev.astype(BF16), vt], axis=1)
        st = _dot_nt(k_cat, _stack_heads(q_ref[sb] * scale, low_half))
        probs, inv = [], []
        for hd in range(H_A):
            r, g = divmod(hd, KV_A)
            sink = sink_ref[0:1, g * REP_A + r:g * REP_A + r + 1]
            s = st[:, hd * BLK:(hd + 1) * BLK] + bias
            mx = jnp.maximum(jnp.max(s, axis=0, keepdims=True), sink)
            p = jnp.exp(s - mx)
            inv.append(1.0 / (jnp.sum(p, axis=0, keepdims=True) + jnp.exp(sink - mx)))
            probs.append(p.astype(BF16))
        ot = jnp.dot(vt_cat, jnp.concatenate(probs, axis=1), preferred_element_type=F32)
        out_t = jnp.concatenate(
            [ot[(hd % KV_A) * DH_A:(hd % KV_A + 1) * DH_A, hd * BLK:(hd + 1) * BLK] * inv[hd]
             for hd in range(H_A)], axis=0)
        o_ref[sb] = out_t.T.astype(o_ref.dtype)
        if carry:
            kprev[sb] = k
            vtprev[sb] = vt


def _attn_blocks(q, k, v, k0, v0t, pos0, sinks, nsb, base):
    nseq, seq, _ = q.shape
    q_spec = pl.BlockSpec((nsb, BLK, W_Q), lambda s, j: (s, j, 0))
    k_spec = pl.BlockSpec((nsb, BLK, W_KV), lambda s, j: (s, j, 0))
    return pl.pallas_call(
        functools.partial(_attn_block_kernel, base, seq > BLK),
        out_shape=jax.ShapeDtypeStruct(q.shape, q.dtype),
        grid=(nseq // nsb, seq // BLK),
        in_specs=[q_spec, k_spec, k_spec, _const_spec(k0.shape), _const_spec(v0t.shape),
                  _const_spec(pos0.shape), _const_spec(sinks.shape)],
        out_specs=q_spec,
        scratch_shapes=[pltpu.VMEM((nsb, BLK, W_KV), k.dtype), pltpu.VMEM((nsb, W_KV, BLK), BF16)],
        compiler_params=_cparams(("arbitrary", "arbitrary")),
        name="swa",
    )(q, k, v, k0, v0t, pos0, sinks)


def _attn_step_kernel(base, q_ref, k_ref, v_ref, kc_ref, vc_ref, sink_ref, o_ref):
    low_half = lax.broadcasted_iota(jnp.int32, (1, LANES), 1) < DH_A
    sinks = jnp.concatenate([sink_ref[0:1, g * REP_A + r:g * REP_A + r + 1]
                             for r in range(REP_A) for g in range(KV_A)], axis=0)
    kpos = base - WINDOW + lax.broadcasted_iota(jnp.int32, (1, 2 * BLK), 1)
    bias = jnp.where((kpos >= 0) & (kpos <= base) & (base - kpos <= WINDOW), 0.0, -jnp.inf)
    pad = jnp.zeros((BLK - 1, W_KV), F32)
    scale = DH_A ** -0.5
    for sb in range(q_ref.shape[0]):
        k_cat = jnp.concatenate([kc_ref[sb], k_ref[sb], pad], axis=0)
        v_cat = jnp.concatenate([vc_ref[sb], v_ref[sb], pad], axis=0)
        s = _dot_nt(_stack_heads(q_ref[sb] * scale, low_half), k_cat) + bias
        mx = jnp.maximum(jnp.max(s, axis=-1, keepdims=True), sinks)
        p = jnp.exp(s - mx)
        den = jnp.sum(p, axis=-1, keepdims=True) + jnp.exp(sinks - mx)
        o = _dot(p * (1.0 / den), v_cat)
        for r in range(REP_A):
            o_ref[sb, :, r * LANES:(r + 1) * LANES] = jnp.where(low_half, o[2 * r:2 * r + 1], o[2 * r + 1:2 * r + 2])


def _attn_step(q, k, v, k_cache, v_cache, sinks, base):
    nseq = q.shape[0]
    spec = lambda rows, n: pl.BlockSpec((DEC_ROWS, rows, n), lambda i: (i, 0, 0))
    return pl.pallas_call(
        functools.partial(_attn_step_kernel, base),
        out_shape=jax.ShapeDtypeStruct(q.shape, F32),
        grid=(nseq // DEC_ROWS,),
        in_specs=[spec(1, W_Q), spec(1, W_KV), spec(1, W_KV), spec(WINDOW, W_KV), spec(WINDOW, W_KV),
                  _const_spec(sinks.shape)],
        out_specs=spec(1, W_Q),
        compiler_params=_cparams(("arbitrary",)),
        name="swa_step",
    )(q, k, v, k_cache, v_cache, sinks)


def _merge_kernel(x_ref, hg_ref, cv_ref, oa_ref, gates_ref, wm_ref, wc_ref, wa_ref, wo_ref, out_ref):
    d_model = x_ref.shape[1]
    mix = (gates_ref[:, 0:d_model] * _dot(hg_ref[...], wm_ref[...])
           + gates_ref[:, d_model:2 * d_model] * _dot(cv_ref[...], wc_ref[...])
           + gates_ref[:, 2 * d_model:3 * d_model] * _dot(oa_ref[...], wa_ref[...]))
    out_ref[...] = x_ref[...] + _dot(mix, wo_ref[...])


def _merge(x, hg, cv, oa, gates, wm, wc, wa, wo, tm):
    rows, d_model = x.shape
    row_spec = lambda n: pl.BlockSpec((tm, n), lambda i: (i, 0))
    return pl.pallas_call(
        _merge_kernel,
        out_shape=jax.ShapeDtypeStruct(x.shape, F32),
        grid=(rows // tm,),
        in_specs=[row_spec(d_model), row_spec(W_M), row_spec(C_CONV), row_spec(W_Q), row_spec(3 * d_model),
                  _const_spec(wm.shape), _const_spec(wc.shape), _const_spec(wa.shape), _const_spec(wo.shape)],
        out_specs=row_spec(d_model),
        compiler_params=_cparams(("arbitrary",)),
        name="merge",
    )(x, hg, cv, oa, gates, wm, wc, wa, wo)


def _ffn_body(x, g2_ref, wg_ref, wu_ref, wd_ref, wc_ref, bc_ref, prev_rows):
    d_ff = wg_ref.shape[1]
    nchunk = 2 if d_ff % (2 * LANES) == 0 else 1
    cw = d_ff // nchunk
    h2 = _rms(x, g2_ref[...]).astype(BF16)
    acc = x
    gps = []
    for c in range(nchunk):
        lo, hi = c * cw, (c + 1) * cw
        gp = jnp.dot(h2, wg_ref[:, lo:hi], preferred_element_type=F32)
        up = jnp.dot(h2, wu_ref[:, lo:hi], preferred_element_type=F32)
        p2, p1 = prev_rows(gp, lo, hi)
        gc = (wc_ref[0:1, lo:hi] * p2 + wc_ref[1:2, lo:hi] * p1 + wc_ref[2:3, lo:hi] * gp
              + bc_ref[:, lo:hi])
        acc = acc + _dot(_silu(gc) * up, wd_ref[lo:hi, :])
        gps.append(gp)
    return acc, gps


def _ffn_seq_kernel(tiles_per_seq, tail_end, final, x_ref, g2_ref, wg_ref, wu_ref, wd_ref, wc_ref, bc_ref,
                    init_ref, gf_ref, out_ref, tail_ref, carry):
    @pl.when(pl.program_id(0) % tiles_per_seq == 0)
    def _():
        carry[...] = init_ref[...]

    tm = x_ref.shape[0]
    row = lax.broadcasted_iota(jnp.int32, (tm, 1), 0)

    def prev_rows(gp, lo, hi):
        c2 = carry[SUBLANES - 2:SUBLANES - 1, lo:hi]
        c1 = carry[SUBLANES - 1:SUBLANES, lo:hi]
        p1 = jnp.where(row == 0, c1, pltpu.roll(gp, 1, axis=0))
        p2 = jnp.where(row == 0, c2, jnp.where(row == 1, c1, pltpu.roll(gp, 2, axis=0)))
        return p2, p1

    acc, gps = _ffn_body(x_ref[...], g2_ref, wg_ref, wu_ref, wd_ref, wc_ref, bc_ref, prev_rows)
    out_ref[...] = _rms(acc, gf_ref[...]) if final else acc
    tail = jnp.concatenate([gp[tail_end - SUBLANES:tail_end, :] for gp in gps], axis=1)
    tail_ref[...] = tail
    if tiles_per_seq > 1:
        carry[...] = tail


def _ffn_seq(x, g2, wg, wu, wd, wc, bc, init, gf, tm, tiles_per_seq, tail_end, final):
    rows, d_model = x.shape
    d_ff = wg.shape[1]
    assert tiles_per_seq == 1 or tail_end == tm
    return pl.pallas_call(
        functools.partial(_ffn_seq_kernel, tiles_per_seq, tail_end, final),
        out_shape=[jax.ShapeDtypeStruct(x.shape, F32),
                   jax.ShapeDtypeStruct((rows // tm, SUBLANES, d_ff), F32)],
        grid=(rows // tm,),
        in_specs=[pl.BlockSpec((tm, d_model), lambda i: (i, 0)), _const_spec(g2.shape),
                  _const_spec(wg.shape), _const_spec(wu.shape), _const_spec(wd.shape),
                  _const_spec(wc.shape), _const_spec(bc.shape), _const_spec(init.shape),
                  _const_spec(gf.shape)],
        out_specs=[pl.BlockSpec((tm, d_model), lambda i: (i, 0)),
                   pl.BlockSpec((None, SUBLANES, d_ff), lambda i: (i, 0, 0))],
        scratch_shapes=[pltpu.VMEM((SUBLANES, d_ff), F32)],
        compiler_params=_cparams(("arbitrary",)),
        name="convffn",
    )(x, g2, wg, wu, wd, wc, bc, init, gf)


def _ffn_step_kernel(final, x_ref, g2_ref, wg_ref, wu_ref, wd_ref, wc_ref, bc_ref,
                     p2_ref, p1_ref, gf_ref, out_ref, gp_ref):
    def prev_rows(gp, lo, hi):
        return p2_ref[:, lo:hi], p1_ref[:, lo:hi]

    acc, gps = _ffn_body(x_ref[...], g2_ref, wg_ref, wu_ref, wd_ref, wc_ref, bc_ref, prev_rows)
    out_ref[...] = _rms(acc, gf_ref[...]) if final else acc
    gp_ref[...] = jnp.concatenate(gps, axis=1)


def _ffn_step(x, g2, wg, wu, wd, wc, bc, p2, p1, gf, tm, final):
    rows, d_model = x.shape
    d_ff = wg.shape[1]
    row_spec = lambda n: pl.BlockSpec((tm, n), lambda i: (i, 0))
    return pl.pallas_call(
        functools.partial(_ffn_step_kernel, final),
        out_shape=[jax.ShapeDtypeStruct(x.shape, F32), jax.ShapeDtypeStruct((rows, d_ff), F32)],
        grid=(rows // tm,),
        in_specs=[row_spec(d_model), _const_spec(g2.shape),
                  _const_spec(wg.shape), _const_spec(wu.shape), _const_spec(wd.shape),
                  _const_spec(wc.shape), _const_spec(bc.shape), row_spec(d_ff), row_spec(d_ff),
                  _const_spec(gf.shape)],
        out_specs=[row_spec(d_model), row_spec(d_ff)],
        compiler_params=_cparams(("arbitrary",)),
        name="convffn_step",
    )(x, g2, wg, wu, wd, wc, bc, p2, p1, gf)


def _row_tile(rows, cap):
    tm = min(rows, cap)
    while rows % tm:
        tm //= 2
    return tm


def kernel(x_prompt, x_sample, state_mlstm_c, state_mlstm_n, state_mlstm_m, state_conv, cache_swa_k, cache_swa_v, state_ffn_conv, meta_tokens, norm1_g, w_in, b_igate, b_fgate, w_mlstm_out, w_dconv, b_dconv, ln_conv_g, ln_conv_b, w_conv_out, attn_sinks, w_attn_out, b_merge, w_out, norm2_g, w_ffn_gate, w_ffn_up, w_ffn_conv, b_ffn_conv, w_ffn_down, final_norm_g):
    bsz, seq, d_model = x_prompt.shape
    nseq_s = x_sample.shape[0]
    depth = w_in.shape[0]
    d_ff = w_ffn_gate.shape[2]
    nblk = seq // BLK
    assert seq % BLK == 0 and x_sample.shape[1] == 1 and nseq_s % DEC_ROWS == 0
    assert cache_swa_k.shape[2] == WINDOW
    act = BF16
    nsb = SEQ_PER_STEP if bsz % SEQ_PER_STEP == 0 else 1

    x_main = x_prompt.reshape(bsz * seq, d_model)
    x_meta = jnp.concatenate([meta_tokens.astype(F32), jnp.zeros((BLK - N_META, d_model), F32)], axis=0)
    x_smp = x_sample.reshape(nseq_s, d_model)

    tm_main = _row_tile(seq, 512)
    tm_smp = _row_tile(nseq_s, 128)

    tab_meta = _rope_tables(jnp.arange(BLK))
    tab_main = _rope_tables(N_META + jnp.arange(seq))
    tab_smp = _rope_tables(jnp.full((tm_smp,), PAST_LEN))
    q_perm = jnp.arange(W_Q).reshape(KV_A, REP_A, DH_A).transpose(1, 0, 2).reshape(W_Q)
    iota_blk = jnp.arange(BLK, dtype=jnp.int32)[:, None]
    pos0_meta = jnp.full((BLK, 1), -1, jnp.int32)
    pos0_main = jnp.where(iota_blk < N_META, iota_blk, -1)

    zeros_c = jnp.zeros((1, NH_M, DH_M, DH_M), F32)
    zeros_nm = jnp.zeros((1, SUBLANES, LANES), F32)
    zeros_kv = jnp.zeros((1, BLK, W_KV), act)
    gf = final_norm_g.reshape(1, d_model)

    p_states = [[] for _ in range(7)]
    s_states = [[] for _ in range(7)]
    for l in range(depth):
        final = l == depth - 1
        wl = w_in[l]
        col_gates = 3 * W_M
        col_o = col_gates + 2 * NH_M
        col_qa = col_o + W_M + 2 * C_CONV
        w_pack = jnp.concatenate(
            [wl[:, :col_gates], wl[:, col_o:col_qa], wl[:, col_qa:col_qa + W_Q][:, q_perm],
             wl[:, col_qa + W_Q:], wl[:, col_gates:col_o],
             jnp.zeros((d_model, LANES - 2 * NH_M), F32)], axis=1).astype(BF16)
        bif = jnp.concatenate([b_igate[l], b_fgate[l], jnp.zeros((LANES - 2 * NH_M,), F32)]).reshape(1, LANES)
        g1 = norm1_g[l].reshape(1, d_model)
        bm = b_merge[l].reshape(1, 3 * d_model)
        w_dc = jnp.concatenate([w_dconv[l], jnp.zeros((HIST - CONV_W, C_CONV), F32)], axis=0)
        b_dc = b_dconv[l].reshape(1, C_CONV)
        ln_g = ln_conv_g[l].reshape(1, C_CONV)
        ln_b = ln_conv_b[l].reshape(1, C_CONV)
        sinks = jnp.concatenate([attn_sinks[l], jnp.zeros((LANES - H_A,), F32)]).reshape(1, LANES)
        wm, wc, wa, wo = (w.astype(BF16) for w in (w_mlstm_out[l], w_conv_out[l], w_attn_out[l][q_perm], w_out[l]))
        g2 = norm2_g[l].reshape(1, d_model)
        wg, wu, wd = (w.astype(BF16) for w in (w_ffn_gate[l], w_ffn_up[l], w_ffn_down[l]))
        w_fc = jnp.concatenate([w_ffn_conv[l], jnp.zeros((SUBLANES - FFN_CONV_W, d_ff), F32)], axis=0)
        b_fc = b_ffn_conv[l].reshape(1, d_ff)

        qkvm, og, u, qa, ka, va, gates, ifg = _in_proj(x_meta, g1, w_pack, bm, bif, tab_meta, BLK, act)
        hg, c_meta, n_meta, m_meta = _mlstm(qkvm[None], ifg[None], og[None], zeros_c, zeros_nm, zeros_nm,
                                            1, N_META)
        cv = _conv(u[None], jnp.zeros((HIST, C_CONV), F32), w_dc, b_dc, ln_g, ln_b)
        oa = _attn_blocks(qa[None], ka[None], va[None], zeros_kv, zeros_kv, pos0_meta, sinks, 1, 0)
        x1 = _merge(x_meta, hg[0], cv[0], oa[0], gates, wm, wc, wa, wo, BLK)
        x_meta, tail_meta = _ffn_seq(x1, g2, wg, wu, wd, w_fc, b_fc, jnp.zeros((SUBLANES, d_ff), F32), gf,
                                     BLK, 1, N_META, False)
        u_meta, ka_meta, va_meta = u, ka, va

        qkvm, og, u, qa, ka, va, gates, ifg = _in_proj(x_main, g1, w_pack, bm, bif, tab_main, tm_main, act)
        per_seq = lambda a: a.reshape(bsz, seq, a.shape[-1])
        hg, p_c, p_n, p_m = _mlstm(per_seq(qkvm), per_seq(ifg), per_seq(og), c_meta, n_meta, m_meta, nsb, BLK)
        hist0 = jnp.concatenate([jnp.zeros((HIST - N_META, C_CONV), F32), u_meta[:N_META].astype(F32)], axis=0)
        cv = _conv(per_seq(u), hist0, w_dc, b_dc, ln_g, ln_b)
        oa = _attn_blocks(per_seq(qa), per_seq(ka), per_seq(va), ka_meta[None], va_meta.T[None], pos0_main,
                          sinks, nsb, N_META)
        flat = lambda a: a.reshape(bsz * seq, a.shape[-1])
        x1 = _merge(x_main, flat(hg), flat(cv), flat(oa), gates, wm, wc, wa, wo, tm_main)
        x_main, tails = _ffn_seq(x1, g2, wg, wu, wd, w_fc, b_fc, tail_meta[0], gf,
                                 tm_main, seq // tm_main, tm_main, final)
        p_states[0].append(p_c)
        p_states[1].append(p_n[:, :NH_M, :])
        p_states[2].append(p_m[:, 0, :NH_M])
        p_states[3].append(per_seq(u)[:, seq - (CONV_W - 1):].astype(F32))
        p_states[4].append(ka.reshape(bsz, seq, KV_A, DH_A)[:, seq - WINDOW:].astype(F32))
        p_states[5].append(va.reshape(bsz, seq, KV_A, DH_A)[:, seq - WINDOW:].astype(F32))
        p_states[6].append(tails.reshape(bsz, seq // tm_main, SUBLANES, d_ff)[:, -1, SUBLANES - (FFN_CONV_W - 1):])

        qkvm, og, u, qa, ka, va, gates, ifg = _in_proj(x_smp, g1, w_pack, bm, bif, tab_smp, tm_smp, F32)
        m_in = jnp.concatenate([state_mlstm_m[l], jnp.zeros((nseq_s, LANES - NH_M), F32)], axis=1)
        hg, s_c, s_n, s_m = _mlstm_step(qkvm, ifg, og, state_mlstm_c[l], state_mlstm_n[l], m_in)
        cv = _conv_step(state_conv[l], u, w_dc, b_dc, ln_g, ln_b)
        oa = _attn_step(qa[:, None], ka[:, None], va[:, None],
                        cache_swa_k[l].reshape(nseq_s, WINDOW, W_KV), cache_swa_v[l].reshape(nseq_s, WINDOW, W_KV),
                        sinks, PAST_LEN)
        x1 = _merge(x_smp, hg, cv, oa[:, 0], gates, wm, wc, wa, wo, tm_smp)
        x_smp, gp_s = _ffn_step(x1, g2, wg, wu, wd, w_fc, b_fc, state_ffn_conv[l][:, 0], state_ffn_conv[l][:, 1],
                                gf, tm_smp, final)
        s_states[0].append(s_c)
        s_states[1].append(s_n)
        s_states[2].append(s_m[:, :NH_M])
        s_states[3].append(jnp.concatenate([state_conv[l][:, 1:], u[:, None]], axis=1))
        s_states[4].append(jnp.concatenate(
            [cache_swa_k[l][:, 1:], ka.reshape(nseq_s, 1, KV_A, DH_A)], axis=1))
        s_states[5].append(jnp.concatenate(
            [cache_swa_v[l][:, 1:], va.reshape(nseq_s, 1, KV_A, DH_A)], axis=1))
        s_states[6].append(jnp.concatenate([state_ffn_conv[l][:, 1:], gp_s[:, None]], axis=1))

    y_prompt = x_main.reshape(bsz, seq, d_model)
    y_sample = x_smp.reshape(nseq_s, 1, d_model)
    return (y_prompt, y_sample, *(jnp.stack(a, axis=0) for a in p_states),
            *(jnp.stack(a, axis=0) for a in s_states))
```

```python
import functools
from typing import NamedTuple

import jax
import jax.numpy as jnp
from jax import lax
from jax.experimental import pallas as pl
from jax.experimental.pallas import tpu as pltpu

F32 = jnp.float32
BF16 = jnp.bfloat16

N_META = 16
NH_M = 4
DH_M = 128
W_M = NH_M * DH_M
C_CONV = 512
CONV_W = 31
H_A = 8
KV_A = 2
DH_A = 64
REP_A = H_A // KV_A
W_Q = H_A * DH_A
W_KV = KV_A * DH_A
ROT_DIM = DH_A // 4
ROT_HALF = ROT_DIM // 2
ROPE_THETA = 500000.0
WINDOW = 128
FFN_CONV_W = 3
PAST_LEN = 16384
RMS_EPS = 1e-6
LN_EPS = 1e-5

BLK = 128
LANES = 128
MXU_TILE = 256
SUBLANES = 8
HIST = 32
DEC_ROWS = 8
SEQ_PER_STEP = 8
VMEM_LIMIT = 52 * 1024 * 1024

OFF_QKVM = 0
OFF_O = OFF_QKVM + 3 * W_M
OFF_GLU = OFF_O + W_M
OFF_QA = OFF_GLU + 2 * C_CONV
OFF_KA = OFF_QA + W_Q
OFF_VA = OFF_KA + W_KV
OFF_G = OFF_VA + W_KV


def _cparams(sem):
    return pltpu.CompilerParams(dimension_semantics=sem, vmem_limit_bytes=VMEM_LIMIT)


class _LayerParam(NamedTuple):
    stack: jax.Array
    layer: int

    @property
    def shape(self):
        return self.stack.shape[1:]


def _const_spec(a):
    nd = len(a.shape)
    if isinstance(a, _LayerParam):
        layer = a.layer
        return pl.BlockSpec((None,) + tuple(a.shape), lambda *_: (layer,) + (0,) * nd,
                            pipeline_mode=pl.Buffered(1))
    return pl.BlockSpec(a.shape, lambda *_: (0,) * nd, pipeline_mode=pl.Buffered(1))


def _operands(*args):
    return tuple(a.stack if isinstance(a, _LayerParam) else a for a in args)


def _rms(x, g):
    return x * lax.rsqrt(jnp.mean(x * x, axis=-1, keepdims=True) + RMS_EPS) * g


def _log_sigmoid(x):
    return jnp.minimum(x, 0.0) - jnp.log1p(jnp.exp(-jnp.abs(x)))


def _sigmoid(x):
    return 1.0 / (1.0 + jnp.exp(-x))


def _silu(x):
    return x * _sigmoid(x)


def _dot(a, b):
    return jnp.dot(a.astype(BF16), b.astype(BF16), preferred_element_type=F32)


def _dot_nt(a, b):
    return lax.dot_general(a.astype(BF16), b.astype(BF16), (((1,), (1,)), ((), ())),
                           preferred_element_type=F32)


def _rope(x, cos, sin_lo, sin_hi):
    width = x.shape[1]
    reps = width // LANES
    if reps > 1:
        cos, sin_lo, sin_hi = (jnp.concatenate([t] * reps, axis=1) for t in (cos, sin_lo, sin_hi))
    return (x * cos + pltpu.roll(x, width - ROT_HALF, axis=1) * sin_lo
            + pltpu.roll(x, ROT_HALF, axis=1) * sin_hi)


def _rope_tables(pos):
    inv = jnp.power(ROPE_THETA, -jnp.arange(ROT_HALF, dtype=F32) / ROT_HALF)
    ang = pos.astype(F32)[:, None] * inv[None, :]
    cos, sin = jnp.cos(ang), jnp.sin(ang)
    npos = pos.shape[0]
    ones = jnp.ones((npos, DH_A - ROT_DIM), F32)
    zeros = jnp.zeros((npos, DH_A - ROT_DIM), F32)
    zh = jnp.zeros((npos, ROT_HALF), F32)
    head = lambda parts: jnp.concatenate(parts * (LANES // DH_A), axis=1)
    return (head([cos, cos, ones]), head([-sin, zh, zeros]), head([zh, sin, zeros]))


def _dwconv_block(window, w_ref, b_ref, g_ref, beta_ref):
    acc = jnp.broadcast_to(b_ref[...], (BLK, C_CONV))
    for sub in range(SUBLANES):
        shifted = window if sub == 0 else pltpu.roll(window, HIST + BLK - sub, axis=0)
        for grp in range(HIST // SUBLANES + 1):
            tap = grp * SUBLANES + sub - (HIST - (CONV_W - 1))
            if 0 <= tap < CONV_W:
                acc = acc + w_ref[tap:tap + 1, :] * shifted[grp * SUBLANES:grp * SUBLANES + BLK, :]
    return _ln_silu(acc, g_ref[...], beta_ref[...])


def _in_proj_kernel(conv_tiles, x_ref, g_ref, w_ref, bm_ref, bif_ref, cos_ref, sin_lo_ref, sin_hi_ref, *refs):
    if conv_tiles:
        hist0_ref, wdc_ref, bdc_ref, lng_ref, lnb_ref = refs[:5]
        qkvm_ref, og_ref, u_ref, qa_ref, ka_ref, va_ref, gates_ref, ifg_ref, cv_ref, buf = refs[5:]
    else:
        qkvm_ref, og_ref, u_ref, qa_ref, ka_ref, va_ref, gates_ref, ifg_ref = refs
    tm, d_model = x_ref.shape
    h = _rms(x_ref[...], g_ref[...]).astype(BF16)
    act = qkvm_ref.dtype

    def proj(off, n):
        return jnp.dot(h, w_ref[:, off:off + n], preferred_element_type=F32)

    glu = proj(OFF_GLU, 2 * C_CONV)
    u = glu[:, :C_CONV] * _sigmoid(glu[:, C_CONV:])
    u_ref[...] = u.astype(act)
    if conv_tiles:
        @pl.when(pl.program_id(0) % conv_tiles == 0)
        def _():
            buf[0:HIST, :] = hist0_ref[...]

        buf[HIST:HIST + tm, :] = u

    def conv_block(blk):
        window = buf[blk * BLK:blk * BLK + HIST + BLK, :]
        cv_ref[blk * BLK:(blk + 1) * BLK, :] = _dwconv_block(
            window, wdc_ref, bdc_ref, lng_ref, lnb_ref).astype(act)

    def mlstm_qkv():
        qkvm_ref[:, 0:W_M] = proj(OFF_QKVM, W_M).astype(act)
        qkvm_ref[:, W_M:2 * W_M] = (proj(OFF_QKVM + W_M, W_M) * DH_M ** -0.5).astype(act)
        qkvm_ref[:, 2 * W_M:3 * W_M] = proj(OFF_QKVM + 2 * W_M, W_M).astype(act)

    def out_gate():
        og_ref[...] = _sigmoid(proj(OFF_O, W_M)).astype(act)

    def attn_qkv():
        tabs = (cos_ref[...], sin_lo_ref[...], sin_hi_ref[...])
        qa_ref[...] = _rope(proj(OFF_QA, W_Q), *tabs).astype(act)
        kv = proj(OFF_KA, 2 * W_KV)
        ka_ref[...] = _rope(kv[:, :W_KV], *tabs).astype(act)
        va_ref[...] = kv[:, W_KV:].astype(act)

    def merge_gates(part):
        cols = slice(part * d_model, (part + 1) * d_model)
        gates_ref[:, cols] = _sigmoid(proj(OFF_G + part * d_model, d_model) + bm_ref[:, cols]).astype(act)

    def log_gates():
        z = proj(OFF_G + 3 * d_model, LANES) + bif_ref[...]
        lane = lax.broadcasted_iota(jnp.int32, z.shape, 1)
        ifg_ref[...] = jnp.where(lane < NH_M, z, _log_sigmoid(z))

    steps = [mlstm_qkv, out_gate, attn_qkv] + [functools.partial(merge_gates, p) for p in range(3)] + [log_gates]
    nconv = tm // BLK if conv_tiles else 0
    for i, step in enumerate(steps):
        for blk in range(nconv * i // len(steps), nconv * (i + 1) // len(steps)):
            conv_block(blk)
        step()
    if conv_tiles > 1:
        buf[0:HIST, :] = buf[tm:tm + HIST, :]


def _in_proj(x, g, w, bm, bif, tabs, tm, act, conv=None, conv_tiles=0):
    rows, d_model = x.shape
    ntab = tabs[0].shape[0] // tm
    widths = (3 * W_M, W_M, C_CONV, W_Q, W_KV, W_KV, 3 * d_model, LANES) + ((C_CONV,) if conv_tiles else ())
    dtypes = (act,) * 7 + (F32,) + ((act,) if conv_tiles else ())
    conv = tuple(conv) if conv_tiles else ()
    row_spec = lambda n: pl.BlockSpec((tm, n), lambda i: (i, 0))
    tab_spec = pl.BlockSpec((tm, LANES), lambda i: (i % ntab, 0))
    return pl.pallas_call(
        functools.partial(_in_proj_kernel, conv_tiles),
        out_shape=[jax.ShapeDtypeStruct((rows, n), dt) for n, dt in zip(widths, dtypes)],
        grid=(rows // tm,),
        in_specs=[row_spec(d_model), _const_spec(g), _const_spec(w),
                  _const_spec(bm), _const_spec(bif), tab_spec, tab_spec, tab_spec]
                 + [_const_spec(a) for a in conv],
        out_specs=[row_spec(n) for n in widths],
        scratch_shapes=[pltpu.VMEM((HIST + tm, C_CONV), F32)] if conv_tiles else [],
        compiler_params=_cparams(("arbitrary",)),
        name="in_proj",
    )(*_operands(x, g, w, bm, bif, *tabs, *conv))


def _mlstm_kernel(nvalid, qkv_ref, ifg_ref, og_ref, c0_ref, n0_ref, m0_ref,
                  hg_ref, c_ref, n_ref, m_ref):
    @pl.when(pl.program_id(1) == 0)
    def _():
        c_ref[...] = jnp.broadcast_to(c0_ref[...], c_ref.shape)
        n_ref[...] = jnp.broadcast_to(n0_ref[...], n_ref.shape)
        m_ref[...] = jnp.broadcast_to(m0_ref[...], m_ref.shape)

    row = lax.broadcasted_iota(jnp.int32, (BLK, LANES), 0)
    lane = lax.broadcasted_iota(jnp.int32, (BLK, LANES), 1)
    causal = (lax.broadcasted_iota(jnp.int32, (BLK, BLK), 0)
              >= lax.broadcasted_iota(jnp.int32, (BLK, BLK), 1))

    for sb in range(qkv_ref.shape[0]):
        gate = jnp.where(lane < 2 * NH_M, ifg_ref[sb], 0.0)
        if nvalid < BLK:
            gate = jnp.where(row >= nvalid, jnp.where(lane < NH_M, -jnp.inf, 0.0), gate)
        csum = jnp.where(lane < NH_M, 0.0, gate)
        shift = 1
        while shift < BLK:
            csum = csum + jnp.where(row >= shift, pltpu.roll(csum, shift, axis=0), 0.0)
            shift *= 2
        b_all = pltpu.roll(csum, LANES - NH_M, axis=1)
        c_all = gate - b_all
        cmax = c_all
        shift = 1
        while shift < BLK:
            cmax = jnp.maximum(cmax, jnp.where(row >= shift, pltpu.roll(cmax, shift, axis=0), -jnp.inf))
            shift *= 2
        m_prev_all = m_ref[sb, 0:1, :]
        mm_all = jnp.maximum(cmax, m_prev_all)
        w_prev_all = jnp.exp(m_prev_all - mm_all)
        floor_all = jnp.exp(-(b_all + mm_all))
        mm_last = mm_all[BLK - 1:BLK, :]
        wts_all = jnp.exp(c_all - mm_last)
        decay_all = w_prev_all[BLK - 1:BLK, :]
        m_ref[sb] = jnp.broadcast_to(b_all[BLK - 1:BLK, :] + mm_last, (SUBLANES, LANES))
        c_rows = c_all.T

        heads = []
        for hd in range(NH_M):
            sl = slice(hd * DH_M, (hd + 1) * DH_M)
            q = qkv_ref[sb, :, sl]
            k = qkv_ref[sb, :, W_M + hd * DH_M:W_M + (hd + 1) * DH_M].astype(F32)
            v = qkv_ref[sb, :, 2 * W_M + hd * DH_M:2 * W_M + (hd + 1) * DH_M].astype(BF16)
            c = c_ref[sb, hd]
            n_row = n_ref[sb, hd:hd + 1, :]
            decay = decay_all[:, hd:hd + 1]

            qk = _dot_nt(q, k)
            qc = _dot(q, c)
            qn = jnp.sum(q.astype(F32) * n_row, axis=-1, keepdims=True)
            dmat = jnp.exp(jnp.where(causal, c_rows[hd:hd + 1, :] - mm_all[:, hd:hd + 1], -jnp.inf))
            kw = k * wts_all[:, hd:hd + 1]
            c_ref[sb, hd] = decay * c + _dot(kw.T, v)
            n_ref[sb, hd:hd + 1, :] = decay * n_row + jnp.sum(kw, axis=0, keepdims=True)
            heads.append((sl, v, qk * dmat, w_prev_all[:, hd:hd + 1], qc, qn, floor_all[:, hd:hd + 1]))

        for sl, v, s, w_prev, qc, qn, floor in heads:
            num = _dot(s, v) + w_prev * qc
            den = jnp.sum(s, axis=-1, keepdims=True) + w_prev * qn
            h = num / jnp.maximum(jnp.abs(den), floor)
            hg_ref[sb, :, sl] = (og_ref[sb, :, sl].astype(F32) * h).astype(hg_ref.dtype)


def _mlstm(qkvm, ifg, og, c0, n0, m0, nsb, nvalid):
    nseq, seq, _ = qkvm.shape
    blk_spec = lambda n: pl.BlockSpec((nsb, BLK, n), lambda b, j: (b, j, 0))
    return pl.pallas_call(
        functools.partial(_mlstm_kernel, nvalid),
        out_shape=[jax.ShapeDtypeStruct((nseq, seq, W_M), og.dtype),
                   jax.ShapeDtypeStruct((nseq, NH_M, DH_M, DH_M), F32),
                   jax.ShapeDtypeStruct((nseq, SUBLANES, LANES), F32),
                   jax.ShapeDtypeStruct((nseq, SUBLANES, LANES), F32)],
        grid=(nseq // nsb, seq // BLK),
        in_specs=[blk_spec(3 * W_M), blk_spec(LANES), blk_spec(W_M),
                  _const_spec(c0), _const_spec(n0), _const_spec(m0)],
        out_specs=[blk_spec(W_M),
                   pl.BlockSpec((nsb, NH_M, DH_M, DH_M), lambda b, j: (b, 0, 0, 0)),
                   pl.BlockSpec((nsb, SUBLANES, LANES), lambda b, j: (b, 0, 0)),
                   pl.BlockSpec((nsb, SUBLANES, LANES), lambda b, j: (b, 0, 0))],
        compiler_params=_cparams(("arbitrary", "arbitrary")),
        name="mlstm",
    )(qkvm, ifg, og, c0, n0, m0)


def _mlstm_step_kernel(qkv_ref, ifg_ref, og_ref, c_ref, n_ref, m_ref,
                       hg_ref, c_out, n_out, m_out):
    gate = ifg_ref[...]
    ig = gate
    lf = pltpu.roll(gate, LANES - NH_M, axis=1)
    log_prev = lf + m_ref[...]
    m_t = jnp.maximum(log_prev, ig)
    dmat = jnp.exp(ig - m_t)
    w_prev = jnp.exp(log_prev - m_t)
    floor = jnp.exp(-m_t)
    m_out[...] = m_t

    qs, kds, vs, ss = [], [], [], []
    for hd in range(NH_M):
        q = qkv_ref[:, hd * DH_M:(hd + 1) * DH_M]
        k = qkv_ref[:, W_M + hd * DH_M:W_M + (hd + 1) * DH_M]
        v = qkv_ref[:, 2 * W_M + hd * DH_M:2 * W_M + (hd + 1) * DH_M]
        d_h = dmat[:, hd:hd + 1]
        qs.append(q)
        kds.append(k * d_h)
        vs.append(v)
        ss.append(jnp.sum(q * k, axis=-1, keepdims=True) * d_h)
    stacked = jnp.concatenate(qs + kds + [jnp.zeros((BLK - 2 * NH_M * DEC_ROWS, DH_M), F32)], axis=0)
    cols = stacked.T

    for r in range(DEC_ROWS):
        for hd in range(NH_M):
            q_col = cols[:, hd * DEC_ROWS + r:hd * DEC_ROWS + r + 1]
            k_col = cols[:, (NH_M + hd) * DEC_ROWS + r:(NH_M + hd) * DEC_ROWS + r + 1]
            c = c_ref[r, hd]
            n_row = n_ref[r, hd:hd + 1, :]
            v_row = vs[hd][r:r + 1, :]
            wp = w_prev[r:r + 1, hd:hd + 1]
            s = ss[hd][r:r + 1, :]
            qc = jnp.sum(q_col * c, axis=0, keepdims=True)
            qn = jnp.sum(qs[hd][r:r + 1, :] * n_row, axis=-1, keepdims=True)
            num = s * v_row + wp * qc
            den = s + wp * qn
            h = num / jnp.maximum(jnp.abs(den), floor[r:r + 1, hd:hd + 1])
            hg_ref[r:r + 1, hd * DH_M:(hd + 1) * DH_M] = og_ref[r:r + 1, hd * DH_M:(hd + 1) * DH_M] * h
            c_out[r, hd] = wp * c + k_col * v_row
            n_out[r, hd:hd + 1, :] = wp * n_row + kds[hd][r:r + 1, :]


def _mlstm_step(qkvm, ifg, og, c, n, m):
    nseq = qkvm.shape[0]
    row_spec = lambda w: pl.BlockSpec((DEC_ROWS, w), lambda i: (i, 0))
    c_spec = pl.BlockSpec((DEC_ROWS, NH_M, DH_M, DH_M), lambda i: (i, 0, 0, 0))
    n_spec = pl.BlockSpec((DEC_ROWS, NH_M, DH_M), lambda i: (i, 0, 0))
    return pl.pallas_call(
        _mlstm_step_kernel,
        out_shape=[jax.ShapeDtypeStruct((nseq, W_M), F32),
                   jax.ShapeDtypeStruct(c.shape, F32),
                   jax.ShapeDtypeStruct(n.shape, F32),
                   jax.ShapeDtypeStruct((nseq, LANES), F32)],
        grid=(nseq // DEC_ROWS,),
        in_specs=[row_spec(3 * W_M), row_spec(LANES), row_spec(W_M), c_spec, n_spec, row_spec(LANES)],
        out_specs=[row_spec(W_M), c_spec, n_spec, row_spec(LANES)],
        compiler_params=_cparams(("arbitrary",)),
        name="mlstm_step",
    )(qkvm, ifg, og, c, n, m)


def _ln_silu(y, g, b):
    yc = y - jnp.mean(y, axis=-1, keepdims=True)
    var = jnp.mean(yc * yc, axis=-1, keepdims=True)
    return _silu(yc * lax.rsqrt(var + LN_EPS) * g + b)


def _conv_step_kernel(st_ref, u_ref, w_ref, b_ref, g_ref, beta_ref, out_ref):
    w_hist = w_ref[0:CONV_W - 1, :]
    rows = []
    for r in range(DEC_ROWS):
        rows.append(jnp.sum(st_ref[r] * w_hist, axis=0, keepdims=True))
    acc = jnp.concatenate(rows, axis=0) + w_ref[CONV_W - 1:CONV_W, :] * u_ref[...] + b_ref[...]
    out_ref[...] = _ln_silu(acc, g_ref[...], beta_ref[...])


def _conv_step(state, u, w, b, g, beta):
    nseq = u.shape[0]
    return pl.pallas_call(
        _conv_step_kernel,
        out_shape=jax.ShapeDtypeStruct(u.shape, F32),
        grid=(nseq // DEC_ROWS,),
        in_specs=[pl.BlockSpec((DEC_ROWS, CONV_W - 1, C_CONV), lambda i: (i, 0, 0)),
                  pl.BlockSpec((DEC_ROWS, C_CONV), lambda i: (i, 0)),
                  _const_spec(w), _const_spec(b), _const_spec(g), _const_spec(beta)],
        out_specs=pl.BlockSpec((DEC_ROWS, C_CONV), lambda i: (i, 0)),
        compiler_params=_cparams(("arbitrary",)),
        name="dwconv_step",
    )(*_operands(state, u, w, b, g, beta))


def _stack_heads(q, low_half):
    heads = []
    for r in range(REP_A):
        slab = q[:, r * LANES:(r + 1) * LANES]
        heads.append(jnp.where(low_half, slab, 0.0))
        heads.append(jnp.where(low_half, 0.0, slab))
    return jnp.concatenate(heads, axis=0)


def _attn_block_kernel(base, carry, q_ref, k_ref, v_ref, k0_ref, v0t_ref, pos0_ref, sink_ref,
                       o_ref, kprev, vtprev):
    j = pl.program_id(1)
    nsb = q_ref.shape[0]

    if carry:
        @pl.when(j == 0)
        def _():
            kprev[...] = jnp.broadcast_to(k0_ref[...], kprev.shape).astype(kprev.dtype)
            vtprev[...] = jnp.broadcast_to(v0t_ref[...], vtprev.shape).astype(vtprev.dtype)

    kpos_prev = jnp.where(j == 0, pos0_ref[...],
                          base + BLK * j - BLK + lax.broadcasted_iota(jnp.int32, (BLK, 1), 0))
    kpos = jnp.concatenate([kpos_prev, base + BLK * j + lax.broadcasted_iota(jnp.int32, (BLK, 1), 0)], axis=0)
    qpos = base + BLK * j + lax.broadcasted_iota(jnp.int32, (1, BLK), 1)
    bias = jnp.where((kpos >= 0) & (kpos <= qpos) & (qpos - kpos <= WINDOW), 0.0, -jnp.inf)
    low_half = lax.broadcasted_iota(jnp.int32, (BLK, LANES), 1) < DH_A
    scale = DH_A ** -0.5

    for sb in range(nsb):
        k = k_ref[sb]
        vt = v_ref[sb].astype(F32).T.astype(BF16)
        k_prev, vt_prev = (kprev[sb], vtprev[sb]) if carry else (k0_ref[0], v0t_ref[0])
        k_cat = jnp.concatenate([k_prev.astype(BF16), k.astype(BF16)], axis=0)
        vt_cat = jnp.concatenate([vt_prev.astype(BF16), vt], axis=1)
        st = _dot_nt(k_cat, _stack_heads(q_ref[sb] * scale, low_half))
        probs, inv = [], []
        for hd in range(H_A):
            r, g = divmod(hd, KV_A)
            sink = sink_ref[0:1, g * REP_A + r:g * REP_A + r + 1]
            s = st[:, hd * BLK:(hd + 1) * BLK] + bias
            mx = jnp.maximum(jnp.max(s, axis=0, keepdims=True), sink)
            p = jnp.exp(s - mx)
            inv.append(1.0 / (jnp.sum(p, axis=0, keepdims=True) + jnp.exp(sink - mx)))
            probs.append(p.astype(BF16))
        ot = jnp.dot(vt_cat, jnp.concatenate(probs, axis=1), preferred_element_type=F32)
        out_t = jnp.concatenate(
            [ot[(hd % KV_A) * DH_A:(hd % KV_A + 1) * DH_A, hd * BLK:(hd + 1) * BLK] * inv[hd]
             for hd in range(H_A)], axis=0)
        o_ref[sb] = out_t.T.astype(o_ref.dtype)
        if carry:
            kprev[sb] = k
            vtprev[sb] = vt


def _attn_blocks(q, k, v, k0, v0t, pos0, sinks, nsb, base):
    nseq, seq, _ = q.shape
    q_spec = pl.BlockSpec((nsb, BLK, W_Q), lambda s, j: (s, j, 0))
    k_spec = pl.BlockSpec((nsb, BLK, W_KV), lambda s, j: (s, j, 0))
    return pl.pallas_call(
        functools.partial(_attn_block_kernel, base, seq > BLK),
        out_shape=jax.ShapeDtypeStruct(q.shape, q.dtype),
        grid=(nseq // nsb, seq // BLK),
        in_specs=[q_spec, k_spec, k_spec, _const_spec(k0), _const_spec(v0t),
                  _const_spec(pos0), _const_spec(sinks)],
        out_specs=q_spec,
        scratch_shapes=[pltpu.VMEM((nsb, BLK, W_KV), k.dtype), pltpu.VMEM((nsb, W_KV, BLK), BF16)],
        compiler_params=_cparams(("arbitrary", "arbitrary")),
        name="swa",
    )(*_operands(q, k, v, k0, v0t, pos0, sinks))


def _attn_step_kernel(base, q_ref, k_ref, v_ref, kc_ref, vc_ref, sink_ref, o_ref):
    low_half = lax.broadcasted_iota(jnp.int32, (1, LANES), 1) < DH_A
    sinks = jnp.concatenate([sink_ref[0:1, g * REP_A + r:g * REP_A + r + 1]
                             for r in range(REP_A) for g in range(KV_A)], axis=0)
    kpos = base - WINDOW + lax.broadcasted_iota(jnp.int32, (1, 2 * BLK), 1)
    bias = jnp.where((kpos >= 0) & (kpos <= base) & (base - kpos <= WINDOW), 0.0, -jnp.inf)
    pad = jnp.zeros((BLK - 1, W_KV), F32)
    scale = DH_A ** -0.5
    for sb in range(q_ref.shape[0]):
        k_cat = jnp.concatenate([kc_ref[sb], k_ref[sb], pad], axis=0)
        v_cat = jnp.concatenate([vc_ref[sb], v_ref[sb], pad], axis=0)
        s = _dot_nt(_stack_heads(q_ref[sb] * scale, low_half), k_cat) + bias
        mx = jnp.maximum(jnp.max(s, axis=-1, keepdims=True), sinks)
        p = jnp.exp(s - mx)
        den = jnp.sum(p, axis=-1, keepdims=True) + jnp.exp(sinks - mx)
        o = _dot(p * (1.0 / den), v_cat)
        for r in range(REP_A):
            o_ref[sb, :, r * LANES:(r + 1) * LANES] = jnp.where(low_half, o[2 * r:2 * r + 1], o[2 * r + 1:2 * r + 2])


def _attn_step(q, k, v, k_cache, v_cache, sinks, base):
    nseq = q.shape[0]
    spec = lambda rows, n: pl.BlockSpec((DEC_ROWS, rows, n), lambda i: (i, 0, 0))
    return pl.pallas_call(
        functools.partial(_attn_step_kernel, base),
        out_shape=jax.ShapeDtypeStruct(q.shape, F32),
        grid=(nseq // DEC_ROWS,),
        in_specs=[spec(1, W_Q), spec(1, W_KV), spec(1, W_KV), spec(WINDOW, W_KV), spec(WINDOW, W_KV),
                  _const_spec(sinks)],
        out_specs=spec(1, W_Q),
        compiler_params=_cparams(("arbitrary",)),
        name="swa_step",
    )(*_operands(q, k, v, k_cache, v_cache, sinks))


def _merge_kernel(x_ref, hg_ref, cv_ref, oa_ref, gates_ref, wm_ref, wc_ref, wa_ref, wo_ref, out_ref):
    d_model = x_ref.shape[1]
    mix = (gates_ref[:, 0:d_model] * _dot(hg_ref[...], wm_ref[...])
           + gates_ref[:, d_model:2 * d_model] * _dot(cv_ref[...], wc_ref[...])
           + gates_ref[:, 2 * d_model:3 * d_model] * _dot(oa_ref[...], wa_ref[...]))
    out_ref[...] = x_ref[...] + _dot(mix, wo_ref[...])


def _merge(x, hg, cv, oa, gates, wm, wc, wa, wo, tm):
    rows, d_model = x.shape
    row_spec = lambda n: pl.BlockSpec((tm, n), lambda i: (i, 0))
    return pl.pallas_call(
        _merge_kernel,
        out_shape=jax.ShapeDtypeStruct(x.shape, F32),
        grid=(rows // tm,),
        in_specs=[row_spec(d_model), row_spec(W_M), row_spec(C_CONV), row_spec(W_Q), row_spec(3 * d_model),
                  _const_spec(wm), _const_spec(wc), _const_spec(wa), _const_spec(wo)],
        out_specs=row_spec(d_model),
        compiler_params=_cparams(("arbitrary",)),
        name="merge",
    )(*_operands(x, hg, cv, oa, gates, wm, wc, wa, wo))


def _ffn_body(x, g2_ref, wg_ref, wu_ref, wd_ref, wc_ref, bc_ref, prev_rows):
    d_ff = wg_ref.shape[1]
    tiles = d_ff // MXU_TILE
    split = (tiles + 1) // 2 * MXU_TILE if tiles >= 2 else d_ff
    bounds = [(0, split), (split, d_ff)] if split < d_ff else [(0, d_ff)]
    h2 = _rms(x, g2_ref[...]).astype(BF16)
    acc = x
    gps = []
    for lo, hi in bounds:
        gp = jnp.dot(h2, wg_ref[:, lo:hi], preferred_element_type=F32)
        up = jnp.dot(h2, wu_ref[:, lo:hi], preferred_element_type=F32)
        p2, p1 = prev_rows(gp, lo, hi)
        gc = (wc_ref[0:1, lo:hi] * p2 + wc_ref[1:2, lo:hi] * p1 + wc_ref[2:3, lo:hi] * gp
              + bc_ref[:, lo:hi])
        acc = acc + _dot(_silu(gc) * up, wd_ref[lo:hi, :])
        gps.append(gp)
    return acc, gps


def _ffn_seq_kernel(tiles_per_seq, tail_end, final, x_ref, g2_ref, wg_ref, wu_ref, wd_ref, wc_ref, bc_ref,
                    init_ref, gf_ref, out_ref, tail_ref, carry):
    @pl.when(pl.program_id(0) % tiles_per_seq == 0)
    def _():
        carry[...] = init_ref[...]

    tm = x_ref.shape[0]
    row = lax.broadcasted_iota(jnp.int32, (tm, 1), 0)

    def prev_rows(gp, lo, hi):
        c2 = carry[SUBLANES - 2:SUBLANES - 1, lo:hi]
        c1 = carry[SUBLANES - 1:SUBLANES, lo:hi]
        p1 = jnp.where(row == 0, c1, pltpu.roll(gp, 1, axis=0))
        p2 = jnp.where(row == 0, c2, jnp.where(row == 1, c1, pltpu.roll(gp, 2, axis=0)))
        return p2, p1

    acc, gps = _ffn_body(x_ref[...], g2_ref, wg_ref, wu_ref, wd_ref, wc_ref, bc_ref, prev_rows)
    out_ref[...] = _rms(acc, gf_ref[...]) if final else acc
    tail = jnp.concatenate([gp[tail_end - SUBLANES:tail_end, :] for gp in gps], axis=1)
    tail_ref[...] = tail
    if tiles_per_seq > 1:
        carry[...] = tail


def _ffn_seq(x, g2, wg, wu, wd, wc, bc, init, gf, tm, tiles_per_seq, tail_end, final):
    rows, d_model = x.shape
    d_ff = wg.shape[1]
    assert tiles_per_seq == 1 or tail_end == tm
    return pl.pallas_call(
        functools.partial(_ffn_seq_kernel, tiles_per_seq, tail_end, final),
        out_shape=[jax.ShapeDtypeStruct(x.shape, F32),
                   jax.ShapeDtypeStruct((rows // tm, SUBLANES, d_ff), F32)],
        grid=(rows // tm,),
        in_specs=[pl.BlockSpec((tm, d_model), lambda i: (i, 0)), _const_spec(g2),
                  _const_spec(wg), _const_spec(wu), _const_spec(wd),
                  _const_spec(wc), _const_spec(bc), _const_spec(init),
                  _const_spec(gf)],
        out_specs=[pl.BlockSpec((tm, d_model), lambda i: (i, 0)),
                   pl.BlockSpec((None, SUBLANES, d_ff), lambda i: (i, 0, 0))],
        scratch_shapes=[pltpu.VMEM((SUBLANES, d_ff), F32)],
        compiler_params=_cparams(("arbitrary",)),
        name="convffn",
    )(*_operands(x, g2, wg, wu, wd, wc, bc, init, gf))


def _ffn_step_kernel(final, x_ref, g2_ref, wg_ref, wu_ref, wd_ref, wc_ref, bc_ref,
                     p2_ref, p1_ref, gf_ref, out_ref, gp_ref):
    def prev_rows(gp, lo, hi):
        return p2_ref[:, lo:hi], p1_ref[:, lo:hi]

    acc, gps = _ffn_body(x_ref[...], g2_ref, wg_ref, wu_ref, wd_ref, wc_ref, bc_ref, prev_rows)
    out_ref[...] = _rms(acc, gf_ref[...]) if final else acc
    gp_ref[...] = jnp.concatenate(gps, axis=1)


def _ffn_step(x, g2, wg, wu, wd, wc, bc, p2, p1, gf, tm, final):
    rows, d_model = x.shape
    d_ff = wg.shape[1]
    row_spec = lambda n: pl.BlockSpec((tm, n), lambda i: (i, 0))
    return pl.pallas_call(
        functools.partial(_ffn_step_kernel, final),
        out_shape=[jax.ShapeDtypeStruct(x.shape, F32), jax.ShapeDtypeStruct((rows, d_ff), F32)],
        grid=(rows // tm,),
        in_specs=[row_spec(d_model), _const_spec(g2),
                  _const_spec(wg), _const_spec(wu), _const_spec(wd),
                  _const_spec(wc), _const_spec(bc), row_spec(d_ff), row_spec(d_ff),
                  _const_spec(gf)],
        out_specs=[row_spec(d_model), row_spec(d_ff)],
        compiler_params=_cparams(("arbitrary",)),
        name="convffn_step",
    )(*_operands(x, g2, wg, wu, wd, wc, bc, p2, p1, gf))


def _row_tile(rows, cap):
    tm = min(rows, cap)
    while rows % tm:
        tm //= 2
    return tm


def kernel(x_prompt, x_sample, state_mlstm_c, state_mlstm_n, state_mlstm_m, state_conv, cache_swa_k, cache_swa_v, state_ffn_conv, meta_tokens, norm1_g, w_in, b_igate, b_fgate, w_mlstm_out, w_dconv, b_dconv, ln_conv_g, ln_conv_b, w_conv_out, attn_sinks, w_attn_out, b_merge, w_out, norm2_g, w_ffn_gate, w_ffn_up, w_ffn_conv, b_ffn_conv, w_ffn_down, final_norm_g):
    bsz, seq, d_model = x_prompt.shape
    nseq_s = x_sample.shape[0]
    depth = w_in.shape[0]
    d_ff = w_ffn_gate.shape[2]
    nblk = seq // BLK
    assert seq % BLK == 0 and x_sample.shape[1] == 1 and nseq_s % DEC_ROWS == 0
    assert cache_swa_k.shape[2] == WINDOW
    act = BF16
    nsb = SEQ_PER_STEP if bsz % SEQ_PER_STEP == 0 else 1

    x_main = x_prompt.reshape(bsz * seq, d_model)
    x_meta = jnp.concatenate([meta_tokens.astype(F32), jnp.zeros((BLK - N_META, d_model), F32)], axis=0)
    x_smp = x_sample.reshape(nseq_s, d_model)

    tm_main = _row_tile(seq, 512)
    tm_smp = _row_tile(nseq_s, 128)

    tab_meta = _rope_tables(jnp.arange(BLK))
    tab_main = _rope_tables(N_META + jnp.arange(seq))
    tab_smp = _rope_tables(jnp.full((tm_smp,), PAST_LEN))
    q_perm = jnp.arange(W_Q).reshape(KV_A, REP_A, DH_A).transpose(1, 0, 2).reshape(W_Q)
    iota_blk = jnp.arange(BLK, dtype=jnp.int32)[:, None]
    pos0_meta = jnp.full((BLK, 1), -1, jnp.int32)
    pos0_main = jnp.where(iota_blk < N_META, iota_blk, -1)

    zeros_c = jnp.zeros((1, NH_M, DH_M, DH_M), F32)
    zeros_nm = jnp.zeros((1, SUBLANES, LANES), F32)
    zeros_kv = jnp.zeros((1, BLK, W_KV), act)
    gf = final_norm_g.reshape(1, d_model)

    col_gates = 3 * W_M
    col_o = col_gates + 2 * NH_M
    col_qa = col_o + W_M + 2 * C_CONV
    w_pack_all = jnp.concatenate(
        [w_in[:, :, :col_gates], w_in[:, :, col_o:col_qa], w_in[:, :, col_qa:col_qa + W_Q][:, :, q_perm],
         w_in[:, :, col_qa + W_Q:], w_in[:, :, col_gates:col_o],
         jnp.zeros((depth, d_model, LANES - 2 * NH_M), F32)], axis=2).astype(BF16)
    bif_all = jnp.concatenate([b_igate, b_fgate, jnp.zeros((depth, LANES - 2 * NH_M), F32)], axis=1)[:, None]
    w_dc_all = jnp.concatenate([w_dconv, jnp.zeros((depth, HIST - CONV_W, C_CONV), F32)], axis=1)
    sinks_all = jnp.concatenate([attn_sinks, jnp.zeros((depth, LANES - H_A), F32)], axis=1)[:, None]
    w_fc_all = jnp.concatenate([w_ffn_conv, jnp.zeros((depth, SUBLANES - FFN_CONV_W, d_ff), F32)], axis=1)
    stacks = dict(
        g1=norm1_g[:, None], w_pack=w_pack_all, bm=b_merge[:, None], bif=bif_all,
        w_dc=w_dc_all, b_dc=b_dconv[:, None], ln_g=ln_conv_g[:, None], ln_b=ln_conv_b[:, None],
        sinks=sinks_all, wm=w_mlstm_out.astype(BF16), wc=w_conv_out.astype(BF16),
        wa=w_attn_out[:, q_perm].astype(BF16), wo=w_out.astype(BF16), g2=norm2_g[:, None],
        wg=w_ffn_gate.astype(BF16), wu=w_ffn_up.astype(BF16), wd=w_ffn_down.astype(BF16),
        w_fc=w_fc_all, b_fc=b_ffn_conv[:, None])

    p_states = [[] for _ in range(7)]
    s_states = [[] for _ in range(7)]
    for l in range(depth):
        final = l == depth - 1
        prm = {name: _LayerParam(stack, l) for name, stack in stacks.items()}
        g1, w_pack, bm, bif = prm["g1"], prm["w_pack"], prm["bm"], prm["bif"]
        w_dc, b_dc, ln_g, ln_b, sinks = prm["w_dc"], prm["b_dc"], prm["ln_g"], prm["ln_b"], prm["sinks"]
        wm, wc, wa, wo, g2 = prm["wm"], prm["wc"], prm["wa"], prm["wo"], prm["g2"]
        wg, wu, wd, w_fc, b_fc = prm["wg"], prm["wu"], prm["wd"], prm["w_fc"], prm["b_fc"]

        conv_w = (w_dc, b_dc, ln_g, ln_b)
        qkvm, og, u, qa, ka, va, gates, ifg, cv = _in_proj(
            x_meta, g1, w_pack, bm, bif, tab_meta, BLK, act, (jnp.zeros((HIST, C_CONV), F32),) + conv_w, 1)
        hg, c_meta, n_meta, m_meta = _mlstm(qkvm[None], ifg[None], og[None], zeros_c, zeros_nm, zeros_nm,
                                            1, N_META)
        oa = _attn_blocks(qa[None], ka[None], va[None], zeros_kv, zeros_kv, pos0_meta, sinks, 1, 0)
        x1 = _merge(x_meta, hg[0], cv, oa[0], gates, wm, wc, wa, wo, BLK)
        x_meta, tail_meta = _ffn_seq(x1, g2, wg, wu, wd, w_fc, b_fc, jnp.zeros((SUBLANES, d_ff), F32), gf,
                                     BLK, 1, N_META, False)
        u_meta, ka_meta, va_meta = u, ka, va

        hist0 = jnp.concatenate([jnp.zeros((HIST - N_META, C_CONV), F32), u_meta[:N_META].astype(F32)], axis=0)
        qkvm, og, u, qa, ka, va, gates, ifg, cv = _in_proj(
            x_main, g1, w_pack, bm, bif, tab_main, tm_main, act, (hist0,) + conv_w, seq // tm_main)
        per_seq = lambda a: a.reshape(bsz, seq, a.shape[-1])
        hg, p_c, p_n, p_m = _mlstm(per_seq(qkvm), per_seq(ifg), per_seq(og), c_meta, n_meta, m_meta, nsb, BLK)
        oa = _attn_blocks(per_seq(qa), per_seq(ka), per_seq(va), ka_meta[None], va_meta.T[None], pos0_main,
                          sinks, nsb, N_META)
        flat = lambda a: a.reshape(bsz * seq, a.shape[-1])
        x1 = _merge(x_main, flat(hg), cv, flat(oa), gates, wm, wc, wa, wo, tm_main)
        x_main, tails = _ffn_seq(x1, g2, wg, wu, wd, w_fc, b_fc, tail_meta[0], gf,
                                 tm_main, seq // tm_main, tm_main, final)
        p_states[0].append(p_c)
        p_states[1].append(p_n[:, :NH_M, :])
        p_states[2].append(p_m[:, 0, :NH_M])
        p_states[3].append(per_seq(u)[:, seq - (CONV_W - 1):].astype(F32))
        last_window = lambda a: per_seq(a)[:, seq - WINDOW:].astype(F32).reshape(bsz, WINDOW, KV_A, DH_A)
        p_states[4].append(last_window(ka))
        p_states[5].append(last_window(va))
        p_states[6].append(tails.reshape(bsz, seq // tm_main, SUBLANES, d_ff)[:, -1, SUBLANES - (FFN_CONV_W - 1):])

        qkvm, og, u, qa, ka, va, gates, ifg = _in_proj(x_smp, g1, w_pack, bm, bif, tab_smp, tm_smp, F32)
        m_in = jnp.concatenate([state_mlstm_m[l], jnp.zeros((nseq_s, LANES - NH_M), F32)], axis=1)
        hg, s_c, s_n, s_m = _mlstm_step(qkvm, ifg, og, state_mlstm_c[l], state_mlstm_n[l], m_in)
        cv = _conv_step(state_conv[l], u, w_dc, b_dc, ln_g, ln_b)
        oa = _attn_step(qa[:, None], ka[:, None], va[:, None],
                        cache_swa_k[l].reshape(nseq_s, WINDOW, W_KV), cache_swa_v[l].reshape(nseq_s, WINDOW, W_KV),
                        sinks, PAST_LEN)
        x1 = _merge(x_smp, hg, cv, oa[:, 0], gates, wm, wc, wa, wo, tm_smp)
        x_smp, gp_s = _ffn_step(x1, g2, wg, wu, wd, w_fc, b_fc, state_ffn_conv[l][:, 0], state_ffn_conv[l][:, 1],
                                gf, tm_smp, final)
        s_states[0].append(s_c)
        s_states[1].append(s_n)
        s_states[2].append(s_m[:, :NH_M])
        s_states[3].append(jnp.concatenate([state_conv[l][:, 1:], u[:, None]], axis=1))
        s_states[4].append(jnp.concatenate(
            [cache_swa_k[l][:, 1:], ka.reshape(nseq_s, 1, KV_A, DH_A)], axis=1))
        s_states[5].append(jnp.concatenate(
            [cache_swa_v[l][:, 1:], va.reshape(nseq_s, 1, KV_A, DH_A)], axis=1))
        s_states[6].append(jnp.concatenate([state_ffn_conv[l][:, 1:], gp_s[:, None]], axis=1))

    y_prompt = x_main.reshape(bsz, seq, d_model)
    y_sample = x_smp.reshape(nseq_s, 1, d_model)
    return (y_prompt, y_sample, *(jnp.stack(a, axis=0) for a in p_states),
            *(jnp.stack(a, axis=0) for a in s_states))
```

```python
import functools
from typing import NamedTuple

import jax
import jax.numpy as jnp
from jax import lax
from jax.experimental import pallas as pl
from jax.experimental.pallas import tpu as pltpu

F32 = jnp.float32
BF16 = jnp.bfloat16

N_META = 16
NH_M = 4
DH_M = 128
W_M = NH_M * DH_M
C_CONV = 512
CONV_W = 31
H_A = 8
KV_A = 2
DH_A = 64
REP_A = H_A // KV_A
W_Q = H_A * DH_A
W_KV = KV_A * DH_A
ROT_DIM = DH_A // 4
ROT_HALF = ROT_DIM // 2
ROPE_THETA = 500000.0
WINDOW = 128
FFN_CONV_W = 3
PAST_LEN = 16384
RMS_EPS = 1e-6
LN_EPS = 1e-5

BLK = 128
LANES = 128
MXU_TILE = 256
SUBLANES = 8
HIST = 32
DEC_ROWS = 8
SEQ_PER_STEP = 8
VMEM_LIMIT = 52 * 1024 * 1024

OFF_QKVM = 0
OFF_O = OFF_QKVM + 3 * W_M
OFF_GLU = OFF_O + W_M
OFF_QA = OFF_GLU + 2 * C_CONV
OFF_KA = OFF_QA + W_Q
OFF_VA = OFF_KA + W_KV
OFF_G = OFF_VA + W_KV


def _cparams(sem):
    return pltpu.CompilerParams(dimension_semantics=sem, vmem_limit_bytes=VMEM_LIMIT)


class _LayerParam(NamedTuple):
    stack: jax.Array
    layer: int

    @property
    def shape(self):
        return self.stack.shape[1:]


def _const_spec(a):
    nd = len(a.shape)
    if isinstance(a, _LayerParam):
        layer = a.layer
        return pl.BlockSpec((None,) + tuple(a.shape), lambda *_: (layer,) + (0,) * nd,
                            pipeline_mode=pl.Buffered(1))
    return pl.BlockSpec(a.shape, lambda *_: (0,) * nd, pipeline_mode=pl.Buffered(1))


def _state_spec(a, rows):
    tail = tuple(a.shape[1:])
    layer = a.layer
    return pl.BlockSpec((None, rows) + tail, lambda i: (layer, i) + (0,) * len(tail))


def _operands(*args):
    return tuple(a.stack if isinstance(a, _LayerParam) else a for a in args)


def _rms(x, g):
    return x * lax.rsqrt(jnp.mean(x * x, axis=-1, keepdims=True) + RMS_EPS) * g


def _log_sigmoid(x):
    return jnp.minimum(x, 0.0) - jnp.log1p(jnp.exp(-jnp.abs(x)))


def _sigmoid(x):
    return 1.0 / (1.0 + jnp.exp(-x))


def _silu(x):
    return x * _sigmoid(x)


def _dot(a, b):
    return jnp.dot(a.astype(BF16), b.astype(BF16), preferred_element_type=F32)


def _dot_nt(a, b):
    return lax.dot_general(a.astype(BF16), b.astype(BF16), (((1,), (1,)), ((), ())),
                           preferred_element_type=F32)


def _rope(x, cos, sin_lo, sin_hi):
    width = x.shape[1]
    reps = width // LANES
    if reps > 1:
        cos, sin_lo, sin_hi = (jnp.concatenate([t] * reps, axis=1) for t in (cos, sin_lo, sin_hi))
    return (x * cos + pltpu.roll(x, width - ROT_HALF, axis=1) * sin_lo
            + pltpu.roll(x, ROT_HALF, axis=1) * sin_hi)


def _rope_tables(pos):
    inv = jnp.power(ROPE_THETA, -jnp.arange(ROT_HALF, dtype=F32) / ROT_HALF)
    ang = pos.astype(F32)[:, None] * inv[None, :]
    cos, sin = jnp.cos(ang), jnp.sin(ang)
    npos = pos.shape[0]
    ones = jnp.ones((npos, DH_A - ROT_DIM), F32)
    zeros = jnp.zeros((npos, DH_A - ROT_DIM), F32)
    zh = jnp.zeros((npos, ROT_HALF), F32)
    head = lambda parts: jnp.concatenate(parts * (LANES // DH_A), axis=1)
    return (head([cos, cos, ones]), head([-sin, zh, zeros]), head([zh, sin, zeros]))


def _dwconv_block(window, w_ref, b_ref, g_ref, beta_ref):
    acc = jnp.broadcast_to(b_ref[...], (BLK, C_CONV))
    for sub in range(SUBLANES):
        shifted = window if sub == 0 else pltpu.roll(window, HIST + BLK - sub, axis=0)
        for grp in range(HIST // SUBLANES + 1):
            tap = grp * SUBLANES + sub - (HIST - (CONV_W - 1))
            if 0 <= tap < CONV_W:
                acc = acc + w_ref[tap:tap + 1, :] * shifted[grp * SUBLANES:grp * SUBLANES + BLK, :]
    return _ln_silu(acc, g_ref[...], beta_ref[...])


def _in_proj_kernel(conv_tiles, x_ref, g_ref, w_ref, bm_ref, bif_ref, cos_ref, sin_lo_ref, sin_hi_ref, *refs):
    if conv_tiles:
        hist0_ref, wdc_ref, bdc_ref, lng_ref, lnb_ref = refs[:5]
        qkvm_ref, og_ref, u_ref, qa_ref, ka_ref, va_ref, gates_ref, ifg_ref, cv_ref, buf = refs[5:]
    else:
        qkvm_ref, og_ref, u_ref, qa_ref, ka_ref, va_ref, gates_ref, ifg_ref = refs
    tm, d_model = x_ref.shape
    h = _rms(x_ref[...], g_ref[...]).astype(BF16)
    act = qkvm_ref.dtype

    def proj(off, n):
        return jnp.dot(h, w_ref[:, off:off + n], preferred_element_type=F32)

    glu = proj(OFF_GLU, 2 * C_CONV)
    u = glu[:, :C_CONV] * _sigmoid(glu[:, C_CONV:])
    u_ref[...] = u.astype(act)
    if conv_tiles:
        @pl.when(pl.program_id(0) % conv_tiles == 0)
        def _():
            buf[0:HIST, :] = hist0_ref[...]

        buf[HIST:HIST + tm, :] = u

    def conv_block(blk):
        window = buf[blk * BLK:blk * BLK + HIST + BLK, :]
        cv_ref[blk * BLK:(blk + 1) * BLK, :] = _dwconv_block(
            window, wdc_ref, bdc_ref, lng_ref, lnb_ref).astype(act)

    def mlstm_qkv():
        qkvm_ref[:, 0:W_M] = proj(OFF_QKVM, W_M).astype(act)
        qkvm_ref[:, W_M:2 * W_M] = (proj(OFF_QKVM + W_M, W_M) * DH_M ** -0.5).astype(act)
        qkvm_ref[:, 2 * W_M:3 * W_M] = proj(OFF_QKVM + 2 * W_M, W_M).astype(act)

    def out_gate():
        og_ref[...] = _sigmoid(proj(OFF_O, W_M)).astype(act)

    def attn_qkv():
        tabs = (cos_ref[...], sin_lo_ref[...], sin_hi_ref[...])
        qa_ref[...] = _rope(proj(OFF_QA, W_Q), *tabs).astype(act)
        kv = proj(OFF_KA, 2 * W_KV)
        ka_ref[...] = _rope(kv[:, :W_KV], *tabs).astype(act)
        va_ref[...] = kv[:, W_KV:].astype(act)

    def merge_gates(part):
        cols = slice(part * d_model, (part + 1) * d_model)
        gates_ref[:, cols] = _sigmoid(proj(OFF_G + part * d_model, d_model) + bm_ref[:, cols]).astype(act)

    def log_gates():
        z = proj(OFF_G + 3 * d_model, LANES) + bif_ref[...]
        lane = lax.broadcasted_iota(jnp.int32, z.shape, 1)
        ifg_ref[...] = jnp.where(lane < NH_M, z, _log_sigmoid(z))

    steps = [mlstm_qkv, out_gate, attn_qkv] + [functools.partial(merge_gates, p) for p in range(3)] + [log_gates]
    nconv = tm // BLK if conv_tiles else 0
    for i, step in enumerate(steps):
        for blk in range(nconv * i // len(steps), nconv * (i + 1) // len(steps)):
            conv_block(blk)
        step()
    if conv_tiles > 1:
        buf[0:HIST, :] = buf[tm:tm + HIST, :]


def _in_proj(x, g, w, bm, bif, tabs, tm, act, conv=None, conv_tiles=0):
    rows, d_model = x.shape
    ntab = tabs[0].shape[0] // tm
    widths = (3 * W_M, W_M, C_CONV, W_Q, W_KV, W_KV, 3 * d_model, LANES) + ((C_CONV,) if conv_tiles else ())
    dtypes = (act,) * 7 + (F32,) + ((act,) if conv_tiles else ())
    conv = tuple(conv) if conv_tiles else ()
    row_spec = lambda n: pl.BlockSpec((tm, n), lambda i: (i, 0))
    tab_spec = pl.BlockSpec((tm, LANES), lambda i: (i % ntab, 0))
    return pl.pallas_call(
        functools.partial(_in_proj_kernel, conv_tiles),
        out_shape=[jax.ShapeDtypeStruct((rows, n), dt) for n, dt in zip(widths, dtypes)],
        grid=(rows // tm,),
        in_specs=[row_spec(d_model), _const_spec(g), _const_spec(w),
                  _const_spec(bm), _const_spec(bif), tab_spec, tab_spec, tab_spec]
                 + [_const_spec(a) for a in conv],
        out_specs=[row_spec(n) for n in widths],
        scratch_shapes=[pltpu.VMEM((HIST + tm, C_CONV), F32)] if conv_tiles else [],
        compiler_params=_cparams(("arbitrary",)),
        name="in_proj",
    )(*_operands(x, g, w, bm, bif, *tabs, *conv))


def _mlstm_kernel(nvalid, qkv_ref, ifg_ref, og_ref, c0_ref, n0_ref, m0_ref,
                  hg_ref, c_ref, n_ref, m_ref):
    @pl.when(pl.program_id(1) == 0)
    def _():
        c_ref[...] = jnp.broadcast_to(c0_ref[...], c_ref.shape)
        n_ref[...] = jnp.broadcast_to(n0_ref[...], n_ref.shape)
        m_ref[...] = jnp.broadcast_to(m0_ref[...], m_ref.shape)

    row = lax.broadcasted_iota(jnp.int32, (BLK, LANES), 0)
    lane = lax.broadcasted_iota(jnp.int32, (BLK, LANES), 1)
    causal = (lax.broadcasted_iota(jnp.int32, (BLK, BLK), 0)
              >= lax.broadcasted_iota(jnp.int32, (BLK, BLK), 1))

    for sb in range(qkv_ref.shape[0]):
        gate = jnp.where(lane < 2 * NH_M, ifg_ref[sb], 0.0)
        if nvalid < BLK:
            gate = jnp.where(row >= nvalid, jnp.where(lane < NH_M, -jnp.inf, 0.0), gate)
        csum = jnp.where(lane < NH_M, 0.0, gate)
        shift = 1
        while shift < BLK:
            csum = csum + jnp.where(row >= shift, pltpu.roll(csum, shift, axis=0), 0.0)
            shift *= 2
        b_all = pltpu.roll(csum, LANES - NH_M, axis=1)
        c_all = gate - b_all
        cmax = c_all
        shift = 1
        while shift < BLK:
            cmax = jnp.maximum(cmax, jnp.where(row >= shift, pltpu.roll(cmax, shift, axis=0), -jnp.inf))
            shift *= 2
        m_prev_all = m_ref[sb, 0:1, :]
        mm_all = jnp.maximum(cmax, m_prev_all)
        w_prev_all = jnp.exp(m_prev_all - mm_all)
        floor_all = jnp.exp(-(b_all + mm_all))
        mm_last = mm_all[BLK - 1:BLK, :]
        wts_all = jnp.exp(c_all - mm_last)
        decay_all = w_prev_all[BLK - 1:BLK, :]
        m_ref[sb] = jnp.broadcast_to(b_all[BLK - 1:BLK, :] + mm_last, (SUBLANES, LANES))
        c_rows = c_all.T

        heads = []
        for hd in range(NH_M):
            sl = slice(hd * DH_M, (hd + 1) * DH_M)
            q = qkv_ref[sb, :, sl]
            k = qkv_ref[sb, :, W_M + hd * DH_M:W_M + (hd + 1) * DH_M].astype(F32)
            v = qkv_ref[sb, :, 2 * W_M + hd * DH_M:2 * W_M + (hd + 1) * DH_M].astype(BF16)
            c = c_ref[sb, hd]
            n_row = n_ref[sb, hd:hd + 1, :]
            decay = decay_all[:, hd:hd + 1]

            qk = _dot_nt(q, k)
            qc = _dot(q, c)
            qn = jnp.sum(q.astype(F32) * n_row, axis=-1, keepdims=True)
            dmat = jnp.exp(jnp.where(causal, c_rows[hd:hd + 1, :] - mm_all[:, hd:hd + 1], -jnp.inf))
            kw = k * wts_all[:, hd:hd + 1]
            c_ref[sb, hd] = decay * c + lax.dot_general(
                kw.astype(BF16), v, (((0,), (0,)), ((), ())), preferred_element_type=F32)
            n_ref[sb, hd:hd + 1, :] = decay * n_row + jnp.sum(kw, axis=0, keepdims=True)
            heads.append((sl, v, qk * dmat, w_prev_all[:, hd:hd + 1], qc, qn, floor_all[:, hd:hd + 1]))

        for sl, v, s, w_prev, qc, qn, floor in heads:
            num = _dot(s, v) + w_prev * qc
            den = jnp.sum(s, axis=-1, keepdims=True) + w_prev * qn
            h = num / jnp.maximum(jnp.abs(den), floor)
            hg_ref[sb, :, sl] = (og_ref[sb, :, sl].astype(F32) * h).astype(hg_ref.dtype)


def _mlstm(qkvm, ifg, og, c0, n0, m0, nsb, nvalid):
    nseq, seq, _ = qkvm.shape
    blk_spec = lambda n: pl.BlockSpec((nsb, BLK, n), lambda b, j: (b, j, 0))
    return pl.pallas_call(
        functools.partial(_mlstm_kernel, nvalid),
        out_shape=[jax.ShapeDtypeStruct((nseq, seq, W_M), og.dtype),
                   jax.ShapeDtypeStruct((nseq, NH_M, DH_M, DH_M), F32),
                   jax.ShapeDtypeStruct((nseq, SUBLANES, LANES), F32),
                   jax.ShapeDtypeStruct((nseq, SUBLANES, LANES), F32)],
        grid=(nseq // nsb, seq // BLK),
        in_specs=[blk_spec(3 * W_M), blk_spec(LANES), blk_spec(W_M),
                  _const_spec(c0), _const_spec(n0), _const_spec(m0)],
        out_specs=[blk_spec(W_M),
                   pl.BlockSpec((nsb, NH_M, DH_M, DH_M), lambda b, j: (b, 0, 0, 0)),
                   pl.BlockSpec((nsb, SUBLANES, LANES), lambda b, j: (b, 0, 0)),
                   pl.BlockSpec((nsb, SUBLANES, LANES), lambda b, j: (b, 0, 0))],
        compiler_params=_cparams(("arbitrary", "arbitrary")),
        name="mlstm",
    )(qkvm, ifg, og, c0, n0, m0)


def _mlstm_step_kernel(qkv_ref, ifg_ref, og_ref, c_ref, n_ref, m_ref,
                       hg_ref, c_out, n_out, m_out):
    gate = ifg_ref[...]
    ig = gate
    lf = pltpu.roll(gate, LANES - NH_M, axis=1)
    log_prev = lf + m_ref[...]
    m_t = jnp.maximum(log_prev, ig)
    dmat = jnp.exp(ig - m_t)
    w_prev = jnp.exp(log_prev - m_t)
    floor = jnp.exp(-m_t)
    m_out[...] = m_t

    qs, kds, vs, ss = [], [], [], []
    for hd in range(NH_M):
        q = qkv_ref[:, hd * DH_M:(hd + 1) * DH_M]
        k = qkv_ref[:, W_M + hd * DH_M:W_M + (hd + 1) * DH_M]
        v = qkv_ref[:, 2 * W_M + hd * DH_M:2 * W_M + (hd + 1) * DH_M]
        d_h = dmat[:, hd:hd + 1]
        qs.append(q)
        kds.append(k * d_h)
        vs.append(v)
        ss.append(jnp.sum(q * k, axis=-1, keepdims=True) * d_h)
    stacked = jnp.concatenate(qs + kds + [jnp.zeros((BLK - 2 * NH_M * DEC_ROWS, DH_M), F32)], axis=0)
    cols = stacked.T

    for r in range(DEC_ROWS):
        for hd in range(NH_M):
            q_col = cols[:, hd * DEC_ROWS + r:hd * DEC_ROWS + r + 1]
            k_col = cols[:, (NH_M + hd) * DEC_ROWS + r:(NH_M + hd) * DEC_ROWS + r + 1]
            c = c_ref[r, hd]
            n_row = n_ref[r, hd:hd + 1, :]
            v_row = vs[hd][r:r + 1, :]
            wp = w_prev[r:r + 1, hd:hd + 1]
            s = ss[hd][r:r + 1, :]
            qc = jnp.sum(q_col * c, axis=0, keepdims=True)
            qn = jnp.sum(qs[hd][r:r + 1, :] * n_row, axis=-1, keepdims=True)
            num = s * v_row + wp * qc
            den = s + wp * qn
            h = num / jnp.maximum(jnp.abs(den), floor[r:r + 1, hd:hd + 1])
            hg_ref[r:r + 1, hd * DH_M:(hd + 1) * DH_M] = og_ref[r:r + 1, hd * DH_M:(hd + 1) * DH_M] * h
            c_out[r, hd] = wp * c + k_col * v_row
            n_out[r, hd:hd + 1, :] = wp * n_row + kds[hd][r:r + 1, :]


def _mlstm_step(qkvm, ifg, og, c, n, m):
    nseq = qkvm.shape[0]
    row_spec = lambda w: pl.BlockSpec((DEC_ROWS, w), lambda i: (i, 0))
    c_spec = pl.BlockSpec((DEC_ROWS, NH_M, DH_M, DH_M), lambda i: (i, 0, 0, 0))
    n_spec = pl.BlockSpec((DEC_ROWS, NH_M, DH_M), lambda i: (i, 0, 0))
    return pl.pallas_call(
        _mlstm_step_kernel,
        out_shape=[jax.ShapeDtypeStruct((nseq, W_M), F32),
                   jax.ShapeDtypeStruct(c.shape, F32),
                   jax.ShapeDtypeStruct(n.shape, F32),
                   jax.ShapeDtypeStruct((nseq, LANES), F32)],
        grid=(nseq // DEC_ROWS,),
        in_specs=[row_spec(3 * W_M), row_spec(LANES), row_spec(W_M),
                  _state_spec(c, DEC_ROWS), _state_spec(n, DEC_ROWS), row_spec(LANES)],
        out_specs=[row_spec(W_M), c_spec, n_spec, row_spec(LANES)],
        compiler_params=_cparams(("arbitrary",)),
        name="mlstm_step",
    )(*_operands(qkvm, ifg, og, c, n, m))


def _ln_silu(y, g, b):
    yc = y - jnp.mean(y, axis=-1, keepdims=True)
    var = jnp.mean(yc * yc, axis=-1, keepdims=True)
    return _silu(yc * lax.rsqrt(var + LN_EPS) * g + b)


def _conv_step_kernel(st_ref, u_ref, w_ref, b_ref, g_ref, beta_ref, out_ref):
    w_hist = w_ref[0:CONV_W - 1, :]
    rows = []
    for r in range(DEC_ROWS):
        rows.append(jnp.sum(st_ref[r] * w_hist, axis=0, keepdims=True))
    acc = jnp.concatenate(rows, axis=0) + w_ref[CONV_W - 1:CONV_W, :] * u_ref[...] + b_ref[...]
    out_ref[...] = _ln_silu(acc, g_ref[...], beta_ref[...])


def _conv_step(state, u, w, b, g, beta):
    nseq = u.shape[0]
    return pl.pallas_call(
        _conv_step_kernel,
        out_shape=jax.ShapeDtypeStruct(u.shape, F32),
        grid=(nseq // DEC_ROWS,),
        in_specs=[_state_spec(state, DEC_ROWS),
                  pl.BlockSpec((DEC_ROWS, C_CONV), lambda i: (i, 0)),
                  _const_spec(w), _const_spec(b), _const_spec(g), _const_spec(beta)],
        out_specs=pl.BlockSpec((DEC_ROWS, C_CONV), lambda i: (i, 0)),
        compiler_params=_cparams(("arbitrary",)),
        name="dwconv_step",
    )(*_operands(state, u, w, b, g, beta))


def _stack_heads(q, low_half):
    heads = []
    for r in range(REP_A):
        slab = q[:, r * LANES:(r + 1) * LANES]
        heads.append(jnp.where(low_half, slab, 0.0))
        heads.append(jnp.where(low_half, 0.0, slab))
    return jnp.concatenate(heads, axis=0)


def _attn_block_kernel(base, carry, q_ref, k_ref, v_ref, k0_ref, v0t_ref, pos0_ref, sink_ref,
                       o_ref, kprev, vtprev):
    j = pl.program_id(1)
    nsb = q_ref.shape[0]

    if carry:
        @pl.when(j == 0)
        def _():
            kprev[...] = jnp.broadcast_to(k0_ref[...], kprev.shape).astype(kprev.dtype)
            vtprev[...] = jnp.broadcast_to(v0t_ref[...], vtprev.shape).astype(vtprev.dtype)

    kpos_prev = jnp.where(j == 0, pos0_ref[...],
                          base + BLK * j - BLK + lax.broadcasted_iota(jnp.int32, (BLK, 1), 0))
    kpos = jnp.concatenate([kpos_prev, base + BLK * j + lax.broadcasted_iota(jnp.int32, (BLK, 1), 0)], axis=0)
    qpos = base + BLK * j + lax.broadcasted_iota(jnp.int32, (1, BLK), 1)
    bias = jnp.where((kpos >= 0) & (kpos <= qpos) & (qpos - kpos <= WINDOW), 0.0, -jnp.inf)
    low_half = lax.broadcasted_iota(jnp.int32, (BLK, LANES), 1) < DH_A
    scale = DH_A ** -0.5

    for sb in range(nsb):
        k = k_ref[sb]
        vt = v_ref[sb].astype(F32).T.astype(BF16)
        k_prev, vt_prev = (kprev[sb], vtprev[sb]) if carry else (k0_ref[0], v0t_ref[0])
        k_cat = jnp.concatenate([k_prev.astype(BF16), k.astype(BF16)], axis=0)
        vt_cat = jnp.concatenate([vt_prev.astype(BF16), vt], axis=1)
        st = _dot_nt(k_cat, _stack_heads(q_ref[sb] * scale, low_half))
        probs, inv = [], []
        for hd in range(H_A):
            r, g = divmod(hd, KV_A)
            sink = sink_ref[0:1, g * REP_A + r:g * REP_A + r + 1]
            s = st[:, hd * BLK:(hd + 1) * BLK] + bias
            mx = jnp.maximum(jnp.max(s, axis=0, keepdims=True), sink)
            p = jnp.exp(s - mx)
            inv.append(1.0 / (jnp.sum(p, axis=0, keepdims=True) + jnp.exp(sink - mx)))
            probs.append(p.astype(BF16))
        ot = jnp.dot(vt_cat, jnp.concatenate(probs, axis=1), preferred_element_type=F32)
        out_t = jnp.concatenate(
            [ot[(hd % KV_A) * DH_A:(hd % KV_A + 1) * DH_A, hd * BLK:(hd + 1) * BLK] * inv[hd]
             for hd in range(H_A)], axis=0)
        o_ref[sb] = out_t.T.astype(o_ref.dtype)
        if carry:
            kprev[sb] = k
            vtprev[sb] = vt


def _attn_blocks(q, k, v, k0, v0t, pos0, sinks, nsb, base):
    nseq, seq, _ = q.shape
    q_spec = pl.BlockSpec((nsb, BLK, W_Q), lambda s, j: (s, j, 0))
    k_spec = pl.BlockSpec((nsb, BLK, W_KV), lambda s, j: (s, j, 0))
    return pl.pallas_call(
        functools.partial(_attn_block_kernel, base, seq > BLK),
        out_shape=jax.ShapeDtypeStruct(q.shape, q.dtype),
        grid=(nseq // nsb, seq // BLK),
        in_specs=[q_spec, k_spec, k_spec, _const_spec(k0), _const_spec(v0t),
                  _const_spec(pos0), _const_spec(sinks)],
        out_specs=q_spec,
        scratch_shapes=[pltpu.VMEM((nsb, BLK, W_KV), k.dtype), pltpu.VMEM((nsb, W_KV, BLK), BF16)],
        compiler_params=_cparams(("arbitrary", "arbitrary")),
        name="swa",
    )(*_operands(q, k, v, k0, v0t, pos0, sinks))


def _attn_step_kernel(base, q_ref, k_ref, v_ref, kc_ref, vc_ref, sink_ref, o_ref):
    low_half = lax.broadcasted_iota(jnp.int32, (1, LANES), 1) < DH_A
    sinks = jnp.concatenate([sink_ref[0:1, g * REP_A + r:g * REP_A + r + 1]
                             for r in range(REP_A) for g in range(KV_A)], axis=0)
    kpos = base - WINDOW + lax.broadcasted_iota(jnp.int32, (1, 2 * BLK), 1)
    bias = jnp.where((kpos >= 0) & (kpos <= base) & (base - kpos <= WINDOW), 0.0, -jnp.inf)
    pad = jnp.zeros((BLK - 1, W_KV), F32)
    scale = DH_A ** -0.5
    for sb in range(q_ref.shape[0]):
        k_cat = jnp.concatenate([kc_ref[sb], k_ref[sb], pad], axis=0)
        v_cat = jnp.concatenate([vc_ref[sb], v_ref[sb], pad], axis=0)
        s = _dot_nt(_stack_heads(q_ref[sb] * scale, low_half), k_cat) + bias
        mx = jnp.maximum(jnp.max(s, axis=-1, keepdims=True), sinks)
        p = jnp.exp(s - mx)
        den = jnp.sum(p, axis=-1, keepdims=True) + jnp.exp(sinks - mx)
        o = _dot(p * (1.0 / den), v_cat)
        for r in range(REP_A):
            o_ref[sb, :, r * LANES:(r + 1) * LANES] = jnp.where(low_half, o[2 * r:2 * r + 1], o[2 * r + 1:2 * r + 2])


def _attn_step(q, k, v, k_cache, v_cache, sinks, base):
    nseq = q.shape[0]
    spec = lambda rows, n: pl.BlockSpec((DEC_ROWS, rows, n), lambda i: (i, 0, 0))
    return pl.pallas_call(
        functools.partial(_attn_step_kernel, base),
        out_shape=jax.ShapeDtypeStruct(q.shape, F32),
        grid=(nseq // DEC_ROWS,),
        in_specs=[spec(1, W_Q), spec(1, W_KV), spec(1, W_KV), spec(WINDOW, W_KV), spec(WINDOW, W_KV),
                  _const_spec(sinks)],
        out_specs=spec(1, W_Q),
        compiler_params=_cparams(("arbitrary",)),
        name="swa_step",
    )(*_operands(q, k, v, k_cache, v_cache, sinks))


def _merge_kernel(x_ref, hg_ref, cv_ref, oa_ref, gates_ref, wm_ref, wc_ref, wa_ref, wo_ref, out_ref):
    d_model = x_ref.shape[1]
    mix = (gates_ref[:, 0:d_model] * _dot(hg_ref[...], wm_ref[...])
           + gates_ref[:, d_model:2 * d_model] * _dot(cv_ref[...], wc_ref[...])
           + gates_ref[:, 2 * d_model:3 * d_model] * _dot(oa_ref[...], wa_ref[...]))
    out_ref[...] = x_ref[...] + _dot(mix, wo_ref[...])


def _merge(x, hg, cv, oa, gates, wm, wc, wa, wo, tm):
    rows, d_model = x.shape
    row_spec = lambda n: pl.BlockSpec((tm, n), lambda i: (i, 0))
    return pl.pallas_call(
        _merge_kernel,
        out_shape=jax.ShapeDtypeStruct(x.shape, F32),
        grid=(rows // tm,),
        in_specs=[row_spec(d_model), row_spec(W_M), row_spec(C_CONV), row_spec(W_Q), row_spec(3 * d_model),
                  _const_spec(wm), _const_spec(wc), _const_spec(wa), _const_spec(wo)],
        out_specs=row_spec(d_model),
        compiler_params=_cparams(("arbitrary",)),
        name="merge",
    )(*_operands(x, hg, cv, oa, gates, wm, wc, wa, wo))


def _ffn_body(x, g2_ref, wg_ref, wu_ref, wd_ref, wc_ref, bc_ref, prev_rows):
    d_ff = wg_ref.shape[1]
    tiles = d_ff // MXU_TILE
    split = (tiles + 1) // 2 * MXU_TILE if tiles >= 2 else d_ff
    bounds = [(0, split), (split, d_ff)] if split < d_ff else [(0, d_ff)]
    h2 = _rms(x, g2_ref[...]).astype(BF16)
    acc = x
    gps = []
    for lo, hi in bounds:
        gp = jnp.dot(h2, wg_ref[:, lo:hi], preferred_element_type=F32)
        up = jnp.dot(h2, wu_ref[:, lo:hi], preferred_element_type=F32)
        p2, p1 = prev_rows(gp, lo, hi)
        gc = (wc_ref[0:1, lo:hi] * p2 + wc_ref[1:2, lo:hi] * p1 + wc_ref[2:3, lo:hi] * gp
              + bc_ref[:, lo:hi])
        acc = acc + _dot(_silu(gc) * up, wd_ref[lo:hi, :])
        gps.append(gp)
    return acc, gps


def _ffn_seq_kernel(tiles_per_seq, tail_end, final, x_ref, g2_ref, wg_ref, wu_ref, wd_ref, wc_ref, bc_ref,
                    init_ref, gf_ref, out_ref, tail_ref, carry):
    @pl.when(pl.program_id(0) % tiles_per_seq == 0)
    def _():
        carry[...] = init_ref[...]

    tm = x_ref.shape[0]
    row = lax.broadcasted_iota(jnp.int32, (tm, 1), 0)

    def prev_rows(gp, lo, hi):
        c2 = carry[SUBLANES - 2:SUBLANES - 1, lo:hi]
        c1 = carry[SUBLANES - 1:SUBLANES, lo:hi]
        p1 = jnp.where(row == 0, c1, pltpu.roll(gp, 1, axis=0))
        p2 = jnp.where(row == 0, c2, jnp.where(row == 1, c1, pltpu.roll(gp, 2, axis=0)))
        return p2, p1

    acc, gps = _ffn_body(x_ref[...], g2_ref, wg_ref, wu_ref, wd_ref, wc_ref, bc_ref, prev_rows)
    out_ref[...] = _rms(acc, gf_ref[...]) if final else acc
    tail = jnp.concatenate([gp[tail_end - SUBLANES:tail_end, :] for gp in gps], axis=1)
    tail_ref[...] = tail
    if tiles_per_seq > 1:
        carry[...] = tail


def _ffn_seq(x, g2, wg, wu, wd, wc, bc, init, gf, tm, tiles_per_seq, tail_end, final):
    rows, d_model = x.shape
    d_ff = wg.shape[1]
    assert tiles_per_seq == 1 or tail_end == tm
    return pl.pallas_call(
        functools.partial(_ffn_seq_kernel, tiles_per_seq, tail_end, final),
        out_shape=[jax.ShapeDtypeStruct(x.shape, F32),
                   jax.ShapeDtypeStruct((rows // tm, SUBLANES, d_ff), F32)],
        grid=(rows // tm,),
        in_specs=[pl.BlockSpec((tm, d_model), lambda i: (i, 0)), _const_spec(g2),
                  _const_spec(wg), _const_spec(wu), _const_spec(wd),
                  _const_spec(wc), _const_spec(bc), _const_spec(init),
                  _const_spec(gf)],
        out_specs=[pl.BlockSpec((tm, d_model), lambda i: (i, 0)),
                   pl.BlockSpec((None, SUBLANES, d_ff), lambda i: (i, 0, 0))],
        scratch_shapes=[pltpu.VMEM((SUBLANES, d_ff), F32)],
        compiler_params=_cparams(("arbitrary",)),
        name="convffn",
    )(*_operands(x, g2, wg, wu, wd, wc, bc, init, gf))


def _ffn_step_kernel(final, x_ref, g2_ref, wg_ref, wu_ref, wd_ref, wc_ref, bc_ref,
                     p2_ref, p1_ref, gf_ref, out_ref, gp_ref):
    def prev_rows(gp, lo, hi):
        return p2_ref[:, lo:hi], p1_ref[:, lo:hi]

    acc, gps = _ffn_body(x_ref[...], g2_ref, wg_ref, wu_ref, wd_ref, wc_ref, bc_ref, prev_rows)
    out_ref[...] = _rms(acc, gf_ref[...]) if final else acc
    gp_ref[...] = jnp.concatenate(gps, axis=1)


def _ffn_step(x, g2, wg, wu, wd, wc, bc, p2, p1, gf, tm, final):
    rows, d_model = x.shape
    d_ff = wg.shape[1]
    row_spec = lambda n: pl.BlockSpec((tm, n), lambda i: (i, 0))
    return pl.pallas_call(
        functools.partial(_ffn_step_kernel, final),
        out_shape=[jax.ShapeDtypeStruct(x.shape, F32), jax.ShapeDtypeStruct((rows, d_ff), F32)],
        grid=(rows // tm,),
        in_specs=[row_spec(d_model), _const_spec(g2),
                  _const_spec(wg), _const_spec(wu), _const_spec(wd),
                  _const_spec(wc), _const_spec(bc), row_spec(d_ff), row_spec(d_ff),
                  _const_spec(gf)],
        out_specs=[row_spec(d_model), row_spec(d_ff)],
        compiler_params=_cparams(("arbitrary",)),
        name="convffn_step",
    )(*_operands(x, g2, wg, wu, wd, wc, bc, p2, p1, gf))


def _row_tile(rows, cap):
    tm = min(rows, cap)
    while rows % tm:
        tm //= 2
    return tm


def kernel(x_prompt, x_sample, state_mlstm_c, state_mlstm_n, state_mlstm_m, state_conv, cache_swa_k, cache_swa_v, state_ffn_conv, meta_tokens, norm1_g, w_in, b_igate, b_fgate, w_mlstm_out, w_dconv, b_dconv, ln_conv_g, ln_conv_b, w_conv_out, attn_sinks, w_attn_out, b_merge, w_out, norm2_g, w_ffn_gate, w_ffn_up, w_ffn_conv, b_ffn_conv, w_ffn_down, final_norm_g):
    bsz, seq, d_model = x_prompt.shape
    nseq_s = x_sample.shape[0]
    depth = w_in.shape[0]
    d_ff = w_ffn_gate.shape[2]
    nblk = seq // BLK
    assert seq % BLK == 0 and x_sample.shape[1] == 1 and nseq_s % DEC_ROWS == 0
    assert cache_swa_k.shape[2] == WINDOW
    act = BF16
    nsb = SEQ_PER_STEP if bsz % SEQ_PER_STEP == 0 else 1

    x_main = x_prompt.reshape(bsz * seq, d_model)
    x_meta = jnp.concatenate([meta_tokens.astype(F32), jnp.zeros((BLK - N_META, d_model), F32)], axis=0)
    x_smp = x_sample.reshape(nseq_s, d_model)

    tm_main = _row_tile(seq, 512)
    tm_smp = _row_tile(nseq_s, 128)

    tab_meta = _rope_tables(jnp.arange(BLK))
    tab_main = _rope_tables(N_META + jnp.arange(seq))
    tab_smp = _rope_tables(jnp.full((tm_smp,), PAST_LEN))
    def regroup_heads(w, axis):
        shape = w.shape
        w = w.reshape(shape[:axis] + (KV_A, REP_A, DH_A) + shape[axis + 1:])
        return jnp.swapaxes(w, axis, axis + 1).reshape(shape)
    iota_blk = jnp.arange(BLK, dtype=jnp.int32)[:, None]
    pos0_meta = jnp.full((BLK, 1), -1, jnp.int32)
    pos0_main = jnp.where(iota_blk < N_META, iota_blk, -1)

    zeros_c = jnp.zeros((1, NH_M, DH_M, DH_M), F32)
    zeros_nm = jnp.zeros((1, SUBLANES, LANES), F32)
    zeros_kv = jnp.zeros((1, BLK, W_KV), act)
    gf = final_norm_g.reshape(1, d_model)

    col_gates = 3 * W_M
    col_o = col_gates + 2 * NH_M
    col_qa = col_o + W_M + 2 * C_CONV
    w_pack_all = jnp.concatenate(
        [w_in[:, :, :col_gates], w_in[:, :, col_o:col_qa], regroup_heads(w_in[:, :, col_qa:col_qa + W_Q], 2),
         w_in[:, :, col_qa + W_Q:], w_in[:, :, col_gates:col_o],
         jnp.zeros((depth, d_model, LANES - 2 * NH_M), F32)], axis=2).astype(BF16)
    bif_all = jnp.concatenate([b_igate, b_fgate, jnp.zeros((depth, LANES - 2 * NH_M), F32)], axis=1)[:, None]
    w_dc_all = jnp.concatenate([w_dconv, jnp.zeros((depth, HIST - CONV_W, C_CONV), F32)], axis=1)
    sinks_all = jnp.concatenate([attn_sinks, jnp.zeros((depth, LANES - H_A), F32)], axis=1)[:, None]
    w_fc_all = jnp.concatenate([w_ffn_conv, jnp.zeros((depth, SUBLANES - FFN_CONV_W, d_ff), F32)], axis=1)
    stacks = dict(
        g1=norm1_g[:, None], w_pack=w_pack_all, bm=b_merge[:, None], bif=bif_all,
        w_dc=w_dc_all, b_dc=b_dconv[:, None], ln_g=ln_conv_g[:, None], ln_b=ln_conv_b[:, None],
        sinks=sinks_all, wm=w_mlstm_out.astype(BF16), wc=w_conv_out.astype(BF16),
        wa=regroup_heads(w_attn_out, 1).astype(BF16), wo=w_out.astype(BF16), g2=norm2_g[:, None],
        wg=w_ffn_gate.astype(BF16), wu=w_ffn_up.astype(BF16), wd=w_ffn_down.astype(BF16),
        w_fc=w_fc_all, b_fc=b_ffn_conv[:, None])

    p_states = [[] for _ in range(7)]
    s_states = [[] for _ in range(7)]
    for l in range(depth):
        final = l == depth - 1
        prm = {name: _LayerParam(stack, l) for name, stack in stacks.items()}
        g1, w_pack, bm, bif = prm["g1"], prm["w_pack"], prm["bm"], prm["bif"]
        w_dc, b_dc, ln_g, ln_b, sinks = prm["w_dc"], prm["b_dc"], prm["ln_g"], prm["ln_b"], prm["sinks"]
        wm, wc, wa, wo, g2 = prm["wm"], prm["wc"], prm["wa"], prm["wo"], prm["g2"]
        wg, wu, wd, w_fc, b_fc = prm["wg"], prm["wu"], prm["wd"], prm["w_fc"], prm["b_fc"]

        conv_w = (w_dc, b_dc, ln_g, ln_b)
        qkvm, og, u, qa, ka, va, gates, ifg, cv = _in_proj(
            x_meta, g1, w_pack, bm, bif, tab_meta, BLK, act, (jnp.zeros((HIST, C_CONV), F32),) + conv_w, 1)
        hg, c_meta, n_meta, m_meta = _mlstm(qkvm[None], ifg[None], og[None], zeros_c, zeros_nm, zeros_nm,
                                            1, N_META)
        oa = _attn_blocks(qa[None], ka[None], va[None], zeros_kv, zeros_kv, pos0_meta, sinks, 1, 0)
        x1 = _merge(x_meta, hg[0], cv, oa[0], gates, wm, wc, wa, wo, BLK)
        x_meta, tail_meta = _ffn_seq(x1, g2, wg, wu, wd, w_fc, b_fc, jnp.zeros((SUBLANES, d_ff), F32), gf,
                                     BLK, 1, N_META, False)
        u_meta, ka_meta, va_meta = u, ka, va

        hist0 = jnp.concatenate([jnp.zeros((HIST - N_META, C_CONV), F32), u_meta[:N_META].astype(F32)], axis=0)
        qkvm, og, u, qa, ka, va, gates, ifg, cv = _in_proj(
            x_main, g1, w_pack, bm, bif, tab_main, tm_main, act, (hist0,) + conv_w, seq // tm_main)
        per_seq = lambda a: a.reshape(bsz, seq, a.shape[-1])
        hg, p_c, p_n, p_m = _mlstm(per_seq(qkvm), per_seq(ifg), per_seq(og), c_meta, n_meta, m_meta, nsb, BLK)
        oa = _attn_blocks(per_seq(qa), per_seq(ka), per_seq(va), ka_meta[None], va_meta.T[None], pos0_main,
                          sinks, nsb, N_META)
        flat = lambda a: a.reshape(bsz * seq, a.shape[-1])
        x1 = _merge(x_main, flat(hg), cv, flat(oa), gates, wm, wc, wa, wo, tm_main)
        x_main, tails = _ffn_seq(x1, g2, wg, wu, wd, w_fc, b_fc, tail_meta[0], gf,
                                 tm_main, seq // tm_main, tm_main, final)
        p_states[0].append(p_c)
        p_states[1].append(p_n[:, :NH_M, :])
        p_states[2].append(p_m[:, 0, :NH_M])
        p_states[3].append(per_seq(u)[:, seq - (CONV_W - 1):].astype(F32))
        last_window = lambda a: per_seq(a)[:, seq - WINDOW:].astype(F32).reshape(bsz, WINDOW, KV_A, DH_A)
        p_states[4].append(last_window(ka))
        p_states[5].append(last_window(va))
        p_states[6].append(tails.reshape(bsz, seq // tm_main, SUBLANES, d_ff)[:, -1, SUBLANES - (FFN_CONV_W - 1):])

        qkvm, og, u, qa, ka, va, gates, ifg = _in_proj(x_smp, g1, w_pack, bm, bif, tab_smp, tm_smp, F32)
        m_in = jnp.concatenate([state_mlstm_m[l], jnp.zeros((nseq_s, LANES - NH_M), F32)], axis=1)
        hg, s_c, s_n, s_m = _mlstm_step(qkvm, ifg, og, _LayerParam(state_mlstm_c, l),
                                        _LayerParam(state_mlstm_n, l), m_in)
        cv = _conv_step(_LayerParam(state_conv, l), u, w_dc, b_dc, ln_g, ln_b)
        oa = _attn_step(qa[:, None], ka[:, None], va[:, None],
                        cache_swa_k[l].reshape(nseq_s, WINDOW, W_KV), cache_swa_v[l].reshape(nseq_s, WINDOW, W_KV),
                        sinks, PAST_LEN)
        x1 = _merge(x_smp, hg, cv, oa[:, 0], gates, wm, wc, wa, wo, tm_smp)
        x_smp, gp_s = _ffn_step(x1, g2, wg, wu, wd, w_fc, b_fc, state_ffn_conv[l][:, 0], state_ffn_conv[l][:, 1],
                                gf, tm_smp, final)
        s_states[0].append(s_c)
        s_states[1].append(s_n)
        s_states[2].append(s_m[:, :NH_M])
        s_states[3].append(u[:, None])
        s_states[4].append(ka.reshape(nseq_s, 1, KV_A, DH_A))
        s_states[5].append(va.reshape(nseq_s, 1, KV_A, DH_A))
        s_states[6].append(gp_s[:, None])

    y_prompt = x_main.reshape(bsz, seq, d_model)
    y_sample = x_smp.reshape(nseq_s, 1, d_model)
    slide = lambda old, new_rows: jnp.concatenate([old[:, :, 1:], jnp.stack(new_rows, axis=0)], axis=2)
    s_out = [jnp.stack(a, axis=0) for a in s_states[:3]] + [
        slide(old, new) for old, new in zip((state_conv, cache_swa_k, cache_swa_v, state_ffn_conv), s_states[3:])]
    return (y_prompt, y_sample, *(jnp.stack(a, axis=0) for a in p_states), *s_out)
```

```python
import functools
from typing import NamedTuple

import jax
import jax.numpy as jnp
from jax import lax
from jax.experimental import pallas as pl
from jax.experimental.pallas import tpu as pltpu

F32 = jnp.float32
BF16 = jnp.bfloat16

N_META = 16
NH_M = 4
DH_M = 128
W_M = NH_M * DH_M
C_CONV = 512
CONV_W = 31
H_A = 8
KV_A = 2
DH_A = 64
REP_A = H_A // KV_A
W_Q = H_A * DH_A
W_KV = KV_A * DH_A
ROT_DIM = DH_A // 4
ROT_HALF = ROT_DIM // 2
ROPE_THETA = 500000.0
WINDOW = 128
FFN_CONV_W = 3
PAST_LEN = 16384
RMS_EPS = 1e-6
LN_EPS = 1e-5

BLK = 128
LANES = 128
MXU_TILE = 256
SUBLANES = 8
HIST = 32
DEC_ROWS = 8
SEQ_PER_STEP = 8
VMEM_LIMIT = 52 * 1024 * 1024

OFF_QKVM = 0
OFF_O = OFF_QKVM + 3 * W_M
OFF_GLU = OFF_O + W_M
OFF_QA = OFF_GLU + 2 * C_CONV
OFF_KA = OFF_QA + W_Q
OFF_VA = OFF_KA + W_KV
OFF_G = OFF_VA + W_KV


def _cparams(sem):
    return pltpu.CompilerParams(dimension_semantics=sem, vmem_limit_bytes=VMEM_LIMIT)


class _LayerParam(NamedTuple):
    stack: jax.Array
    layer: int

    @property
    def shape(self):
        return self.stack.shape[1:]


def _const_spec(a):
    nd = len(a.shape)
    if isinstance(a, _LayerParam):
        layer = a.layer
        return pl.BlockSpec((None,) + tuple(a.shape), lambda *_: (layer,) + (0,) * nd,
                            pipeline_mode=pl.Buffered(1))
    return pl.BlockSpec(a.shape, lambda *_: (0,) * nd, pipeline_mode=pl.Buffered(1))


def _state_spec(a, rows):
    tail = tuple(a.shape[1:])
    layer = a.layer
    return pl.BlockSpec((None, rows) + tail, lambda i: (layer, i) + (0,) * len(tail))


def _operands(*args):
    return tuple(a.stack if isinstance(a, _LayerParam) else a for a in args)


def _rms(x, g):
    return x * lax.rsqrt(jnp.mean(x * x, axis=-1, keepdims=True) + RMS_EPS) * g


def _log_sigmoid(x):
    return jnp.minimum(x, 0.0) - jnp.log1p(jnp.exp(-jnp.abs(x)))


def _sigmoid(x):
    return 1.0 / (1.0 + jnp.exp(-x))


def _silu(x):
    return x * _sigmoid(x)


def _dot(a, b):
    return jnp.dot(a.astype(BF16), b.astype(BF16), preferred_element_type=F32)


def _dot_nt(a, b):
    return lax.dot_general(a.astype(BF16), b.astype(BF16), (((1,), (1,)), ((), ())),
                           preferred_element_type=F32)


def _rope(x, cos, sin_lo, sin_hi):
    width = x.shape[1]
    reps = width // LANES
    if reps > 1:
        cos, sin_lo, sin_hi = (jnp.concatenate([t] * reps, axis=1) for t in (cos, sin_lo, sin_hi))
    return (x * cos + pltpu.roll(x, width - ROT_HALF, axis=1) * sin_lo
            + pltpu.roll(x, ROT_HALF, axis=1) * sin_hi)


def _rope_tables(pos):
    inv = jnp.power(ROPE_THETA, -jnp.arange(ROT_HALF, dtype=F32) / ROT_HALF)
    ang = pos.astype(F32)[:, None] * inv[None, :]
    cos, sin = jnp.cos(ang), jnp.sin(ang)
    npos = pos.shape[0]
    ones = jnp.ones((npos, DH_A - ROT_DIM), F32)
    zeros = jnp.zeros((npos, DH_A - ROT_DIM), F32)
    zh = jnp.zeros((npos, ROT_HALF), F32)
    head = lambda parts: jnp.concatenate(parts * (LANES // DH_A), axis=1)
    return (head([cos, cos, ones]), head([-sin, zh, zeros]), head([zh, sin, zeros]))


def _dwconv_block(window, w_ref, b_ref, g_ref, beta_ref):
    acc = jnp.broadcast_to(b_ref[...], (BLK, C_CONV))
    for sub in range(SUBLANES):
        shifted = window if sub == 0 else pltpu.roll(window, HIST + BLK - sub, axis=0)
        for grp in range(HIST // SUBLANES + 1):
            tap = grp * SUBLANES + sub - (HIST - (CONV_W - 1))
            if 0 <= tap < CONV_W:
                acc = acc + w_ref[tap:tap + 1, :] * shifted[grp * SUBLANES:grp * SUBLANES + BLK, :]
    return _ln_silu(acc, g_ref[...], beta_ref[...])


def _in_proj_kernel(conv_tiles, x_ref, g_ref, w_ref, bm_ref, bif_ref, cos_ref, sin_lo_ref, sin_hi_ref, *refs):
    if conv_tiles:
        hist0_ref, wdc_ref, bdc_ref, lng_ref, lnb_ref = refs[:5]
        qkvm_ref, og_ref, u_ref, qa_ref, ka_ref, va_ref, gates_ref, ifg_ref, cv_ref, buf = refs[5:]
    else:
        qkvm_ref, og_ref, u_ref, qa_ref, ka_ref, va_ref, gates_ref, ifg_ref = refs
    tm, d_model = x_ref.shape
    h = _rms(x_ref[...], g_ref[...]).astype(BF16)
    act = qkvm_ref.dtype

    def proj(off, n):
        return jnp.dot(h, w_ref[:, off:off + n], preferred_element_type=F32)

    glu = proj(OFF_GLU, 2 * C_CONV)
    u = glu[:, :C_CONV] * _sigmoid(glu[:, C_CONV:])
    u_ref[...] = u.astype(act)
    if conv_tiles:
        @pl.when(pl.program_id(0) % conv_tiles == 0)
        def _():
            buf[0:HIST, :] = hist0_ref[...]

        buf[HIST:HIST + tm, :] = u

    def conv_block(blk):
        window = buf[blk * BLK:blk * BLK + HIST + BLK, :]
        cv_ref[blk * BLK:(blk + 1) * BLK, :] = _dwconv_block(
            window, wdc_ref, bdc_ref, lng_ref, lnb_ref).astype(act)

    def mlstm_qkv():
        qkvm_ref[:, 0:W_M] = proj(OFF_QKVM, W_M).astype(act)
        qkvm_ref[:, W_M:2 * W_M] = (proj(OFF_QKVM + W_M, W_M) * DH_M ** -0.5).astype(act)
        qkvm_ref[:, 2 * W_M:3 * W_M] = proj(OFF_QKVM + 2 * W_M, W_M).astype(act)

    def out_gate():
        og_ref[...] = _sigmoid(proj(OFF_O, W_M)).astype(act)

    def attn_qkv():
        tabs = (cos_ref[...], sin_lo_ref[...], sin_hi_ref[...])
        qa_ref[...] = _rope(proj(OFF_QA, W_Q), *tabs).astype(act)
        kv = proj(OFF_KA, 2 * W_KV)
        ka_ref[...] = _rope(kv[:, :W_KV], *tabs).astype(act)
        va_ref[...] = kv[:, W_KV:].astype(act)

    def merge_gates(part):
        cols = slice(part * d_model, (part + 1) * d_model)
        gates_ref[:, cols] = _sigmoid(proj(OFF_G + part * d_model, d_model) + bm_ref[:, cols]).astype(act)

    def log_gates():
        z = proj(OFF_G + 3 * d_model, LANES) + bif_ref[...]
        lane = lax.broadcasted_iota(jnp.int32, z.shape, 1)
        ifg_ref[...] = jnp.where(lane < NH_M, z, _log_sigmoid(z))

    steps = [mlstm_qkv, out_gate, attn_qkv] + [functools.partial(merge_gates, p) for p in range(3)] + [log_gates]
    nconv = tm // BLK if conv_tiles else 0
    for i, step in enumerate(steps):
        for blk in range(nconv * i // len(steps), nconv * (i + 1) // len(steps)):
            conv_block(blk)
        step()
    if conv_tiles > 1:
        buf[0:HIST, :] = buf[tm:tm + HIST, :]


def _in_proj(x, g, w, bm, bif, tabs, tm, act, conv=None, conv_tiles=0):
    rows, d_model = x.shape
    ntab = tabs[0].shape[0] // tm
    widths = (3 * W_M, W_M, C_CONV, W_Q, W_KV, W_KV, 3 * d_model, LANES) + ((C_CONV,) if conv_tiles else ())
    dtypes = (act,) * 7 + (F32,) + ((act,) if conv_tiles else ())
    conv = tuple(conv) if conv_tiles else ()
    row_spec = lambda n: pl.BlockSpec((tm, n), lambda i: (i, 0))
    tab_spec = pl.BlockSpec((tm, LANES), lambda i: (i % ntab, 0))
    return pl.pallas_call(
        functools.partial(_in_proj_kernel, conv_tiles),
        out_shape=[jax.ShapeDtypeStruct((rows, n), dt) for n, dt in zip(widths, dtypes)],
        grid=(rows // tm,),
        in_specs=[row_spec(d_model), _const_spec(g), _const_spec(w),
                  _const_spec(bm), _const_spec(bif), tab_spec, tab_spec, tab_spec]
                 + [_const_spec(a) for a in conv],
        out_specs=[row_spec(n) for n in widths],
        scratch_shapes=[pltpu.VMEM((HIST + tm, C_CONV), F32)] if conv_tiles else [],
        compiler_params=_cparams(("arbitrary",)),
        name="in_proj",
    )(*_operands(x, g, w, bm, bif, *tabs, *conv))


def _mlstm_kernel(nvalid, qkv_ref, ifg_ref, og_ref, c0_ref, n0_ref, m0_ref,
                  hg_ref, c_ref, n_ref, m_ref):
    @pl.when(pl.program_id(1) == 0)
    def _():
        c_ref[...] = jnp.broadcast_to(c0_ref[...], c_ref.shape)
        n_ref[...] = jnp.broadcast_to(n0_ref[...], n_ref.shape)
        m_ref[...] = jnp.broadcast_to(m0_ref[...], m_ref.shape)

    row = lax.broadcasted_iota(jnp.int32, (BLK, LANES), 0)
    lane = lax.broadcasted_iota(jnp.int32, (BLK, LANES), 1)
    causal = (lax.broadcasted_iota(jnp.int32, (BLK, BLK), 0)
              >= lax.broadcasted_iota(jnp.int32, (BLK, BLK), 1))

    heads = []
    for sb in range(qkv_ref.shape[0]):
        gate = jnp.where(lane < 2 * NH_M, ifg_ref[sb], 0.0)
        if nvalid < BLK:
            gate = jnp.where(row >= nvalid, jnp.where(lane < NH_M, -jnp.inf, 0.0), gate)
        csum = jnp.where(lane < NH_M, 0.0, gate)
        shift = 1
        while shift < BLK:
            csum = csum + jnp.where(row >= shift, pltpu.roll(csum, shift, axis=0), 0.0)
            shift *= 2
        b_all = pltpu.roll(csum, LANES - NH_M, axis=1)
        c_all = gate - b_all
        cmax = c_all
        shift = 1
        while shift < BLK:
            cmax = jnp.maximum(cmax, jnp.where(row >= shift, pltpu.roll(cmax, shift, axis=0), -jnp.inf))
            shift *= 2
        m_prev_all = m_ref[sb, 0:1, :]
        mm_all = jnp.maximum(cmax, m_prev_all)
        w_prev_all = jnp.exp(m_prev_all - mm_all)
        floor_all = jnp.exp(-(b_all + mm_all))
        mm_last = mm_all[BLK - 1:BLK, :]
        wts_all = jnp.exp(c_all - mm_last)
        decay_all = w_prev_all[BLK - 1:BLK, :]
        m_ref[sb] = jnp.broadcast_to(b_all[BLK - 1:BLK, :] + mm_last, (SUBLANES, LANES))
        c_rows = c_all.T

        for hd in range(NH_M):
            sl = slice(hd * DH_M, (hd + 1) * DH_M)
            q = qkv_ref[sb, :, sl]
            k = qkv_ref[sb, :, W_M + hd * DH_M:W_M + (hd + 1) * DH_M].astype(F32)
            v = qkv_ref[sb, :, 2 * W_M + hd * DH_M:2 * W_M + (hd + 1) * DH_M].astype(BF16)
            c = c_ref[sb, hd]
            n_row = n_ref[sb, hd:hd + 1, :]
            decay = decay_all[:, hd:hd + 1]

            qk = _dot_nt(q, k)
            qc = _dot(q, c)
            qn = jnp.sum(q.astype(F32) * n_row, axis=-1, keepdims=True)
            dmat = jnp.exp(jnp.where(causal, c_rows[hd:hd + 1, :] - mm_all[:, hd:hd + 1], -jnp.inf))
            kw = k * wts_all[:, hd:hd + 1]
            c_ref[sb, hd] = decay * c + lax.dot_general(
                kw.astype(BF16), v, (((0,), (0,)), ((), ())), preferred_element_type=F32)
            n_ref[sb, hd:hd + 1, :] = decay * n_row + jnp.sum(kw, axis=0, keepdims=True)
            heads.append((sb, sl, v, qk * dmat, w_prev_all[:, hd:hd + 1], qc, qn, floor_all[:, hd:hd + 1]))

    for sb, sl, v, s, w_prev, qc, qn, floor in heads:
        num = _dot(s, v) + w_prev * qc
        den = jnp.sum(s, axis=-1, keepdims=True) + w_prev * qn
        h = num / jnp.maximum(jnp.abs(den), floor)
        hg_ref[sb, :, sl] = (og_ref[sb, :, sl].astype(F32) * h).astype(hg_ref.dtype)


def _mlstm(qkvm, ifg, og, c0, n0, m0, nsb, nvalid):
    nseq, seq, _ = qkvm.shape
    blk_spec = lambda n: pl.BlockSpec((nsb, BLK, n), lambda b, j: (b, j, 0))
    return pl.pallas_call(
        functools.partial(_mlstm_kernel, nvalid),
        out_shape=[jax.ShapeDtypeStruct((nseq, seq, W_M), og.dtype),
                   jax.ShapeDtypeStruct((nseq, NH_M, DH_M, DH_M), F32),
                   jax.ShapeDtypeStruct((nseq, SUBLANES, LANES), F32),
                   jax.ShapeDtypeStruct((nseq, SUBLANES, LANES), F32)],
        grid=(nseq // nsb, seq // BLK),
        in_specs=[blk_spec(3 * W_M), blk_spec(LANES), blk_spec(W_M),
                  _const_spec(c0), _const_spec(n0), _const_spec(m0)],
        out_specs=[blk_spec(W_M),
                   pl.BlockSpec((nsb, NH_M, DH_M, DH_M), lambda b, j: (b, 0, 0, 0)),
                   pl.BlockSpec((nsb, SUBLANES, LANES), lambda b, j: (b, 0, 0)),
                   pl.BlockSpec((nsb, SUBLANES, LANES), lambda b, j: (b, 0, 0))],
        compiler_params=_cparams(("arbitrary", "arbitrary")),
        name="mlstm",
    )(qkvm, ifg, og, c0, n0, m0)


def _mlstm_step_kernel(qkv_ref, ifg_ref, og_ref, c_ref, n_ref, m_ref,
                       hg_ref, c_out, n_out, m_out):
    gate = ifg_ref[...]
    ig = gate
    lf = pltpu.roll(gate, LANES - NH_M, axis=1)
    log_prev = lf + m_ref[...]
    m_t = jnp.maximum(log_prev, ig)
    dmat = jnp.exp(ig - m_t)
    w_prev = jnp.exp(log_prev - m_t)
    floor = jnp.exp(-m_t)
    m_out[...] = m_t

    qs, kds, vs, ss = [], [], [], []
    for hd in range(NH_M):
        q = qkv_ref[:, hd * DH_M:(hd + 1) * DH_M]
        k = qkv_ref[:, W_M + hd * DH_M:W_M + (hd + 1) * DH_M]
        v = qkv_ref[:, 2 * W_M + hd * DH_M:2 * W_M + (hd + 1) * DH_M]
        d_h = dmat[:, hd:hd + 1]
        qs.append(q)
        kds.append(k * d_h)
        vs.append(v)
        ss.append(jnp.sum(q * k, axis=-1, keepdims=True) * d_h)
    stacked = jnp.concatenate(qs + kds + [jnp.zeros((BLK - 2 * NH_M * DEC_ROWS, DH_M), F32)], axis=0)
    cols = stacked.T

    for r in range(DEC_ROWS):
        for hd in range(NH_M):
            q_col = cols[:, hd * DEC_ROWS + r:hd * DEC_ROWS + r + 1]
            k_col = cols[:, (NH_M + hd) * DEC_ROWS + r:(NH_M + hd) * DEC_ROWS + r + 1]
            c = c_ref[r, hd]
            n_row = n_ref[r, hd:hd + 1, :]
            v_row = vs[hd][r:r + 1, :]
            wp = w_prev[r:r + 1, hd:hd + 1]
            s = ss[hd][r:r + 1, :]
            qc = jnp.sum(q_col * c, axis=0, keepdims=True)
            qn = jnp.sum(qs[hd][r:r + 1, :] * n_row, axis=-1, keepdims=True)
            num = s * v_row + wp * qc
            den = s + wp * qn
            h = num / jnp.maximum(jnp.abs(den), floor[r:r + 1, hd:hd + 1])
            hg_ref[r:r + 1, hd * DH_M:(hd + 1) * DH_M] = og_ref[r:r + 1, hd * DH_M:(hd + 1) * DH_M] * h
            c_out[r, hd] = wp * c + k_col * v_row
            n_out[r, hd:hd + 1, :] = wp * n_row + kds[hd][r:r + 1, :]


def _mlstm_step(qkvm, ifg, og, c, n, m):
    nseq = qkvm.shape[0]
    row_spec = lambda w: pl.BlockSpec((DEC_ROWS, w), lambda i: (i, 0))
    c_spec = pl.BlockSpec((DEC_ROWS, NH_M, DH_M, DH_M), lambda i: (i, 0, 0, 0))
    n_spec = pl.BlockSpec((DEC_ROWS, NH_M, DH_M), lambda i: (i, 0, 0))
    return pl.pallas_call(
        _mlstm_step_kernel,
        out_shape=[jax.ShapeDtypeStruct((nseq, W_M), F32),
                   jax.ShapeDtypeStruct(c.shape, F32),
                   jax.ShapeDtypeStruct(n.shape, F32),
                   jax.ShapeDtypeStruct((nseq, LANES), F32)],
        grid=(nseq // DEC_ROWS,),
        in_specs=[row_spec(3 * W_M), row_spec(LANES), row_spec(W_M),
                  _state_spec(c, DEC_ROWS), _state_spec(n, DEC_ROWS), row_spec(LANES)],
        out_specs=[row_spec(W_M), c_spec, n_spec, row_spec(LANES)],
        compiler_params=_cparams(("arbitrary",)),
        name="mlstm_step",
    )(*_operands(qkvm, ifg, og, c, n, m))


def _ln_silu(y, g, b):
    yc = y - jnp.mean(y, axis=-1, keepdims=True)
    var = jnp.mean(yc * yc, axis=-1, keepdims=True)
    return _silu(yc * lax.rsqrt(var + LN_EPS) * g + b)


def _conv_step_kernel(st_ref, u_ref, w_ref, b_ref, g_ref, beta_ref, out_ref):
    w_hist = w_ref[0:CONV_W - 1, :]
    rows = []
    for r in range(DEC_ROWS):
        rows.append(jnp.sum(st_ref[r] * w_hist, axis=0, keepdims=True))
    acc = jnp.concatenate(rows, axis=0) + w_ref[CONV_W - 1:CONV_W, :] * u_ref[...] + b_ref[...]
    out_ref[...] = _ln_silu(acc, g_ref[...], beta_ref[...])


def _conv_step(state, u, w, b, g, beta):
    nseq = u.shape[0]
    return pl.pallas_call(
        _conv_step_kernel,
        out_shape=jax.ShapeDtypeStruct(u.shape, F32),
        grid=(nseq // DEC_ROWS,),
        in_specs=[_state_spec(state, DEC_ROWS),
                  pl.BlockSpec((DEC_ROWS, C_CONV), lambda i: (i, 0)),
                  _const_spec(w), _const_spec(b), _const_spec(g), _const_spec(beta)],
        out_specs=pl.BlockSpec((DEC_ROWS, C_CONV), lambda i: (i, 0)),
        compiler_params=_cparams(("arbitrary",)),
        name="dwconv_step",
    )(*_operands(state, u, w, b, g, beta))


def _stack_heads(q, low_half):
    heads = []
    for r in range(REP_A):
        slab = q[:, r * LANES:(r + 1) * LANES]
        heads.append(jnp.where(low_half, slab, 0.0))
        heads.append(jnp.where(low_half, 0.0, slab))
    return jnp.concatenate(heads, axis=0)


def _attn_block_kernel(base, carry, q_ref, k_ref, v_ref, k0_ref, v0t_ref, pos0_ref, sink_ref,
                       o_ref, kprev, vtprev):
    j = pl.program_id(1)
    nsb = q_ref.shape[0]

    if carry:
        @pl.when(j == 0)
        def _():
            kprev[...] = jnp.broadcast_to(k0_ref[...], kprev.shape).astype(kprev.dtype)
            vtprev[...] = jnp.broadcast_to(v0t_ref[...], vtprev.shape).astype(vtprev.dtype)

    kpos_prev = jnp.where(j == 0, pos0_ref[...],
                          base + BLK * j - BLK + lax.broadcasted_iota(jnp.int32, (BLK, 1), 0))
    kpos = jnp.concatenate([kpos_prev, base + BLK * j + lax.broadcasted_iota(jnp.int32, (BLK, 1), 0)], axis=0)
    qpos = base + BLK * j + lax.broadcasted_iota(jnp.int32, (1, BLK), 1)
    bias = jnp.where((kpos >= 0) & (kpos <= qpos) & (qpos - kpos <= WINDOW), 0.0, -jnp.inf)
    low_half = lax.broadcasted_iota(jnp.int32, (BLK, LANES), 1) < DH_A
    scale = DH_A ** -0.5

    scores, vt_cats = [], []
    for sb in range(nsb):
        k = k_ref[sb]
        vt = v_ref[sb].astype(F32).T.astype(BF16)
        k_prev, vt_prev = (kprev[sb], vtprev[sb]) if carry else (k0_ref[0], v0t_ref[0])
        k_cat = jnp.concatenate([k_prev.astype(BF16), k.astype(BF16)], axis=0)
        vt_cats.append(jnp.concatenate([vt_prev.astype(BF16), vt], axis=1))
        scores.append(_dot_nt(k_cat, _stack_heads(q_ref[sb] * scale, low_half)))
        if carry:
            kprev[sb] = k
            vtprev[sb] = vt

    probs, invs = [], []
    for st in scores:
        p_heads, inv = [], []
        for hd in range(H_A):
            r, g = divmod(hd, KV_A)
            sink = sink_ref[0:1, g * REP_A + r:g * REP_A + r + 1]
            s = st[:, hd * BLK:(hd + 1) * BLK] + bias
            mx = jnp.maximum(jnp.max(s, axis=0, keepdims=True), sink)
            p = jnp.exp(s - mx)
            inv.append(1.0 / (jnp.sum(p, axis=0, keepdims=True) + jnp.exp(sink - mx)))
            p_heads.append(p.astype(BF16))
        probs.append(jnp.concatenate(p_heads, axis=1))
        invs.append(inv)

    for sb in range(nsb):
        ot = jnp.dot(vt_cats[sb], probs[sb], preferred_element_type=F32)
        out_t = jnp.concatenate(
            [ot[(hd % KV_A) * DH_A:(hd % KV_A + 1) * DH_A, hd * BLK:(hd + 1) * BLK] * invs[sb][hd]
             for hd in range(H_A)], axis=0)
        o_ref[sb] = out_t.T.astype(o_ref.dtype)


def _attn_blocks(q, k, v, k0, v0t, pos0, sinks, nsb, base):
    nseq, seq, _ = q.shape
    q_spec = pl.BlockSpec((nsb, BLK, W_Q), lambda s, j: (s, j, 0))
    k_spec = pl.BlockSpec((nsb, BLK, W_KV), lambda s, j: (s, j, 0))
    return pl.pallas_call(
        functools.partial(_attn_block_kernel, base, seq > BLK),
        out_shape=jax.ShapeDtypeStruct(q.shape, q.dtype),
        grid=(nseq // nsb, seq // BLK),
        in_specs=[q_spec, k_spec, k_spec, _const_spec(k0), _const_spec(v0t),
                  _const_spec(pos0), _const_spec(sinks)],
        out_specs=q_spec,
        scratch_shapes=[pltpu.VMEM((nsb, BLK, W_KV), k.dtype), pltpu.VMEM((nsb, W_KV, BLK), BF16)],
        compiler_params=_cparams(("arbitrary", "arbitrary")),
        name="swa",
    )(*_operands(q, k, v, k0, v0t, pos0, sinks))


def _attn_step_kernel(base, q_ref, k_ref, v_ref, kc_ref, vc_ref, sink_ref, o_ref):
    low_half = lax.broadcasted_iota(jnp.int32, (1, LANES), 1) < DH_A
    sinks = jnp.concatenate([sink_ref[0:1, g * REP_A + r:g * REP_A + r + 1]
                             for r in range(REP_A) for g in range(KV_A)], axis=0)
    kpos = base - WINDOW + lax.broadcasted_iota(jnp.int32, (1, 2 * BLK), 1)
    bias = jnp.where((kpos >= 0) & (kpos <= base) & (base - kpos <= WINDOW), 0.0, -jnp.inf)
    pad = jnp.zeros((BLK - 1, W_KV), F32)
    scale = DH_A ** -0.5
    nsb = q_ref.shape[0]
    scores = []
    for sb in range(nsb):
        k_cat = jnp.concatenate([kc_ref[sb], k_ref[sb], pad], axis=0)
        scores.append(_dot_nt(_stack_heads(q_ref[sb] * scale, low_half), k_cat) + bias)
    probs = []
    for s in scores:
        mx = jnp.maximum(jnp.max(s, axis=-1, keepdims=True), sinks)
        p = jnp.exp(s - mx)
        den = jnp.sum(p, axis=-1, keepdims=True) + jnp.exp(sinks - mx)
        probs.append(p * (1.0 / den))
    for sb in range(nsb):
        v_cat = jnp.concatenate([vc_ref[sb], v_ref[sb], pad], axis=0)
        o = _dot(probs[sb], v_cat)
        for r in range(REP_A):
            o_ref[sb, :, r * LANES:(r + 1) * LANES] = jnp.where(low_half, o[2 * r:2 * r + 1], o[2 * r + 1:2 * r + 2])


def _attn_step(q, k, v, k_cache, v_cache, sinks, base):
    nseq = q.shape[0]
    spec = lambda rows, n: pl.BlockSpec((DEC_ROWS, rows, n), lambda i: (i, 0, 0))
    return pl.pallas_call(
        functools.partial(_attn_step_kernel, base),
        out_shape=jax.ShapeDtypeStruct(q.shape, F32),
        grid=(nseq // DEC_ROWS,),
        in_specs=[spec(1, W_Q), spec(1, W_KV), spec(1, W_KV), spec(WINDOW, W_KV), spec(WINDOW, W_KV),
                  _const_spec(sinks)],
        out_specs=spec(1, W_Q),
        compiler_params=_cparams(("arbitrary",)),
        name="swa_step",
    )(*_operands(q, k, v, k_cache, v_cache, sinks))


def _merge_kernel(x_ref, hg_ref, cv_ref, oa_ref, gates_ref, wm_ref, wc_ref, wa_ref, wo_ref, out_ref):
    d_model = x_ref.shape[1]
    mix = (gates_ref[:, 0:d_model] * _dot(hg_ref[...], wm_ref[...])
           + gates_ref[:, d_model:2 * d_model] * _dot(cv_ref[...], wc_ref[...])
           + gates_ref[:, 2 * d_model:3 * d_model] * _dot(oa_ref[...], wa_ref[...]))
    out_ref[...] = x_ref[...] + _dot(mix, wo_ref[...])


def _merge(x, hg, cv, oa, gates, wm, wc, wa, wo, tm):
    rows, d_model = x.shape
    row_spec = lambda n: pl.BlockSpec((tm, n), lambda i: (i, 0))
    return pl.pallas_call(
        _merge_kernel,
        out_shape=jax.ShapeDtypeStruct(x.shape, F32),
        grid=(rows // tm,),
        in_specs=[row_spec(d_model), row_spec(W_M), row_spec(C_CONV), row_spec(W_Q), row_spec(3 * d_model),
                  _const_spec(wm), _const_spec(wc), _const_spec(wa), _const_spec(wo)],
        out_specs=row_spec(d_model),
        compiler_params=_cparams(("arbitrary",)),
        name="merge",
    )(*_operands(x, hg, cv, oa, gates, wm, wc, wa, wo))


def _ffn_body(x, g2_ref, wg_ref, wu_ref, wd_ref, wc_ref, bc_ref, prev_rows):
    d_ff = wg_ref.shape[1]
    tiles = d_ff // MXU_TILE
    split = (tiles + 1) // 2 * MXU_TILE if tiles >= 2 else d_ff
    bounds = [(0, split), (split, d_ff)] if split < d_ff else [(0, d_ff)]
    h2 = _rms(x, g2_ref[...]).astype(BF16)
    acc = x
    gps = []
    for lo, hi in bounds:
        gp = jnp.dot(h2, wg_ref[:, lo:hi], preferred_element_type=F32)
        up = jnp.dot(h2, wu_ref[:, lo:hi], preferred_element_type=F32)
        p2, p1 = prev_rows(gp, lo, hi)
        gc = (wc_ref[0:1, lo:hi] * p2 + wc_ref[1:2, lo:hi] * p1 + wc_ref[2:3, lo:hi] * gp
              + bc_ref[:, lo:hi])
        acc = acc + _dot(_silu(gc) * up, wd_ref[lo:hi, :])
        gps.append(gp)
    return acc, gps


def _ffn_seq_kernel(tiles_per_seq, tail_end, final, x_ref, g2_ref, wg_ref, wu_ref, wd_ref, wc_ref, bc_ref,
                    init_ref, gf_ref, out_ref, tail_ref, carry):
    @pl.when(pl.program_id(0) % tiles_per_seq == 0)
    def _():
        carry[...] = init_ref[...]

    tm = x_ref.shape[0]
    row = lax.broadcasted_iota(jnp.int32, (tm, 1), 0)

    def prev_rows(gp, lo, hi):
        c2 = carry[SUBLANES - 2:SUBLANES - 1, lo:hi]
        c1 = carry[SUBLANES - 1:SUBLANES, lo:hi]
        p1 = jnp.where(row == 0, c1, pltpu.roll(gp, 1, axis=0))
        p2 = jnp.where(row == 0, c2, jnp.where(row == 1, c1, pltpu.roll(gp, 2, axis=0)))
        return p2, p1

    acc, gps = _ffn_body(x_ref[...], g2_ref, wg_ref, wu_ref, wd_ref, wc_ref, bc_ref, prev_rows)
    out_ref[...] = _rms(acc, gf_ref[...]) if final else acc
    tail = jnp.concatenate([gp[tail_end - SUBLANES:tail_end, :] for gp in gps], axis=1)
    tail_ref[...] = tail
    if tiles_per_seq > 1:
        carry[...] = tail


def _ffn_seq(x, g2, wg, wu, wd, wc, bc, init, gf, tm, tiles_per_seq, tail_end, final):
    rows, d_model = x.shape
    d_ff = wg.shape[1]
    assert tiles_per_seq == 1 or tail_end == tm
    return pl.pallas_call(
        functools.partial(_ffn_seq_kernel, tiles_per_seq, tail_end, final),
        out_shape=[jax.ShapeDtypeStruct(x.shape, F32),
                   jax.ShapeDtypeStruct((rows // tm, SUBLANES, d_ff), F32)],
        grid=(rows // tm,),
        in_specs=[pl.BlockSpec((tm, d_model), lambda i: (i, 0)), _const_spec(g2),
                  _const_spec(wg), _const_spec(wu), _const_spec(wd),
                  _const_spec(wc), _const_spec(bc), _const_spec(init),
                  _const_spec(gf)],
        out_specs=[pl.BlockSpec((tm, d_model), lambda i: (i, 0)),
                   pl.BlockSpec((None, SUBLANES, d_ff), lambda i: (i, 0, 0))],
        scratch_shapes=[pltpu.VMEM((SUBLANES, d_ff), F32)],
        compiler_params=_cparams(("arbitrary",)),
        name="convffn",
    )(*_operands(x, g2, wg, wu, wd, wc, bc, init, gf))


def _ffn_step_kernel(final, x_ref, g2_ref, wg_ref, wu_ref, wd_ref, wc_ref, bc_ref,
                     p2_ref, p1_ref, gf_ref, out_ref, gp_ref):
    def prev_rows(gp, lo, hi):
        return p2_ref[:, lo:hi], p1_ref[:, lo:hi]

    acc, gps = _ffn_body(x_ref[...], g2_ref, wg_ref, wu_ref, wd_ref, wc_ref, bc_ref, prev_rows)
    out_ref[...] = _rms(acc, gf_ref[...]) if final else acc
    gp_ref[...] = jnp.concatenate(gps, axis=1)


def _ffn_step(x, g2, wg, wu, wd, wc, bc, p2, p1, gf, tm, final):
    rows, d_model = x.shape
    d_ff = wg.shape[1]
    row_spec = lambda n: pl.BlockSpec((tm, n), lambda i: (i, 0))
    return pl.pallas_call(
        functools.partial(_ffn_step_kernel, final),
        out_shape=[jax.ShapeDtypeStruct(x.shape, F32), jax.ShapeDtypeStruct((rows, d_ff), F32)],
        grid=(rows // tm,),
        in_specs=[row_spec(d_model), _const_spec(g2),
                  _const_spec(wg), _const_spec(wu), _const_spec(wd),
                  _const_spec(wc), _const_spec(bc), row_spec(d_ff), row_spec(d_ff),
                  _const_spec(gf)],
        out_specs=[row_spec(d_model), row_spec(d_ff)],
        compiler_params=_cparams(("arbitrary",)),
        name="convffn_step",
    )(*_operands(x, g2, wg, wu, wd, wc, bc, p2, p1, gf))


def _row_tile(rows, cap):
    tm = min(rows, cap)
    while rows % tm:
        tm //= 2
    return tm


def kernel(x_prompt, x_sample, state_mlstm_c, state_mlstm_n, state_mlstm_m, state_conv, cache_swa_k, cache_swa_v, state_ffn_conv, meta_tokens, norm1_g, w_in, b_igate, b_fgate, w_mlstm_out, w_dconv, b_dconv, ln_conv_g, ln_conv_b, w_conv_out, attn_sinks, w_attn_out, b_merge, w_out, norm2_g, w_ffn_gate, w_ffn_up, w_ffn_conv, b_ffn_conv, w_ffn_down, final_norm_g):
    bsz, seq, d_model = x_prompt.shape
    nseq_s = x_sample.shape[0]
    depth = w_in.shape[0]
    d_ff = w_ffn_gate.shape[2]
    nblk = seq // BLK
    assert seq % BLK == 0 and x_sample.shape[1] == 1 and nseq_s % DEC_ROWS == 0
    assert cache_swa_k.shape[2] == WINDOW
    act = BF16
    nsb = SEQ_PER_STEP if bsz % SEQ_PER_STEP == 0 else 1

    x_main = x_prompt.reshape(bsz * seq, d_model)
    x_meta = jnp.concatenate([meta_tokens.astype(F32), jnp.zeros((BLK - N_META, d_model), F32)], axis=0)
    x_smp = x_sample.reshape(nseq_s, d_model)

    tm_main = _row_tile(seq, 512)
    tm_smp = _row_tile(nseq_s, 128)

    tab_meta = _rope_tables(jnp.arange(BLK))
    tab_main = _rope_tables(N_META + jnp.arange(seq))
    tab_smp = _rope_tables(jnp.full((tm_smp,), PAST_LEN))
    def regroup_heads(w, axis):
        shape = w.shape
        w = w.reshape(shape[:axis] + (KV_A, REP_A, DH_A) + shape[axis + 1:])
        return jnp.swapaxes(w, axis, axis + 1).reshape(shape)
    iota_blk = jnp.arange(BLK, dtype=jnp.int32)[:, None]
    pos0_meta = jnp.full((BLK, 1), -1, jnp.int32)
    pos0_main = jnp.where(iota_blk < N_META, iota_blk, -1)

    zeros_c = jnp.zeros((1, NH_M, DH_M, DH_M), F32)
    zeros_nm = jnp.zeros((1, SUBLANES, LANES), F32)
    zeros_kv = jnp.zeros((1, BLK, W_KV), act)
    gf = final_norm_g.reshape(1, d_model)

    col_gates = 3 * W_M
    col_o = col_gates + 2 * NH_M
    col_qa = col_o + W_M + 2 * C_CONV
    w_pack_all = jnp.concatenate(
        [w_in[:, :, :col_gates], w_in[:, :, col_o:col_qa], regroup_heads(w_in[:, :, col_qa:col_qa + W_Q], 2),
         w_in[:, :, col_qa + W_Q:], w_in[:, :, col_gates:col_o],
         jnp.zeros((depth, d_model, LANES - 2 * NH_M), F32)], axis=2).astype(BF16)
    bif_all = jnp.concatenate([b_igate, b_fgate, jnp.zeros((depth, LANES - 2 * NH_M), F32)], axis=1)[:, None]
    w_dc_all = jnp.concatenate([w_dconv, jnp.zeros((depth, HIST - CONV_W, C_CONV), F32)], axis=1)
    sinks_all = jnp.concatenate([attn_sinks, jnp.zeros((depth, LANES - H_A), F32)], axis=1)[:, None]
    w_fc_all = jnp.concatenate([w_ffn_conv, jnp.zeros((depth, SUBLANES - FFN_CONV_W, d_ff), F32)], axis=1)
    stacks = dict(
        g1=norm1_g[:, None], w_pack=w_pack_all, bm=b_merge[:, None], bif=bif_all,
        w_dc=w_dc_all, b_dc=b_dconv[:, None], ln_g=ln_conv_g[:, None], ln_b=ln_conv_b[:, None],
        sinks=sinks_all, wm=w_mlstm_out.astype(BF16), wc=w_conv_out.astype(BF16),
        wa=regroup_heads(w_attn_out, 1).astype(BF16), wo=w_out.astype(BF16), g2=norm2_g[:, None],
        wg=w_ffn_gate.astype(BF16), wu=w_ffn_up.astype(BF16), wd=w_ffn_down.astype(BF16),
        w_fc=w_fc_all, b_fc=b_ffn_conv[:, None])

    p_states = [[] for _ in range(7)]
    s_states = [[] for _ in range(7)]
    for l in range(depth):
        final = l == depth - 1
        prm = {name: _LayerParam(stack, l) for name, stack in stacks.items()}
        g1, w_pack, bm, bif = prm["g1"], prm["w_pack"], prm["bm"], prm["bif"]
        w_dc, b_dc, ln_g, ln_b, sinks = prm["w_dc"], prm["b_dc"], prm["ln_g"], prm["ln_b"], prm["sinks"]
        wm, wc, wa, wo, g2 = prm["wm"], prm["wc"], prm["wa"], prm["wo"], prm["g2"]
        wg, wu, wd, w_fc, b_fc = prm["wg"], prm["wu"], prm["wd"], prm["w_fc"], prm["b_fc"]

        conv_w = (w_dc, b_dc, ln_g, ln_b)
        qkvm, og, u, qa, ka, va, gates, ifg, cv = _in_proj(
            x_meta, g1, w_pack, bm, bif, tab_meta, BLK, act, (jnp.zeros((HIST, C_CONV), F32),) + conv_w, 1)
        hg, c_meta, n_meta, m_meta = _mlstm(qkvm[None], ifg[None], og[None], zeros_c, zeros_nm, zeros_nm,
                                            1, N_META)
        oa = _attn_blocks(qa[None], ka[None], va[None], zeros_kv, zeros_kv, pos0_meta, sinks, 1, 0)
        x1 = _merge(x_meta, hg[0], cv, oa[0], gates, wm, wc, wa, wo, BLK)
        x_meta, tail_meta = _ffn_seq(x1, g2, wg, wu, wd, w_fc, b_fc, jnp.zeros((SUBLANES, d_ff), F32), gf,
                                     BLK, 1, N_META, False)
        u_meta, ka_meta, va_meta = u, ka, va

        hist0 = jnp.concatenate([jnp.zeros((HIST - N_META, C_CONV), F32), u_meta[:N_META].astype(F32)], axis=0)
        qkvm, og, u, qa, ka, va, gates, ifg, cv = _in_proj(
            x_main, g1, w_pack, bm, bif, tab_main, tm_main, act, (hist0,) + conv_w, seq // tm_main)
        per_seq = lambda a: a.reshape(bsz, seq, a.shape[-1])
        hg, p_c, p_n, p_m = _mlstm(per_seq(qkvm), per_seq(ifg), per_seq(og), c_meta, n_meta, m_meta, nsb, BLK)
        oa = _attn_blocks(per_seq(qa), per_seq(ka), per_seq(va), ka_meta[None], va_meta.T[None], pos0_main,
                          sinks, nsb, N_META)
        flat = lambda a: a.reshape(bsz * seq, a.shape[-1])
        x1 = _merge(x_main, flat(hg), cv, flat(oa), gates, wm, wc, wa, wo, tm_main)
        x_main, tails = _ffn_seq(x1, g2, wg, wu, wd, w_fc, b_fc, tail_meta[0], gf,
                                 tm_main, seq // tm_main, tm_main, final)
        p_states[0].append(p_c)
        p_states[1].append(p_n[:, :NH_M, :])
        p_states[2].append(p_m[:, 0, :NH_M])
        p_states[3].append(per_seq(u)[:, seq - (CONV_W - 1):].astype(F32))
        last_window = lambda a: per_seq(a)[:, seq - WINDOW:].astype(F32).reshape(bsz, WINDOW, KV_A, DH_A)
        p_states[4].append(last_window(ka))
        p_states[5].append(last_window(va))
        p_states[6].append(tails.reshape(bsz, seq // tm_main, SUBLANES, d_ff)[:, -1, SUBLANES - (FFN_CONV_W - 1):])

        qkvm, og, u, qa, ka, va, gates, ifg = _in_proj(x_smp, g1, w_pack, bm, bif, tab_smp, tm_smp, F32)
        m_in = jnp.concatenate([state_mlstm_m[l], jnp.zeros((nseq_s, LANES - NH_M), F32)], axis=1)
        hg, s_c, s_n, s_m = _mlstm_step(qkvm, ifg, og, _LayerParam(state_mlstm_c, l),
                                        _LayerParam(state_mlstm_n, l), m_in)
        cv = _conv_step(_LayerParam(state_conv, l), u, w_dc, b_dc, ln_g, ln_b)
        oa = _attn_step(qa[:, None], ka[:, None], va[:, None],
                        cache_swa_k[l].reshape(nseq_s, WINDOW, W_KV), cache_swa_v[l].reshape(nseq_s, WINDOW, W_KV),
                        sinks, PAST_LEN)
        x1 = _merge(x_smp, hg, cv, oa[:, 0], gates, wm, wc, wa, wo, tm_smp)
        x_smp, gp_s = _ffn_step(x1, g2, wg, wu, wd, w_fc, b_fc, state_ffn_conv[l][:, 0], state_ffn_conv[l][:, 1],
                                gf, tm_smp, final)
        s_states[0].append(s_c)
        s_states[1].append(s_n)
        s_states[2].append(s_m[:, :NH_M])
        s_states[3].append(u[:, None])
        s_states[4].append(ka.reshape(nseq_s, 1, KV_A, DH_A))
        s_states[5].append(va.reshape(nseq_s, 1, KV_A, DH_A))
        s_states[6].append(gp_s[:, None])

    y_prompt = x_main.reshape(bsz, seq, d_model)
    y_sample = x_smp.reshape(nseq_s, 1, d_model)
    slide = lambda old, new_rows: jnp.concatenate([old[:, :, 1:], jnp.stack(new_rows, axis=0)], axis=2)
    s_out = [jnp.stack(a, axis=0) for a in s_states[:3]] + [
        slide(old, new) for old, new in zip((state_conv, cache_swa_k, cache_swa_v, state_ffn_conv), s_states[3:])]
    return (y_prompt, y_sample, *(jnp.stack(a, axis=0) for a in p_states), *s_out)
```

```python
import functools
from typing import NamedTuple

import jax
import jax.numpy as jnp
from jax import lax
from jax.experimental import pallas as pl
from jax.experimental.pallas import tpu as pltpu

F32 = jnp.float32
BF16 = jnp.bfloat16

N_META = 16
NH_M = 4
DH_M = 128
W_M = NH_M * DH_M
C_CONV = 512
CONV_W = 31
H_A = 8
KV_A = 2
DH_A = 64
REP_A = H_A // KV_A
W_Q = H_A * DH_A
W_KV = KV_A * DH_A
ROT_DIM = DH_A // 4
ROT_HALF = ROT_DIM // 2
ROPE_THETA = 500000.0
WINDOW = 128
FFN_CONV_W = 3
PAST_LEN = 16384
RMS_EPS = 1e-6
LN_EPS = 1e-5

BLK = 128
LANES = 128
MXU_TILE = 256
SUBLANES = 8
HIST = 32
DEC_ROWS = 8
SEQ_PER_STEP = 8
VMEM_LIMIT = 52 * 1024 * 1024

OFF_QKVM = 0
OFF_O = OFF_QKVM + 3 * W_M
OFF_GLU = OFF_O + W_M
OFF_QA = OFF_GLU + 2 * C_CONV
OFF_KA = OFF_QA + W_Q
OFF_VA = OFF_KA + W_KV
OFF_G = OFF_VA + W_KV


def _cparams(sem):
    return pltpu.CompilerParams(dimension_semantics=sem, vmem_limit_bytes=VMEM_LIMIT)


class _LayerParam(NamedTuple):
    stack: jax.Array
    layer: int

    @property
    def shape(self):
        return self.stack.shape[1:]


def _const_spec(a):
    nd = len(a.shape)
    if isinstance(a, _LayerParam):
        layer = a.layer
        return pl.BlockSpec((None,) + tuple(a.shape), lambda *_: (layer,) + (0,) * nd,
                            pipeline_mode=pl.Buffered(1))
    return pl.BlockSpec(a.shape, lambda *_: (0,) * nd, pipeline_mode=pl.Buffered(1))


def _state_spec(a, rows):
    tail = tuple(a.shape[1:])
    layer = a.layer
    return pl.BlockSpec((None, rows) + tail, lambda i: (layer, i) + (0,) * len(tail))


def _operands(*args):
    return tuple(a.stack if isinstance(a, _LayerParam) else a for a in args)


def _rms(x, g):
    return x * lax.rsqrt(jnp.mean(x * x, axis=-1, keepdims=True) + RMS_EPS) * g


def _log_sigmoid(x):
    return jnp.minimum(x, 0.0) - jnp.log1p(jnp.exp(-jnp.abs(x)))


def _sigmoid(x):
    return 1.0 / (1.0 + jnp.exp(-x))


def _silu(x):
    return x * _sigmoid(x)


def _dot(a, b):
    return jnp.dot(a.astype(BF16), b.astype(BF16), preferred_element_type=F32)


def _dot_nt(a, b):
    return lax.dot_general(a.astype(BF16), b.astype(BF16), (((1,), (1,)), ((), ())),
                           preferred_element_type=F32)


def _rope(x, cos, sin_lo, sin_hi):
    width = x.shape[1]
    reps = width // LANES
    if reps > 1:
        cos, sin_lo, sin_hi = (jnp.concatenate([t] * reps, axis=1) for t in (cos, sin_lo, sin_hi))
    return (x * cos + pltpu.roll(x, width - ROT_HALF, axis=1) * sin_lo
            + pltpu.roll(x, ROT_HALF, axis=1) * sin_hi)


def _rope_tables(pos):
    inv = jnp.power(ROPE_THETA, -jnp.arange(ROT_HALF, dtype=F32) / ROT_HALF)
    ang = pos.astype(F32)[:, None] * inv[None, :]
    cos, sin = jnp.cos(ang), jnp.sin(ang)
    npos = pos.shape[0]
    ones = jnp.ones((npos, DH_A - ROT_DIM), F32)
    zeros = jnp.zeros((npos, DH_A - ROT_DIM), F32)
    zh = jnp.zeros((npos, ROT_HALF), F32)
    head = lambda parts: jnp.concatenate(parts * (LANES // DH_A), axis=1)
    return (head([cos, cos, ones]), head([-sin, zh, zeros]), head([zh, sin, zeros]))


def _dwconv_block(window, w_ref, b_ref, g_ref, beta_ref):
    acc = jnp.broadcast_to(b_ref[...], (BLK, C_CONV))
    for sub in range(SUBLANES):
        shifted = window if sub == 0 else pltpu.roll(window, HIST + BLK - sub, axis=0)
        for grp in range(HIST // SUBLANES + 1):
            tap = grp * SUBLANES + sub - (HIST - (CONV_W - 1))
            if 0 <= tap < CONV_W:
                acc = acc + w_ref[tap:tap + 1, :] * shifted[grp * SUBLANES:grp * SUBLANES + BLK, :]
    return _ln_silu(acc, g_ref[...], beta_ref[...])


def _in_proj_kernel(conv_tiles, x_ref, g_ref, w_ref, bm_ref, bif_ref, cos_ref, sin_lo_ref, sin_hi_ref, *refs):
    if conv_tiles:
        hist0_ref, wdc_ref, bdc_ref, lng_ref, lnb_ref = refs[:5]
        qkvm_ref, og_ref, u_ref, qa_ref, ka_ref, va_ref, gates_ref, ifg_ref, cv_ref, buf = refs[5:]
    else:
        qkvm_ref, og_ref, u_ref, qa_ref, ka_ref, va_ref, gates_ref, ifg_ref = refs
    tm, d_model = x_ref.shape
    h = _rms(x_ref[...], g_ref[...]).astype(BF16)
    act = qkvm_ref.dtype

    def proj(off, n):
        return jnp.dot(h, w_ref[:, off:off + n], preferred_element_type=F32)

    def glu_rows(rows):
        glu = jnp.dot(h[rows, :], w_ref[:, OFF_GLU:OFF_GLU + 2 * C_CONV], preferred_element_type=F32)
        u = glu[:, :C_CONV] * _sigmoid(glu[:, C_CONV:])
        u_ref[rows, :] = u.astype(act)
        return u

    if conv_tiles:
        @pl.when(pl.program_id(0) % conv_tiles == 0)
        def _():
            buf[0:HIST, :] = hist0_ref[...]
    else:
        glu_rows(slice(0, tm))

    def conv_block(blk):
        rows = slice(blk * BLK, (blk + 1) * BLK)
        buf[HIST + blk * BLK:HIST + (blk + 1) * BLK, :] = glu_rows(rows)
        window = buf[blk * BLK:blk * BLK + HIST + BLK, :]
        cv_ref[rows, :] = _dwconv_block(window, wdc_ref, bdc_ref, lng_ref, lnb_ref).astype(act)

    def mlstm_qkv():
        qkvm_ref[:, 0:W_M] = proj(OFF_QKVM, W_M).astype(act)
        qkvm_ref[:, W_M:2 * W_M] = (proj(OFF_QKVM + W_M, W_M) * DH_M ** -0.5).astype(act)
        qkvm_ref[:, 2 * W_M:3 * W_M] = proj(OFF_QKVM + 2 * W_M, W_M).astype(act)

    def out_gate():
        og_ref[...] = _sigmoid(proj(OFF_O, W_M)).astype(act)

    def attn_qkv():
        tabs = (cos_ref[...], sin_lo_ref[...], sin_hi_ref[...])
        qa_ref[...] = _rope(proj(OFF_QA, W_Q), *tabs).astype(act)
        kv = proj(OFF_KA, 2 * W_KV)
        ka_ref[...] = _rope(kv[:, :W_KV], *tabs).astype(act)
        va_ref[...] = kv[:, W_KV:].astype(act)

    def merge_gates(part):
        cols = slice(part * d_model, (part + 1) * d_model)
        gates_ref[:, cols] = _sigmoid(proj(OFF_G + part * d_model, d_model) + bm_ref[:, cols]).astype(act)

    def log_gates():
        z = proj(OFF_G + 3 * d_model, LANES) + bif_ref[...]
        lane = lax.broadcasted_iota(jnp.int32, z.shape, 1)
        ifg_ref[...] = jnp.where(lane < NH_M, z, _log_sigmoid(z))

    steps = [mlstm_qkv, out_gate, attn_qkv] + [functools.partial(merge_gates, p) for p in range(3)] + [log_gates]
    nconv = tm // BLK if conv_tiles else 0
    at_step = {len(steps) * blk // nconv: blk for blk in range(nconv)} if nconv else {}
    assert len(at_step) == nconv
    for i, step in enumerate(steps):
        if i in at_step:
            conv_block(at_step[i])
        step()
    if conv_tiles > 1:
        buf[0:HIST, :] = buf[tm:tm + HIST, :]


def _in_proj(x, g, w, bm, bif, tabs, tm, act, conv=None, conv_tiles=0):
    rows, d_model = x.shape
    ntab = tabs[0].shape[0] // tm
    widths = (3 * W_M, W_M, C_CONV, W_Q, W_KV, W_KV, 3 * d_model, LANES) + ((C_CONV,) if conv_tiles else ())
    dtypes = (act,) * 7 + (F32,) + ((act,) if conv_tiles else ())
    conv = tuple(conv) if conv_tiles else ()
    row_spec = lambda n: pl.BlockSpec((tm, n), lambda i: (i, 0))
    tab_spec = pl.BlockSpec((tm, LANES), lambda i: (i % ntab, 0))
    return pl.pallas_call(
        functools.partial(_in_proj_kernel, conv_tiles),
        out_shape=[jax.ShapeDtypeStruct((rows, n), dt) for n, dt in zip(widths, dtypes)],
        grid=(rows // tm,),
        in_specs=[row_spec(d_model), _const_spec(g), _const_spec(w),
                  _const_spec(bm), _const_spec(bif), tab_spec, tab_spec, tab_spec]
                 + [_const_spec(a) for a in conv],
        out_specs=[row_spec(n) for n in widths],
        scratch_shapes=[pltpu.VMEM((HIST + tm, C_CONV), F32)] if conv_tiles else [],
        compiler_params=_cparams(("arbitrary",)),
        name="in_proj",
    )(*_operands(x, g, w, bm, bif, *tabs, *conv))


def _mlstm_kernel(nvalid, qkv_ref, ifg_ref, og_ref, c0_ref, n0_ref, m0_ref,
                  hg_ref, c_ref, n_ref, m_ref):
    @pl.when(pl.program_id(1) == 0)
    def _():
        c_ref[...] = jnp.broadcast_to(c0_ref[...], c_ref.shape)
        n_ref[...] = jnp.broadcast_to(n0_ref[...], n_ref.shape)
        m_ref[...] = jnp.broadcast_to(m0_ref[...], m_ref.shape)

    row = lax.broadcasted_iota(jnp.int32, (BLK, LANES), 0)
    lane = lax.broadcasted_iota(jnp.int32, (BLK, LANES), 1)
    causal = (lax.broadcasted_iota(jnp.int32, (BLK, BLK), 0)
              >= lax.broadcasted_iota(jnp.int32, (BLK, BLK), 1))

    heads = []
    for sb in range(qkv_ref.shape[0]):
        gate = jnp.where(lane < 2 * NH_M, ifg_ref[sb], 0.0)
        if nvalid < BLK:
            gate = jnp.where(row >= nvalid, jnp.where(lane < NH_M, -jnp.inf, 0.0), gate)
        csum = jnp.where(lane < NH_M, 0.0, gate)
        shift = 1
        while shift < BLK:
            csum = csum + jnp.where(row >= shift, pltpu.roll(csum, shift, axis=0), 0.0)
            shift *= 2
        b_all = pltpu.roll(csum, LANES - NH_M, axis=1)
        c_all = gate - b_all
        cmax = c_all
        shift = 1
        while shift < BLK:
            cmax = jnp.maximum(cmax, jnp.where(row >= shift, pltpu.roll(cmax, shift, axis=0), -jnp.inf))
            shift *= 2
        m_prev_all = m_ref[sb, 0:1, :]
        mm_all = jnp.maximum(cmax, m_prev_all)
        w_prev_all = jnp.exp(m_prev_all - mm_all)
        floor_all = jnp.exp(-(b_all + mm_all))
        mm_last = mm_all[BLK - 1:BLK, :]
        wts_all = jnp.exp(c_all - mm_last)
        decay_all = w_prev_all[BLK - 1:BLK, :]
        m_ref[sb] = jnp.broadcast_to(b_all[BLK - 1:BLK, :] + mm_last, (SUBLANES, LANES))
        c_rows = c_all.T

        for hd in range(NH_M):
            sl = slice(hd * DH_M, (hd + 1) * DH_M)
            q = qkv_ref[sb, :, sl]
            k = qkv_ref[sb, :, W_M + hd * DH_M:W_M + (hd + 1) * DH_M].astype(F32)
            v = qkv_ref[sb, :, 2 * W_M + hd * DH_M:2 * W_M + (hd + 1) * DH_M].astype(BF16)
            c = c_ref[sb, hd]
            n_row = n_ref[sb, hd:hd + 1, :]
            decay = decay_all[:, hd:hd + 1]

            qk = _dot_nt(q, k)
            qc = _dot(q, c)
            qn = jnp.sum(q.astype(F32) * n_row, axis=-1, keepdims=True)
            dmat = jnp.exp(jnp.where(causal, c_rows[hd:hd + 1, :] - mm_all[:, hd:hd + 1], -jnp.inf))
            kw = k * wts_all[:, hd:hd + 1]
            c_ref[sb, hd] = decay * c + lax.dot_general(
                kw.astype(BF16), v, (((0,), (0,)), ((), ())), preferred_element_type=F32)
            n_ref[sb, hd:hd + 1, :] = decay * n_row + jnp.sum(kw, axis=0, keepdims=True)
            heads.append((sb, sl, v, qk * dmat, w_prev_all[:, hd:hd + 1], qc, qn, floor_all[:, hd:hd + 1]))

    for sb, sl, v, s, w_prev, qc, qn, floor in heads:
        num = _dot(s, v) + w_prev * qc
        den = jnp.sum(s, axis=-1, keepdims=True) + w_prev * qn
        h = num / jnp.maximum(jnp.abs(den), floor)
        hg_ref[sb, :, sl] = (og_ref[sb, :, sl].astype(F32) * h).astype(hg_ref.dtype)


def _mlstm(qkvm, ifg, og, c0, n0, m0, nsb, nvalid):
    nseq, seq, _ = qkvm.shape
    blk_spec = lambda n: pl.BlockSpec((nsb, BLK, n), lambda b, j: (b, j, 0))
    return pl.pallas_call(
        functools.partial(_mlstm_kernel, nvalid),
        out_shape=[jax.ShapeDtypeStruct((nseq, seq, W_M), og.dtype),
                   jax.ShapeDtypeStruct((nseq, NH_M, DH_M, DH_M), F32),
                   jax.ShapeDtypeStruct((nseq, SUBLANES, LANES), F32),
                   jax.ShapeDtypeStruct((nseq, SUBLANES, LANES), F32)],
        grid=(nseq // nsb, seq // BLK),
        in_specs=[blk_spec(3 * W_M), blk_spec(LANES), blk_spec(W_M),
                  _const_spec(c0), _const_spec(n0), _const_spec(m0)],
        out_specs=[blk_spec(W_M),
                   pl.BlockSpec((nsb, NH_M, DH_M, DH_M), lambda b, j: (b, 0, 0, 0)),
                   pl.BlockSpec((nsb, SUBLANES, LANES), lambda b, j: (b, 0, 0)),
                   pl.BlockSpec((nsb, SUBLANES, LANES), lambda b, j: (b, 0, 0))],
        compiler_params=_cparams(("arbitrary", "arbitrary")),
        name="mlstm",
    )(qkvm, ifg, og, c0, n0, m0)


def _mlstm_step_kernel(qkv_ref, ifg_ref, og_ref, c_ref, n_ref, m_ref,
                       hg_ref, c_out, n_out, m_out):
    gate = ifg_ref[...]
    ig = gate
    lf = pltpu.roll(gate, LANES - NH_M, axis=1)
    log_prev = lf + m_ref[...]
    m_t = jnp.maximum(log_prev, ig)
    dmat = jnp.exp(ig - m_t)
    w_prev = jnp.exp(log_prev - m_t)
    floor = jnp.exp(-m_t)
    m_out[...] = m_t

    qs, kds, vs, ss = [], [], [], []
    for hd in range(NH_M):
        q = qkv_ref[:, hd * DH_M:(hd + 1) * DH_M]
        k = qkv_ref[:, W_M + hd * DH_M:W_M + (hd + 1) * DH_M]
        v = qkv_ref[:, 2 * W_M + hd * DH_M:2 * W_M + (hd + 1) * DH_M]
        d_h = dmat[:, hd:hd + 1]
        qs.append(q)
        kds.append(k * d_h)
        vs.append(v)
        ss.append(jnp.sum(q * k, axis=-1, keepdims=True) * d_h)
    stacked = jnp.concatenate(qs + kds + [jnp.zeros((BLK - 2 * NH_M * DEC_ROWS, DH_M), F32)], axis=0)
    cols = stacked.T

    for r in range(DEC_ROWS):
        for hd in range(NH_M):
            q_col = cols[:, hd * DEC_ROWS + r:hd * DEC_ROWS + r + 1]
            k_col = cols[:, (NH_M + hd) * DEC_ROWS + r:(NH_M + hd) * DEC_ROWS + r + 1]
            c = c_ref[r, hd]
            n_row = n_ref[r, hd:hd + 1, :]
            v_row = vs[hd][r:r + 1, :]
            wp = w_prev[r:r + 1, hd:hd + 1]
            s = ss[hd][r:r + 1, :]
            qc = jnp.sum(q_col * c, axis=0, keepdims=True)
            qn = jnp.sum(qs[hd][r:r + 1, :] * n_row, axis=-1, keepdims=True)
            num = s * v_row + wp * qc
            den = s + wp * qn
            h = num / jnp.maximum(jnp.abs(den), floor[r:r + 1, hd:hd + 1])
            hg_ref[r:r + 1, hd * DH_M:(hd + 1) * DH_M] = og_ref[r:r + 1, hd * DH_M:(hd + 1) * DH_M] * h
            c_out[r, hd] = wp * c + k_col * v_row
            n_out[r, hd:hd + 1, :] = wp * n_row + kds[hd][r:r + 1, :]


def _mlstm_step(qkvm, ifg, og, c, n, m):
    nseq = qkvm.shape[0]
    row_spec = lambda w: pl.BlockSpec((DEC_ROWS, w), lambda i: (i, 0))
    c_spec = pl.BlockSpec((DEC_ROWS, NH_M, DH_M, DH_M), lambda i: (i, 0, 0, 0))
    n_spec = pl.BlockSpec((DEC_ROWS, NH_M, DH_M), lambda i: (i, 0, 0))
    return pl.pallas_call(
        _mlstm_step_kernel,
        out_shape=[jax.ShapeDtypeStruct((nseq, W_M), F32),
                   jax.ShapeDtypeStruct(c.shape, F32),
                   jax.ShapeDtypeStruct(n.shape, F32),
                   jax.ShapeDtypeStruct((nseq, LANES), F32)],
        grid=(nseq // DEC_ROWS,),
        in_specs=[row_spec(3 * W_M), row_spec(LANES), row_spec(W_M),
                  _state_spec(c, DEC_ROWS), _state_spec(n, DEC_ROWS), row_spec(LANES)],
        out_specs=[row_spec(W_M), c_spec, n_spec, row_spec(LANES)],
        compiler_params=_cparams(("arbitrary",)),
        name="mlstm_step",
    )(*_operands(qkvm, ifg, og, c, n, m))


def _ln_silu(y, g, b):
    yc = y - jnp.mean(y, axis=-1, keepdims=True)
    var = jnp.mean(yc * yc, axis=-1, keepdims=True)
    return _silu(yc * lax.rsqrt(var + LN_EPS) * g + b)


def _conv_step_kernel(st_ref, u_ref, w_ref, b_ref, g_ref, beta_ref, out_ref):
    w_hist = w_ref[0:CONV_W - 1, :]
    rows = []
    for r in range(DEC_ROWS):
        rows.append(jnp.sum(st_ref[r] * w_hist, axis=0, keepdims=True))
    acc = jnp.concatenate(rows, axis=0) + w_ref[CONV_W - 1:CONV_W, :] * u_ref[...] + b_ref[...]
    out_ref[...] = _ln_silu(acc, g_ref[...], beta_ref[...])


def _conv_step(state, u, w, b, g, beta):
    nseq = u.shape[0]
    return pl.pallas_call(
        _conv_step_kernel,
        out_shape=jax.ShapeDtypeStruct(u.shape, F32),
        grid=(nseq // DEC_ROWS,),
        in_specs=[_state_spec(state, DEC_ROWS),
                  pl.BlockSpec((DEC_ROWS, C_CONV), lambda i: (i, 0)),
                  _const_spec(w), _const_spec(b), _const_spec(g), _const_spec(beta)],
        out_specs=pl.BlockSpec((DEC_ROWS, C_CONV), lambda i: (i, 0)),
        compiler_params=_cparams(("arbitrary",)),
        name="dwconv_step",
    )(*_operands(state, u, w, b, g, beta))


def _stack_heads(q, low_half):
    heads = []
    for r in range(REP_A):
        slab = q[:, r * LANES:(r + 1) * LANES]
        heads.append(jnp.where(low_half, slab, 0.0))
        heads.append(jnp.where(low_half, 0.0, slab))
    return jnp.concatenate(heads, axis=0)


def _attn_block_kernel(base, carry, q_ref, k_ref, v_ref, k0_ref, v0t_ref, pos0_ref, sink_ref,
                       o_ref, kprev, vtprev):
    j = pl.program_id(1)
    nsb = q_ref.shape[0]

    if carry:
        @pl.when(j == 0)
        def _():
            kprev[...] = jnp.broadcast_to(k0_ref[...], kprev.shape).astype(kprev.dtype)
            vtprev[...] = jnp.broadcast_to(v0t_ref[...], vtprev.shape).astype(vtprev.dtype)

    kpos_prev = jnp.where(j == 0, pos0_ref[...],
                          base + BLK * j - BLK + lax.broadcasted_iota(jnp.int32, (BLK, 1), 0))
    kpos = jnp.concatenate([kpos_prev, base + BLK * j + lax.broadcasted_iota(jnp.int32, (BLK, 1), 0)], axis=0)
    qpos = base + BLK * j + lax.broadcasted_iota(jnp.int32, (1, BLK), 1)
    bias = jnp.where((kpos >= 0) & (kpos <= qpos) & (qpos - kpos <= WINDOW), 0.0, -jnp.inf)
    low_half = lax.broadcasted_iota(jnp.int32, (BLK, LANES), 1) < DH_A
    scale = DH_A ** -0.5

    scores, vt_cats = [], []
    for sb in range(nsb):
        k = k_ref[sb]
        vt = v_ref[sb].astype(F32).T.astype(BF16)
        k_prev, vt_prev = (kprev[sb], vtprev[sb]) if carry else (k0_ref[0], v0t_ref[0])
        k_cat = jnp.concatenate([k_prev.astype(BF16), k.astype(BF16)], axis=0)
        vt_cats.append(jnp.concatenate([vt_prev.astype(BF16), vt], axis=1))
        scores.append(_dot_nt(k_cat, _stack_heads(q_ref[sb] * scale, low_half)))
        if carry:
            kprev[sb] = k
            vtprev[sb] = vt

    probs, invs = [], []
    for st in scores:
        p_heads, inv = [], []
        for hd in range(H_A):
            r, g = divmod(hd, KV_A)
            sink = sink_ref[0:1, g * REP_A + r:g * REP_A + r + 1]
            s = st[:, hd * BLK:(hd + 1) * BLK] + bias
            mx = jnp.maximum(jnp.max(s, axis=0, keepdims=True), sink)
            p = jnp.exp(s - mx)
            inv.append(1.0 / (jnp.sum(p, axis=0, keepdims=True) + jnp.exp(sink - mx)))
            p_heads.append(p.astype(BF16))
        probs.append(jnp.concatenate(p_heads, axis=1))
        invs.append(inv)

    for sb in range(nsb):
        ot = jnp.dot(vt_cats[sb], probs[sb], preferred_element_type=F32)
        out_t = jnp.concatenate(
            [ot[(hd % KV_A) * DH_A:(hd % KV_A + 1) * DH_A, hd * BLK:(hd + 1) * BLK] * invs[sb][hd]
             for hd in range(H_A)], axis=0)
        o_ref[sb] = out_t.T.astype(o_ref.dtype)


def _attn_blocks(q, k, v, k0, v0t, pos0, sinks, nsb, base):
    nseq, seq, _ = q.shape
    q_spec = pl.BlockSpec((nsb, BLK, W_Q), lambda s, j: (s, j, 0))
    k_spec = pl.BlockSpec((nsb, BLK, W_KV), lambda s, j: (s, j, 0))
    return pl.pallas_call(
        functools.partial(_attn_block_kernel, base, seq > BLK),
        out_shape=jax.ShapeDtypeStruct(q.shape, q.dtype),
        grid=(nseq // nsb, seq // BLK),
        in_specs=[q_spec, k_spec, k_spec, _const_spec(k0), _const_spec(v0t),
                  _const_spec(pos0), _const_spec(sinks)],
        out_specs=q_spec,
        scratch_shapes=[pltpu.VMEM((nsb, BLK, W_KV), k.dtype), pltpu.VMEM((nsb, W_KV, BLK), BF16)],
        compiler_params=_cparams(("arbitrary", "arbitrary")),
        name="swa",
    )(*_operands(q, k, v, k0, v0t, pos0, sinks))


def _attn_step_kernel(base, q_ref, k_ref, v_ref, kc_ref, vc_ref, sink_ref, o_ref):
    low_half = lax.broadcasted_iota(jnp.int32, (1, LANES), 1) < DH_A
    sinks = jnp.concatenate([sink_ref[0:1, g * REP_A + r:g * REP_A + r + 1]
                             for r in range(REP_A) for g in range(KV_A)], axis=0)
    kpos = base - WINDOW + lax.broadcasted_iota(jnp.int32, (1, 2 * BLK), 1)
    bias = jnp.where((kpos >= 0) & (kpos <= base) & (base - kpos <= WINDOW), 0.0, -jnp.inf)
    pad = jnp.zeros((BLK - 1, W_KV), F32)
    scale = DH_A ** -0.5
    nsb = q_ref.shape[0]
    scores = []
    for sb in range(nsb):
        k_cat = jnp.concatenate([kc_ref[sb], k_ref[sb], pad], axis=0)
        scores.append(_dot_nt(_stack_heads(q_ref[sb] * scale, low_half), k_cat) + bias)
    probs = []
    for s in scores:
        mx = jnp.maximum(jnp.max(s, axis=-1, keepdims=True), sinks)
        p = jnp.exp(s - mx)
        den = jnp.sum(p, axis=-1, keepdims=True) + jnp.exp(sinks - mx)
        probs.append(p * (1.0 / den))
    for sb in range(nsb):
        v_cat = jnp.concatenate([vc_ref[sb], v_ref[sb], pad], axis=0)
        o = _dot(probs[sb], v_cat)
        for r in range(REP_A):
            o_ref[sb, :, r * LANES:(r + 1) * LANES] = jnp.where(low_half, o[2 * r:2 * r + 1], o[2 * r + 1:2 * r + 2])


def _attn_step(q, k, v, k_cache, v_cache, sinks, base):
    nseq = q.shape[0]
    spec = lambda rows, n: pl.BlockSpec((DEC_ROWS, rows, n), lambda i: (i, 0, 0))
    return pl.pallas_call(
        functools.partial(_attn_step_kernel, base),
        out_shape=jax.ShapeDtypeStruct(q.shape, F32),
        grid=(nseq // DEC_ROWS,),
        in_specs=[spec(1, W_Q), spec(1, W_KV), spec(1, W_KV), spec(WINDOW, W_KV), spec(WINDOW, W_KV),
                  _const_spec(sinks)],
        out_specs=spec(1, W_Q),
        compiler_params=_cparams(("arbitrary",)),
        name="swa_step",
    )(*_operands(q, k, v, k_cache, v_cache, sinks))


def _merge_kernel(x_ref, hg_ref, cv_ref, oa_ref, gates_ref, wm_ref, wc_ref, wa_ref, wo_ref, out_ref):
    d_model = x_ref.shape[1]
    mix = (gates_ref[:, 0:d_model] * _dot(hg_ref[...], wm_ref[...])
           + gates_ref[:, d_model:2 * d_model] * _dot(cv_ref[...], wc_ref[...])
           + gates_ref[:, 2 * d_model:3 * d_model] * _dot(oa_ref[...], wa_ref[...]))
    out_ref[...] = x_ref[...] + _dot(mix, wo_ref[...])


def _merge(x, hg, cv, oa, gates, wm, wc, wa, wo, tm):
    rows, d_model = x.shape
    row_spec = lambda n: pl.BlockSpec((tm, n), lambda i: (i, 0))
    return pl.pallas_call(
        _merge_kernel,
        out_shape=jax.ShapeDtypeStruct(x.shape, F32),
        grid=(rows // tm,),
        in_specs=[row_spec(d_model), row_spec(W_M), row_spec(C_CONV), row_spec(W_Q), row_spec(3 * d_model),
                  _const_spec(wm), _const_spec(wc), _const_spec(wa), _const_spec(wo)],
        out_specs=row_spec(d_model),
        compiler_params=_cparams(("arbitrary",)),
        name="merge",
    )(*_operands(x, hg, cv, oa, gates, wm, wc, wa, wo))


def _ffn_body(x, g2_ref, wg_ref, wu_ref, wd_ref, wc_ref, bc_ref, prev_rows):
    d_ff = wg_ref.shape[1]
    tiles = d_ff // MXU_TILE
    split = (tiles + 1) // 2 * MXU_TILE if tiles >= 2 else d_ff
    bounds = [(0, split), (split, d_ff)] if split < d_ff else [(0, d_ff)]
    h2 = _rms(x, g2_ref[...]).astype(BF16)
    gps, acts = [], []
    for lo, hi in bounds:
        gp = jnp.dot(h2, wg_ref[:, lo:hi], preferred_element_type=F32)
        up = jnp.dot(h2, wu_ref[:, lo:hi], preferred_element_type=F32)
        p2, p1 = prev_rows(gp, lo, hi)
        gc = (wc_ref[0:1, lo:hi] * p2 + wc_ref[1:2, lo:hi] * p1 + wc_ref[2:3, lo:hi] * gp
              + bc_ref[:, lo:hi])
        acts.append((_silu(gc) * up).astype(BF16))
        gps.append(gp)
    acc = x
    for (lo, hi), a in zip(bounds, acts):
        acc = acc + jnp.dot(a, wd_ref[lo:hi, :], preferred_element_type=F32)
    return acc, gps


def _ffn_seq_kernel(tiles_per_seq, tail_end, final, x_ref, g2_ref, wg_ref, wu_ref, wd_ref, wc_ref, bc_ref,
                    init_ref, gf_ref, out_ref, tail_ref, carry):
    @pl.when(pl.program_id(0) % tiles_per_seq == 0)
    def _():
        carry[...] = init_ref[...]

    tm = x_ref.shape[0]
    row = lax.broadcasted_iota(jnp.int32, (tm, 1), 0)

    def prev_rows(gp, lo, hi):
        c2 = carry[SUBLANES - 2:SUBLANES - 1, lo:hi]
        c1 = carry[SUBLANES - 1:SUBLANES, lo:hi]
        p1 = jnp.where(row == 0, c1, pltpu.roll(gp, 1, axis=0))
        p2 = jnp.where(row == 0, c2, jnp.where(row == 1, c1, pltpu.roll(gp, 2, axis=0)))
        return p2, p1

    acc, gps = _ffn_body(x_ref[...], g2_ref, wg_ref, wu_ref, wd_ref, wc_ref, bc_ref, prev_rows)
    out_ref[...] = _rms(acc, gf_ref[...]) if final else acc
    tail = jnp.concatenate([gp[tail_end - SUBLANES:tail_end, :] for gp in gps], axis=1)
    tail_ref[...] = tail
    if tiles_per_seq > 1:
        carry[...] = tail


def _ffn_seq(x, g2, wg, wu, wd, wc, bc, init, gf, tm, tiles_per_seq, tail_end, final):
    rows, d_model = x.shape
    d_ff = wg.shape[1]
    assert tiles_per_seq == 1 or tail_end == tm
    return pl.pallas_call(
        functools.partial(_ffn_seq_kernel, tiles_per_seq, tail_end, final),
        out_shape=[jax.ShapeDtypeStruct(x.shape, F32),
                   jax.ShapeDtypeStruct((rows // tm, SUBLANES, d_ff), F32)],
        grid=(rows // tm,),
        in_specs=[pl.BlockSpec((tm, d_model), lambda i: (i, 0)), _const_spec(g2),
                  _const_spec(wg), _const_spec(wu), _const_spec(wd),
                  _const_spec(wc), _const_spec(bc), _const_spec(init),
                  _const_spec(gf)],
        out_specs=[pl.BlockSpec((tm, d_model), lambda i: (i, 0)),
                   pl.BlockSpec((None, SUBLANES, d_ff), lambda i: (i, 0, 0))],
        scratch_shapes=[pltpu.VMEM((SUBLANES, d_ff), F32)],
        compiler_params=_cparams(("arbitrary",)),
        name="convffn",
    )(*_operands(x, g2, wg, wu, wd, wc, bc, init, gf))


def _ffn_step_kernel(final, x_ref, g2_ref, wg_ref, wu_ref, wd_ref, wc_ref, bc_ref,
                     p2_ref, p1_ref, gf_ref, out_ref, gp_ref):
    def prev_rows(gp, lo, hi):
        return p2_ref[:, lo:hi], p1_ref[:, lo:hi]

    acc, gps = _ffn_body(x_ref[...], g2_ref, wg_ref, wu_ref, wd_ref, wc_ref, bc_ref, prev_rows)
    out_ref[...] = _rms(acc, gf_ref[...]) if final else acc
    gp_ref[...] = jnp.concatenate(gps, axis=1)


def _ffn_step(x, g2, wg, wu, wd, wc, bc, p2, p1, gf, tm, final):
    rows, d_model = x.shape
    d_ff = wg.shape[1]
    row_spec = lambda n: pl.BlockSpec((tm, n), lambda i: (i, 0))
    return pl.pallas_call(
        functools.partial(_ffn_step_kernel, final),
        out_shape=[jax.ShapeDtypeStruct(x.shape, F32), jax.ShapeDtypeStruct((rows, d_ff), F32)],
        grid=(rows // tm,),
        in_specs=[row_spec(d_model), _const_spec(g2),
                  _const_spec(wg), _const_spec(wu), _const_spec(wd),
                  _const_spec(wc), _const_spec(bc), row_spec(d_ff), row_spec(d_ff),
                  _const_spec(gf)],
        out_specs=[row_spec(d_model), row_spec(d_ff)],
        compiler_params=_cparams(("arbitrary",)),
        name="convffn_step",
    )(*_operands(x, g2, wg, wu, wd, wc, bc, p2, p1, gf))


def _row_tile(rows, cap):
    tm = min(rows, cap)
    while rows % tm:
        tm //= 2
    return tm


def kernel(x_prompt, x_sample, state_mlstm_c, state_mlstm_n, state_mlstm_m, state_conv, cache_swa_k, cache_swa_v, state_ffn_conv, meta_tokens, norm1_g, w_in, b_igate, b_fgate, w_mlstm_out, w_dconv, b_dconv, ln_conv_g, ln_conv_b, w_conv_out, attn_sinks, w_attn_out, b_merge, w_out, norm2_g, w_ffn_gate, w_ffn_up, w_ffn_conv, b_ffn_conv, w_ffn_down, final_norm_g):
    bsz, seq, d_model = x_prompt.shape
    nseq_s = x_sample.shape[0]
    depth = w_in.shape[0]
    d_ff = w_ffn_gate.shape[2]
    nblk = seq // BLK
    assert seq % BLK == 0 and x_sample.shape[1] == 1 and nseq_s % DEC_ROWS == 0
    assert cache_swa_k.shape[2] == WINDOW
    act = BF16
    nsb = SEQ_PER_STEP if bsz % SEQ_PER_STEP == 0 else 1

    x_main = x_prompt.reshape(bsz * seq, d_model)
    x_meta = jnp.concatenate([meta_tokens.astype(F32), jnp.zeros((BLK - N_META, d_model), F32)], axis=0)
    x_smp = x_sample.reshape(nseq_s, d_model)

    tm_main = _row_tile(seq, 512)
    tm_smp = _row_tile(nseq_s, 128)

    tab_meta = _rope_tables(jnp.arange(BLK))
    tab_main = _rope_tables(N_META + jnp.arange(seq))
    tab_smp = _rope_tables(jnp.full((tm_smp,), PAST_LEN))
    def regroup_heads(w, axis):
        shape = w.shape
        w = w.reshape(shape[:axis] + (KV_A, REP_A, DH_A) + shape[axis + 1:])
        return jnp.swapaxes(w, axis, axis + 1).reshape(shape)
    iota_blk = jnp.arange(BLK, dtype=jnp.int32)[:, None]
    pos0_meta = jnp.full((BLK, 1), -1, jnp.int32)
    pos0_main = jnp.where(iota_blk < N_META, iota_blk, -1)

    zeros_c = jnp.zeros((1, NH_M, DH_M, DH_M), F32)
    zeros_nm = jnp.zeros((1, SUBLANES, LANES), F32)
    zeros_kv = jnp.zeros((1, BLK, W_KV), act)
    gf = final_norm_g.reshape(1, d_model)

    col_gates = 3 * W_M
    col_o = col_gates + 2 * NH_M
    col_qa = col_o + W_M + 2 * C_CONV
    w_pack_all = jnp.concatenate(
        [w_in[:, :, :col_gates], w_in[:, :, col_o:col_qa], regroup_heads(w_in[:, :, col_qa:col_qa + W_Q], 2),
         w_in[:, :, col_qa + W_Q:], w_in[:, :, col_gates:col_o],
         jnp.zeros((depth, d_model, LANES - 2 * NH_M), F32)], axis=2).astype(BF16)
    bif_all = jnp.concatenate([b_igate, b_fgate, jnp.zeros((depth, LANES - 2 * NH_M), F32)], axis=1)[:, None]
    w_dc_all = jnp.concatenate([w_dconv, jnp.zeros((depth, HIST - CONV_W, C_CONV), F32)], axis=1)
    sinks_all = jnp.concatenate([attn_sinks, jnp.zeros((depth, LANES - H_A), F32)], axis=1)[:, None]
    w_fc_all = jnp.concatenate([w_ffn_conv, jnp.zeros((depth, SUBLANES - FFN_CONV_W, d_ff), F32)], axis=1)
    stacks = dict(
        g1=norm1_g[:, None], w_pack=w_pack_all, bm=b_merge[:, None], bif=bif_all,
        w_dc=w_dc_all, b_dc=b_dconv[:, None], ln_g=ln_conv_g[:, None], ln_b=ln_conv_b[:, None],
        sinks=sinks_all, wm=w_mlstm_out.astype(BF16), wc=w_conv_out.astype(BF16),
        wa=regroup_heads(w_attn_out, 1).astype(BF16), wo=w_out.astype(BF16), g2=norm2_g[:, None],
        wg=w_ffn_gate.astype(BF16), wu=w_ffn_up.astype(BF16), wd=w_ffn_down.astype(BF16),
        w_fc=w_fc_all, b_fc=b_ffn_conv[:, None])

    p_states = [[] for _ in range(7)]
    s_states = [[] for _ in range(7)]
    for l in range(depth):
        final = l == depth - 1
        prm = {name: _LayerParam(stack, l) for name, stack in stacks.items()}
        g1, w_pack, bm, bif = prm["g1"], prm["w_pack"], prm["bm"], prm["bif"]
        w_dc, b_dc, ln_g, ln_b, sinks = prm["w_dc"], prm["b_dc"], prm["ln_g"], prm["ln_b"], prm["sinks"]
        wm, wc, wa, wo, g2 = prm["wm"], prm["wc"], prm["wa"], prm["wo"], prm["g2"]
        wg, wu, wd, w_fc, b_fc = prm["wg"], prm["wu"], prm["wd"], prm["w_fc"], prm["b_fc"]

        conv_w = (w_dc, b_dc, ln_g, ln_b)
        qkvm, og, u, qa, ka, va, gates, ifg, cv = _in_proj(
            x_meta, g1, w_pack, bm, bif, tab_meta, BLK, act, (jnp.zeros((HIST, C_CONV), F32),) + conv_w, 1)
        hg, c_meta, n_meta, m_meta = _mlstm(qkvm[None], ifg[None], og[None], zeros_c, zeros_nm, zeros_nm,
                                            1, N_META)
        oa = _attn_blocks(qa[None], ka[None], va[None], zeros_kv, zeros_kv, pos0_meta, sinks, 1, 0)
        x1 = _merge(x_meta, hg[0], cv, oa[0], gates, wm, wc, wa, wo, BLK)
        x_meta, tail_meta = _ffn_seq(x1, g2, wg, wu, wd, w_fc, b_fc, jnp.zeros((SUBLANES, d_ff), F32), gf,
                                     BLK, 1, N_META, False)
        u_meta, ka_meta, va_meta = u, ka, va

        hist0 = jnp.concatenate([jnp.zeros((HIST - N_META, C_CONV), F32), u_meta[:N_META].astype(F32)], axis=0)
        qkvm, og, u, qa, ka, va, gates, ifg, cv = _in_proj(
            x_main, g1, w_pack, bm, bif, tab_main, tm_main, act, (hist0,) + conv_w, seq // tm_main)
        per_seq = lambda a: a.reshape(bsz, seq, a.shape[-1])
        hg, p_c, p_n, p_m = _mlstm(per_seq(qkvm), per_seq(ifg), per_seq(og), c_meta, n_meta, m_meta, nsb, BLK)
        oa = _attn_blocks(per_seq(qa), per_seq(ka), per_seq(va), ka_meta[None], va_meta.T[None], pos0_main,
                          sinks, nsb, N_META)
        flat = lambda a: a.reshape(bsz * seq, a.shape[-1])
        x1 = _merge(x_main, flat(hg), cv, flat(oa), gates, wm, wc, wa, wo, tm_main)
        x_main, tails = _ffn_seq(x1, g2, wg, wu, wd, w_fc, b_fc, tail_meta[0], gf,
                                 tm_main, seq // tm_main, tm_main, final)
        p_states[0].append(p_c)
        p_states[1].append(p_n[:, :NH_M, :])
        p_states[2].append(p_m[:, 0, :NH_M])
        p_states[3].append(per_seq(u)[:, seq - (CONV_W - 1):].astype(F32))
        last_window = lambda a: per_seq(a)[:, seq - WINDOW:].astype(F32).reshape(bsz, WINDOW, KV_A, DH_A)
        p_states[4].append(last_window(ka))
        p_states[5].append(last_window(va))
        p_states[6].append(tails.reshape(bsz, seq // tm_main, SUBLANES, d_ff)[:, -1, SUBLANES - (FFN_CONV_W - 1):])

        qkvm, og, u, qa, ka, va, gates, ifg = _in_proj(x_smp, g1, w_pack, bm, bif, tab_smp, tm_smp, F32)
        m_in = jnp.concatenate([state_mlstm_m[l], jnp.zeros((nseq_s, LANES - NH_M), F32)], axis=1)
        hg, s_c, s_n, s_m = _mlstm_step(qkvm, ifg, og, _LayerParam(state_mlstm_c, l),
                                        _LayerParam(state_mlstm_n, l), m_in)
        cv = _conv_step(_LayerParam(state_conv, l), u, w_dc, b_dc, ln_g, ln_b)
        oa = _attn_step(qa[:, None], ka[:, None], va[:, None],
                        cache_swa_k[l].reshape(nseq_s, WINDOW, W_KV), cache_swa_v[l].reshape(nseq_s, WINDOW, W_KV),
                        sinks, PAST_LEN)
        x1 = _merge(x_smp, hg, cv, oa[:, 0], gates, wm, wc, wa, wo, tm_smp)
        x_smp, gp_s = _ffn_step(x1, g2, wg, wu, wd, w_fc, b_fc, state_ffn_conv[l][:, 0], state_ffn_conv[l][:, 1],
                                gf, tm_smp, final)
        s_states[0].append(s_c)
        s_states[1].append(s_n)
        s_states[2].append(s_m[:, :NH_M])
        s_states[3].append(u[:, None])
        s_states[4].append(ka.reshape(nseq_s, 1, KV_A, DH_A))
        s_states[5].append(va.reshape(nseq_s, 1, KV_A, DH_A))
        s_states[6].append(gp_s[:, None])

    y_prompt = x_main.reshape(bsz, seq, d_model)
    y_sample = x_smp.reshape(nseq_s, 1, d_model)
    slide = lambda old, new_rows: jnp.concatenate([old[:, :, 1:], jnp.stack(new_rows, axis=0)], axis=2)
    s_out = [jnp.stack(a, axis=0) for a in s_states[:3]] + [
        slide(old, new) for old, new in zip((state_conv, cache_swa_k, cache_swa_v, state_ffn_conv), s_states[3:])]
    return (y_prompt, y_sample, *(jnp.stack(a, axis=0) for a in p_states), *s_out)
```

```python
import functools
from typing import NamedTuple

import jax
import jax.numpy as jnp
from jax import lax
from jax.experimental import pallas as pl
from jax.experimental.pallas import tpu as pltpu

F32 = jnp.float32
BF16 = jnp.bfloat16

N_META = 16
NH_M = 4
DH_M = 128
W_M = NH_M * DH_M
C_CONV = 512
CONV_W = 31
H_A = 8
KV_A = 2
DH_A = 64
REP_A = H_A // KV_A
W_Q = H_A * DH_A
W_KV = KV_A * DH_A
ROT_DIM = DH_A // 4
ROT_HALF = ROT_DIM // 2
ROPE_THETA = 500000.0
WINDOW = 128
FFN_CONV_W = 3
PAST_LEN = 16384
RMS_EPS = 1e-6
LN_EPS = 1e-5

BLK = 128
LANES = 128
MXU_TILE = 256
SUBLANES = 8
HIST = 32
DEC_ROWS = 8
SEQ_PER_STEP = 8
VMEM_LIMIT = 52 * 1024 * 1024

OFF_QKVM = 0
OFF_O = OFF_QKVM + 3 * W_M
OFF_GLU = OFF_O + W_M
OFF_QA = OFF_GLU + 2 * C_CONV
OFF_KA = OFF_QA + W_Q
OFF_VA = OFF_KA + W_KV
OFF_G = OFF_VA + W_KV


def _cparams(sem):
    return pltpu.CompilerParams(dimension_semantics=sem, vmem_limit_bytes=VMEM_LIMIT)


class _LayerParam(NamedTuple):
    stack: jax.Array
    layer: int

    @property
    def shape(self):
        return self.stack.shape[1:]


def _const_spec(a):
    nd = len(a.shape)
    if isinstance(a, _LayerParam):
        layer = a.layer
        return pl.BlockSpec((None,) + tuple(a.shape), lambda *_: (layer,) + (0,) * nd,
                            pipeline_mode=pl.Buffered(1))
    return pl.BlockSpec(a.shape, lambda *_: (0,) * nd, pipeline_mode=pl.Buffered(1))


def _state_spec(a, rows):
    tail = tuple(a.shape[1:])
    layer = a.layer
    return pl.BlockSpec((None, rows) + tail, lambda i: (layer, i) + (0,) * len(tail))


def _operands(*args):
    return tuple(a.stack if isinstance(a, _LayerParam) else a for a in args)


def _rms(x, g):
    return x * lax.rsqrt(jnp.mean(x * x, axis=-1, keepdims=True) + RMS_EPS) * g


def _log_sigmoid(x):
    return jnp.minimum(x, 0.0) - jnp.log1p(jnp.exp(-jnp.abs(x)))


def _sigmoid(x):
    return 1.0 / (1.0 + jnp.exp(-x))


def _silu(x):
    return x * _sigmoid(x)


def _dot(a, b):
    return jnp.dot(a.astype(BF16), b.astype(BF16), preferred_element_type=F32)


def _dot_nt(a, b):
    return lax.dot_general(a.astype(BF16), b.astype(BF16), (((1,), (1,)), ((), ())),
                           preferred_element_type=F32)


def _rope(x, cos, sin_lo, sin_hi):
    width = x.shape[1]
    reps = width // LANES
    if reps > 1:
        cos, sin_lo, sin_hi = (jnp.concatenate([t] * reps, axis=1) for t in (cos, sin_lo, sin_hi))
    return (x * cos + pltpu.roll(x, width - ROT_HALF, axis=1) * sin_lo
            + pltpu.roll(x, ROT_HALF, axis=1) * sin_hi)


def _rope_tables(pos):
    inv = jnp.power(ROPE_THETA, -jnp.arange(ROT_HALF, dtype=F32) / ROT_HALF)
    ang = pos.astype(F32)[:, None] * inv[None, :]
    cos, sin = jnp.cos(ang), jnp.sin(ang)
    npos = pos.shape[0]
    ones = jnp.ones((npos, DH_A - ROT_DIM), F32)
    zeros = jnp.zeros((npos, DH_A - ROT_DIM), F32)
    zh = jnp.zeros((npos, ROT_HALF), F32)
    head = lambda parts: jnp.concatenate(parts * (LANES // DH_A), axis=1)
    return (head([cos, cos, ones]), head([-sin, zh, zeros]), head([zh, sin, zeros]))


def _dwconv_block(window, w_ref, b_ref, g_ref, beta_ref):
    acc = jnp.broadcast_to(b_ref[...], (BLK, C_CONV))
    for sub in range(SUBLANES):
        shifted = window if sub == 0 else pltpu.roll(window, HIST + BLK - sub, axis=0)
        for grp in range(HIST // SUBLANES + 1):
            tap = grp * SUBLANES + sub - (HIST - (CONV_W - 1))
            if 0 <= tap < CONV_W:
                acc = acc + w_ref[tap:tap + 1, :] * shifted[grp * SUBLANES:grp * SUBLANES + BLK, :]
    return _ln_silu(acc, g_ref[...], beta_ref[...])


def _in_proj_kernel(conv_tiles, x_ref, g_ref, w_ref, bif_ref, cos_ref, sin_lo_ref, sin_hi_ref, *refs):
    if conv_tiles:
        hist0_ref, wdc_ref, bdc_ref, lng_ref, lnb_ref = refs[:5]
        qkvm_ref, og_ref, u_ref, qa_ref, ka_ref, va_ref, ifg_ref, cv_ref, buf = refs[5:]
    else:
        qkvm_ref, og_ref, u_ref, qa_ref, ka_ref, va_ref, ifg_ref = refs
    tm, d_model = x_ref.shape
    h = _rms(x_ref[...], g_ref[...]).astype(BF16)
    act = qkvm_ref.dtype

    def proj(off, n):
        return jnp.dot(h, w_ref[:, off:off + n], preferred_element_type=F32)

    def glu_rows(rows):
        glu = jnp.dot(h[rows, :], w_ref[:, OFF_GLU:OFF_GLU + 2 * C_CONV], preferred_element_type=F32)
        u = glu[:, :C_CONV] * _sigmoid(glu[:, C_CONV:])
        u_ref[rows, :] = u.astype(act)
        return u

    if conv_tiles:
        @pl.when(pl.program_id(0) % conv_tiles == 0)
        def _():
            buf[0:HIST, :] = hist0_ref[...]
    else:
        glu_rows(slice(0, tm))

    def conv_block(blk):
        rows = slice(blk * BLK, (blk + 1) * BLK)
        buf[HIST + blk * BLK:HIST + (blk + 1) * BLK, :] = glu_rows(rows)
        window = buf[blk * BLK:blk * BLK + HIST + BLK, :]
        cv_ref[rows, :] = _dwconv_block(window, wdc_ref, bdc_ref, lng_ref, lnb_ref).astype(act)

    def mlstm_qkv():
        qkvm_ref[:, 0:W_M] = proj(OFF_QKVM, W_M).astype(act)
        qkvm_ref[:, W_M:2 * W_M] = (proj(OFF_QKVM + W_M, W_M) * DH_M ** -0.5).astype(act)
        qkvm_ref[:, 2 * W_M:3 * W_M] = proj(OFF_QKVM + 2 * W_M, W_M).astype(act)

    def out_gate():
        og_ref[...] = _sigmoid(proj(OFF_O, W_M)).astype(act)

    def attn_qkv():
        tabs = (cos_ref[...], sin_lo_ref[...], sin_hi_ref[...])
        qa_ref[...] = _rope(proj(OFF_QA, W_Q), *tabs).astype(act)
        kv = proj(OFF_KA, 2 * W_KV)
        ka_ref[...] = _rope(kv[:, :W_KV], *tabs).astype(act)
        va_ref[...] = kv[:, W_KV:].astype(act)

    def log_gates():
        z = proj(OFF_G, LANES) + bif_ref[...]
        lane = lax.broadcasted_iota(jnp.int32, z.shape, 1)
        ifg_ref[...] = jnp.where(lane < NH_M, z, _log_sigmoid(z))

    steps = [mlstm_qkv, out_gate, attn_qkv, log_gates]
    nconv = tm // BLK if conv_tiles else 0
    at_step = {len(steps) * blk // nconv: blk for blk in range(nconv)} if nconv else {}
    assert len(at_step) == nconv
    for i, step in enumerate(steps):
        if i in at_step:
            conv_block(at_step[i])
        step()
    if conv_tiles > 1:
        buf[0:HIST, :] = buf[tm:tm + HIST, :]


def _in_proj(x, g, w, bif, tabs, tm, act, conv=None, conv_tiles=0):
    rows, d_model = x.shape
    ntab = tabs[0].shape[0] // tm
    widths = (3 * W_M, W_M, C_CONV, W_Q, W_KV, W_KV, LANES) + ((C_CONV,) if conv_tiles else ())
    dtypes = (act,) * 6 + (F32,) + ((act,) if conv_tiles else ())
    conv = tuple(conv) if conv_tiles else ()
    row_spec = lambda n: pl.BlockSpec((tm, n), lambda i: (i, 0))
    tab_spec = pl.BlockSpec((tm, LANES), lambda i: (i % ntab, 0))
    return pl.pallas_call(
        functools.partial(_in_proj_kernel, conv_tiles),
        out_shape=[jax.ShapeDtypeStruct((rows, n), dt) for n, dt in zip(widths, dtypes)],
        grid=(rows // tm,),
        in_specs=[row_spec(d_model), _const_spec(g), _const_spec(w),
                  _const_spec(bif), tab_spec, tab_spec, tab_spec]
                 + [_const_spec(a) for a in conv],
        out_specs=[row_spec(n) for n in widths],
        scratch_shapes=[pltpu.VMEM((HIST + tm, C_CONV), F32)] if conv_tiles else [],
        compiler_params=_cparams(("arbitrary",)),
        name="in_proj",
    )(*_operands(x, g, w, bif, *tabs, *conv))


def _mlstm_kernel(nvalid, qkv_ref, ifg_ref, og_ref, c0_ref, n0_ref, m0_ref,
                  hg_ref, c_ref, n_ref, m_ref):
    @pl.when(pl.program_id(1) == 0)
    def _():
        c_ref[...] = jnp.broadcast_to(c0_ref[...], c_ref.shape)
        n_ref[...] = jnp.broadcast_to(n0_ref[...], n_ref.shape)
        m_ref[...] = jnp.broadcast_to(m0_ref[...], m_ref.shape)

    row = lax.broadcasted_iota(jnp.int32, (BLK, LANES), 0)
    lane = lax.broadcasted_iota(jnp.int32, (BLK, LANES), 1)
    causal = (lax.broadcasted_iota(jnp.int32, (BLK, BLK), 0)
              >= lax.broadcasted_iota(jnp.int32, (BLK, BLK), 1))

    heads = []
    for sb in range(qkv_ref.shape[0]):
        gate = jnp.where(lane < 2 * NH_M, ifg_ref[sb], 0.0)
        if nvalid < BLK:
            gate = jnp.where(row >= nvalid, jnp.where(lane < NH_M, -jnp.inf, 0.0), gate)
        csum = jnp.where(lane < NH_M, 0.0, gate)
        shift = 1
        while shift < BLK:
            csum = csum + jnp.where(row >= shift, pltpu.roll(csum, shift, axis=0), 0.0)
            shift *= 2
        b_all = pltpu.roll(csum, LANES - NH_M, axis=1)
        c_all = gate - b_all
        cmax = c_all
        shift = 1
        while shift < BLK:
            cmax = jnp.maximum(cmax, jnp.where(row >= shift, pltpu.roll(cmax, shift, axis=0), -jnp.inf))
            shift *= 2
        m_prev_all = m_ref[sb, 0:1, :]
        mm_all = jnp.maximum(cmax, m_prev_all)
        w_prev_all = jnp.exp(m_prev_all - mm_all)
        floor_all = jnp.exp(-(b_all + mm_all))
        mm_last = mm_all[BLK - 1:BLK, :]
        wts_all = jnp.exp(c_all - mm_last)
        decay_all = w_prev_all[BLK - 1:BLK, :]
        m_ref[sb] = jnp.broadcast_to(b_all[BLK - 1:BLK, :] + mm_last, (SUBLANES, LANES))
        c_rows = c_all.T

        for hd in range(NH_M):
            sl = slice(hd * DH_M, (hd + 1) * DH_M)
            q = qkv_ref[sb, :, sl]
            k = qkv_ref[sb, :, W_M + hd * DH_M:W_M + (hd + 1) * DH_M].astype(F32)
            v = qkv_ref[sb, :, 2 * W_M + hd * DH_M:2 * W_M + (hd + 1) * DH_M].astype(BF16)
            c = c_ref[sb, hd]
            n_row = n_ref[sb, hd:hd + 1, :]
            decay = decay_all[:, hd:hd + 1]

            qk = _dot_nt(q, k)
            qc = _dot(q, c)
            qn = jnp.sum(q.astype(F32) * n_row, axis=-1, keepdims=True)
            dmat = jnp.exp(jnp.where(causal, c_rows[hd:hd + 1, :] - mm_all[:, hd:hd + 1], -jnp.inf))
            kw = k * wts_all[:, hd:hd + 1]
            c_ref[sb, hd] = decay * c + lax.dot_general(
                kw.astype(BF16), v, (((0,), (0,)), ((), ())), preferred_element_type=F32)
            n_ref[sb, hd:hd + 1, :] = decay * n_row + jnp.sum(kw, axis=0, keepdims=True)
            heads.append((sb, sl, v, qk * dmat, w_prev_all[:, hd:hd + 1], qc, qn, floor_all[:, hd:hd + 1]))

    for sb, sl, v, s, w_prev, qc, qn, floor in heads:
        num = _dot(s, v) + w_prev * qc
        den = jnp.sum(s, axis=-1, keepdims=True) + w_prev * qn
        h = num / jnp.maximum(jnp.abs(den), floor)
        hg_ref[sb, :, sl] = (og_ref[sb, :, sl].astype(F32) * h).astype(hg_ref.dtype)


def _mlstm(qkvm, ifg, og, c0, n0, m0, nsb, nvalid):
    nseq, seq, _ = qkvm.shape
    blk_spec = lambda n: pl.BlockSpec((nsb, BLK, n), lambda b, j: (b, j, 0))
    return pl.pallas_call(
        functools.partial(_mlstm_kernel, nvalid),
        out_shape=[jax.ShapeDtypeStruct((nseq, seq, W_M), og.dtype),
                   jax.ShapeDtypeStruct((nseq, NH_M, DH_M, DH_M), F32),
                   jax.ShapeDtypeStruct((nseq, SUBLANES, LANES), F32),
                   jax.ShapeDtypeStruct((nseq, SUBLANES, LANES), F32)],
        grid=(nseq // nsb, seq // BLK),
        in_specs=[blk_spec(3 * W_M), blk_spec(LANES), blk_spec(W_M),
                  _const_spec(c0), _const_spec(n0), _const_spec(m0)],
        out_specs=[blk_spec(W_M),
                   pl.BlockSpec((nsb, NH_M, DH_M, DH_M), lambda b, j: (b, 0, 0, 0)),
                   pl.BlockSpec((nsb, SUBLANES, LANES), lambda b, j: (b, 0, 0)),
                   pl.BlockSpec((nsb, SUBLANES, LANES), lambda b, j: (b, 0, 0))],
        compiler_params=_cparams(("arbitrary", "arbitrary")),
        name="mlstm",
    )(qkvm, ifg, og, c0, n0, m0)


def _mlstm_step_kernel(qkv_ref, ifg_ref, og_ref, c_ref, n_ref, m_ref,
                       hg_ref, c_out, n_out, m_out):
    gate = ifg_ref[...]
    ig = gate
    lf = pltpu.roll(gate, LANES - NH_M, axis=1)
    log_prev = lf + m_ref[...]
    m_t = jnp.maximum(log_prev, ig)
    dmat = jnp.exp(ig - m_t)
    w_prev = jnp.exp(log_prev - m_t)
    floor = jnp.exp(-m_t)
    m_out[...] = m_t

    qs, kds, vs, ss = [], [], [], []
    for hd in range(NH_M):
        q = qkv_ref[:, hd * DH_M:(hd + 1) * DH_M]
        k = qkv_ref[:, W_M + hd * DH_M:W_M + (hd + 1) * DH_M]
        v = qkv_ref[:, 2 * W_M + hd * DH_M:2 * W_M + (hd + 1) * DH_M]
        d_h = dmat[:, hd:hd + 1]
        qs.append(q)
        kds.append(k * d_h)
        vs.append(v)
        ss.append(jnp.sum(q * k, axis=-1, keepdims=True) * d_h)
    stacked = jnp.concatenate(qs + kds + [jnp.zeros((BLK - 2 * NH_M * DEC_ROWS, DH_M), F32)], axis=0)
    cols = stacked.T

    for r in range(DEC_ROWS):
        for hd in range(NH_M):
            q_col = cols[:, hd * DEC_ROWS + r:hd * DEC_ROWS + r + 1]
            k_col = cols[:, (NH_M + hd) * DEC_ROWS + r:(NH_M + hd) * DEC_ROWS + r + 1]
            c = c_ref[r, hd]
            n_row = n_ref[r, hd:hd + 1, :]
            v_row = vs[hd][r:r + 1, :]
            wp = w_prev[r:r + 1, hd:hd + 1]
            s = ss[hd][r:r + 1, :]
            qc = jnp.sum(q_col * c, axis=0, keepdims=True)
            qn = jnp.sum(qs[hd][r:r + 1, :] * n_row, axis=-1, keepdims=True)
            num = s * v_row + wp * qc
            den = s + wp * qn
            h = num / jnp.maximum(jnp.abs(den), floor[r:r + 1, hd:hd + 1])
            hg_ref[r:r + 1, hd * DH_M:(hd + 1) * DH_M] = og_ref[r:r + 1, hd * DH_M:(hd + 1) * DH_M] * h
            c_out[r, hd] = wp * c + k_col * v_row
            n_out[r, hd:hd + 1, :] = wp * n_row + kds[hd][r:r + 1, :]


def _mlstm_step(qkvm, ifg, og, c, n, m):
    nseq = qkvm.shape[0]
    row_spec = lambda w: pl.BlockSpec((DEC_ROWS, w), lambda i: (i, 0))
    c_spec = pl.BlockSpec((DEC_ROWS, NH_M, DH_M, DH_M), lambda i: (i, 0, 0, 0))
    n_spec = pl.BlockSpec((DEC_ROWS, NH_M, DH_M), lambda i: (i, 0, 0))
    return pl.pallas_call(
        _mlstm_step_kernel,
        out_shape=[jax.ShapeDtypeStruct((nseq, W_M), F32),
                   jax.ShapeDtypeStruct(c.shape, F32),
                   jax.ShapeDtypeStruct(n.shape, F32),
                   jax.ShapeDtypeStruct((nseq, LANES), F32)],
        grid=(nseq // DEC_ROWS,),
        in_specs=[row_spec(3 * W_M), row_spec(LANES), row_spec(W_M),
                  _state_spec(c, DEC_ROWS), _state_spec(n, DEC_ROWS), row_spec(LANES)],
        out_specs=[row_spec(W_M), c_spec, n_spec, row_spec(LANES)],
        compiler_params=_cparams(("arbitrary",)),
        name="mlstm_step",
    )(*_operands(qkvm, ifg, og, c, n, m))


def _ln_silu(y, g, b):
    yc = y - jnp.mean(y, axis=-1, keepdims=True)
    var = jnp.mean(yc * yc, axis=-1, keepdims=True)
    return _silu(yc * lax.rsqrt(var + LN_EPS) * g + b)


def _conv_step_kernel(st_ref, u_ref, w_ref, b_ref, g_ref, beta_ref, out_ref):
    w_hist = w_ref[0:CONV_W - 1, :]
    rows = []
    for r in range(DEC_ROWS):
        rows.append(jnp.sum(st_ref[r] * w_hist, axis=0, keepdims=True))
    acc = jnp.concatenate(rows, axis=0) + w_ref[CONV_W - 1:CONV_W, :] * u_ref[...] + b_ref[...]
    out_ref[...] = _ln_silu(acc, g_ref[...], beta_ref[...])


def _conv_step(state, u, w, b, g, beta):
    nseq = u.shape[0]
    return pl.pallas_call(
        _conv_step_kernel,
        out_shape=jax.ShapeDtypeStruct(u.shape, F32),
        grid=(nseq // DEC_ROWS,),
        in_specs=[_state_spec(state, DEC_ROWS),
                  pl.BlockSpec((DEC_ROWS, C_CONV), lambda i: (i, 0)),
                  _const_spec(w), _const_spec(b), _const_spec(g), _const_spec(beta)],
        out_specs=pl.BlockSpec((DEC_ROWS, C_CONV), lambda i: (i, 0)),
        compiler_params=_cparams(("arbitrary",)),
        name="dwconv_step",
    )(*_operands(state, u, w, b, g, beta))


def _stack_heads(q, low_half):
    heads = []
    for r in range(REP_A):
        slab = q[:, r * LANES:(r + 1) * LANES]
        heads.append(jnp.where(low_half, slab, 0.0))
        heads.append(jnp.where(low_half, 0.0, slab))
    return jnp.concatenate(heads, axis=0)


def _attn_block_kernel(base, carry, q_ref, k_ref, v_ref, k0_ref, v0t_ref, pos0_ref, sink_ref,
                       o_ref, kprev, vtprev):
    j = pl.program_id(1)
    nsb = q_ref.shape[0]

    if carry:
        @pl.when(j == 0)
        def _():
            kprev[...] = jnp.broadcast_to(k0_ref[...], kprev.shape).astype(kprev.dtype)
            vtprev[...] = jnp.broadcast_to(v0t_ref[...], vtprev.shape).astype(vtprev.dtype)

    kpos_prev = jnp.where(j == 0, pos0_ref[...],
                          base + BLK * j - BLK + lax.broadcasted_iota(jnp.int32, (BLK, 1), 0))
    kpos = jnp.concatenate([kpos_prev, base + BLK * j + lax.broadcasted_iota(jnp.int32, (BLK, 1), 0)], axis=0)
    qpos = base + BLK * j + lax.broadcasted_iota(jnp.int32, (1, BLK), 1)
    bias = jnp.where((kpos >= 0) & (kpos <= qpos) & (qpos - kpos <= WINDOW), 0.0, -jnp.inf)
    low_half = lax.broadcasted_iota(jnp.int32, (BLK, LANES), 1) < DH_A
    scale = DH_A ** -0.5

    scores, vt_cats = [], []
    for sb in range(nsb):
        k = k_ref[sb]
        vt = v_ref[sb].astype(F32).T.astype(BF16)
        k_prev, vt_prev = (kprev[sb], vtprev[sb]) if carry else (k0_ref[0], v0t_ref[0])
        k_cat = jnp.concatenate([k_prev.astype(BF16), k.astype(BF16)], axis=0)
        vt_cats.append(jnp.concatenate([vt_prev.astype(BF16), vt], axis=1))
        scores.append(_dot_nt(k_cat, _stack_heads(q_ref[sb] * scale, low_half)))
        if carry:
            kprev[sb] = k
            vtprev[sb] = vt

    probs, invs = [], []
    for st in scores:
        p_heads, inv = [], []
        for hd in range(H_A):
            r, g = divmod(hd, KV_A)
            sink = sink_ref[0:1, g * REP_A + r:g * REP_A + r + 1]
            s = st[:, hd * BLK:(hd + 1) * BLK] + bias
            mx = jnp.maximum(jnp.max(s, axis=0, keepdims=True), sink)
            p = jnp.exp(s - mx)
            inv.append(1.0 / (jnp.sum(p, axis=0, keepdims=True) + jnp.exp(sink - mx)))
            p_heads.append(p.astype(BF16))
        probs.append(jnp.concatenate(p_heads, axis=1))
        invs.append(inv)

    for sb in range(nsb):
        ot = jnp.dot(vt_cats[sb], probs[sb], preferred_element_type=F32)
        out_t = jnp.concatenate(
            [ot[(hd % KV_A) * DH_A:(hd % KV_A + 1) * DH_A, hd * BLK:(hd + 1) * BLK] * invs[sb][hd]
             for hd in range(H_A)], axis=0)
        o_ref[sb] = out_t.T.astype(o_ref.dtype)


def _attn_blocks(q, k, v, k0, v0t, pos0, sinks, nsb, base):
    nseq, seq, _ = q.shape
    q_spec = pl.BlockSpec((nsb, BLK, W_Q), lambda s, j: (s, j, 0))
    k_spec = pl.BlockSpec((nsb, BLK, W_KV), lambda s, j: (s, j, 0))
    return pl.pallas_call(
        functools.partial(_attn_block_kernel, base, seq > BLK),
        out_shape=jax.ShapeDtypeStruct(q.shape, q.dtype),
        grid=(nseq // nsb, seq // BLK),
        in_specs=[q_spec, k_spec, k_spec, _const_spec(k0), _const_spec(v0t),
                  _const_spec(pos0), _const_spec(sinks)],
        out_specs=q_spec,
        scratch_shapes=[pltpu.VMEM((nsb, BLK, W_KV), k.dtype), pltpu.VMEM((nsb, W_KV, BLK), BF16)],
        compiler_params=_cparams(("arbitrary", "arbitrary")),
        name="swa",
    )(*_operands(q, k, v, k0, v0t, pos0, sinks))


def _attn_step_kernel(base, q_ref, k_ref, v_ref, kc_ref, vc_ref, sink_ref, o_ref):
    low_half = lax.broadcasted_iota(jnp.int32, (1, LANES), 1) < DH_A
    sinks = jnp.concatenate([sink_ref[0:1, g * REP_A + r:g * REP_A + r + 1]
                             for r in range(REP_A) for g in range(KV_A)], axis=0)
    kpos = base - WINDOW + lax.broadcasted_iota(jnp.int32, (1, 2 * BLK), 1)
    bias = jnp.where((kpos >= 0) & (kpos <= base) & (base - kpos <= WINDOW), 0.0, -jnp.inf)
    pad = jnp.zeros((BLK - 1, W_KV), F32)
    scale = DH_A ** -0.5
    nsb = q_ref.shape[0]
    scores = []
    for sb in range(nsb):
        k_cat = jnp.concatenate([kc_ref[sb], k_ref[sb], pad], axis=0)
        scores.append(_dot_nt(_stack_heads(q_ref[sb] * scale, low_half), k_cat) + bias)
    probs = []
    for s in scores:
        mx = jnp.maximum(jnp.max(s, axis=-1, keepdims=True), sinks)
        p = jnp.exp(s - mx)
        den = jnp.sum(p, axis=-1, keepdims=True) + jnp.exp(sinks - mx)
        probs.append(p * (1.0 / den))
    for sb in range(nsb):
        v_cat = jnp.concatenate([vc_ref[sb], v_ref[sb], pad], axis=0)
        o = _dot(probs[sb], v_cat)
        for r in range(REP_A):
            o_ref[sb, :, r * LANES:(r + 1) * LANES] = jnp.where(low_half, o[2 * r:2 * r + 1], o[2 * r + 1:2 * r + 2])


def _attn_step(q, k, v, k_cache, v_cache, sinks, base):
    nseq = q.shape[0]
    spec = lambda rows, n: pl.BlockSpec((DEC_ROWS, rows, n), lambda i: (i, 0, 0))
    return pl.pallas_call(
        functools.partial(_attn_step_kernel, base),
        out_shape=jax.ShapeDtypeStruct(q.shape, F32),
        grid=(nseq // DEC_ROWS,),
        in_specs=[spec(1, W_Q), spec(1, W_KV), spec(1, W_KV), spec(WINDOW, W_KV), spec(WINDOW, W_KV),
                  _const_spec(sinks)],
        out_specs=spec(1, W_Q),
        compiler_params=_cparams(("arbitrary",)),
        name="swa_step",
    )(*_operands(q, k, v, k_cache, v_cache, sinks))


def _merge_kernel(x_ref, hg_ref, cv_ref, oa_ref, g1_ref, wgm_ref, bm_ref, wm_ref, wc_ref, wa_ref, wo_ref,
                  out_ref):
    d_model = x_ref.shape[1]
    x = x_ref[...]
    h = _rms(x, g1_ref[...]).astype(BF16)
    mix = None
    for part, (branch_ref, w_ref) in enumerate(((hg_ref, wm_ref), (cv_ref, wc_ref), (oa_ref, wa_ref))):
        cols = slice(part * d_model, (part + 1) * d_model)
        gate = _sigmoid(jnp.dot(h, wgm_ref[:, cols], preferred_element_type=F32) + bm_ref[:, cols])
        term = gate * _dot(branch_ref[...], w_ref[...])
        mix = term if mix is None else mix + term
    out_ref[...] = x + _dot(mix, wo_ref[...])


def _merge(x, hg, cv, oa, g1, wgm, bm, wm, wc, wa, wo, tm):
    rows, d_model = x.shape
    row_spec = lambda n: pl.BlockSpec((tm, n), lambda i: (i, 0))
    return pl.pallas_call(
        _merge_kernel,
        out_shape=jax.ShapeDtypeStruct(x.shape, F32),
        grid=(rows // tm,),
        in_specs=[row_spec(d_model), row_spec(W_M), row_spec(C_CONV), row_spec(W_Q),
                  _const_spec(g1), _const_spec(wgm), _const_spec(bm),
                  _const_spec(wm), _const_spec(wc), _const_spec(wa), _const_spec(wo)],
        out_specs=row_spec(d_model),
        compiler_params=_cparams(("arbitrary",)),
        name="merge",
    )(*_operands(x, hg, cv, oa, g1, wgm, bm, wm, wc, wa, wo))


def _ffn_body(x, g2_ref, wg_ref, wu_ref, wd_ref, wc_ref, bc_ref, prev_rows):
    d_ff = wg_ref.shape[1]
    tiles = d_ff // MXU_TILE
    split = (tiles + 1) // 2 * MXU_TILE if tiles >= 2 else d_ff
    bounds = [(0, split), (split, d_ff)] if split < d_ff else [(0, d_ff)]
    h2 = _rms(x, g2_ref[...]).astype(BF16)
    gps, acts = [], []
    for lo, hi in bounds:
        gp = jnp.dot(h2, wg_ref[:, lo:hi], preferred_element_type=F32)
        up = jnp.dot(h2, wu_ref[:, lo:hi], preferred_element_type=F32)
        p2, p1 = prev_rows(gp, lo, hi)
        gc = (wc_ref[0:1, lo:hi] * p2 + wc_ref[1:2, lo:hi] * p1 + wc_ref[2:3, lo:hi] * gp
              + bc_ref[:, lo:hi])
        acts.append((_silu(gc) * up).astype(BF16))
        gps.append(gp)
    acc = x
    for (lo, hi), a in zip(bounds, acts):
        acc = acc + jnp.dot(a, wd_ref[lo:hi, :], preferred_element_type=F32)
    return acc, gps


def _ffn_seq_kernel(tiles_per_seq, tail_end, final, x_ref, g2_ref, wg_ref, wu_ref, wd_ref, wc_ref, bc_ref,
                    init_ref, gf_ref, out_ref, tail_ref, carry):
    @pl.when(pl.program_id(0) % tiles_per_seq == 0)
    def _():
        carry[...] = init_ref[...]

    tm = x_ref.shape[0]
    row = lax.broadcasted_iota(jnp.int32, (tm, 1), 0)

    def prev_rows(gp, lo, hi):
        c2 = carry[SUBLANES - 2:SUBLANES - 1, lo:hi]
        c1 = carry[SUBLANES - 1:SUBLANES, lo:hi]
        p1 = jnp.where(row == 0, c1, pltpu.roll(gp, 1, axis=0))
        p2 = jnp.where(row == 0, c2, jnp.where(row == 1, c1, pltpu.roll(gp, 2, axis=0)))
        return p2, p1

    acc, gps = _ffn_body(x_ref[...], g2_ref, wg_ref, wu_ref, wd_ref, wc_ref, bc_ref, prev_rows)
    out_ref[...] = _rms(acc, gf_ref[...]) if final else acc
    tail = jnp.concatenate([gp[tail_end - SUBLANES:tail_end, :] for gp in gps], axis=1)
    tail_ref[...] = tail
    if tiles_per_seq > 1:
        carry[...] = tail


def _ffn_seq(x, g2, wg, wu, wd, wc, bc, init, gf, tm, tiles_per_seq, tail_end, final):
    rows, d_model = x.shape
    d_ff = wg.shape[1]
    assert tiles_per_seq == 1 or tail_end == tm
    return pl.pallas_call(
        functools.partial(_ffn_seq_kernel, tiles_per_seq, tail_end, final),
        out_shape=[jax.ShapeDtypeStruct(x.shape, F32),
                   jax.ShapeDtypeStruct((rows // tm, SUBLANES, d_ff), F32)],
        grid=(rows // tm,),
        in_specs=[pl.BlockSpec((tm, d_model), lambda i: (i, 0)), _const_spec(g2),
                  _const_spec(wg), _const_spec(wu), _const_spec(wd),
                  _const_spec(wc), _const_spec(bc), _const_spec(init),
                  _const_spec(gf)],
        out_specs=[pl.BlockSpec((tm, d_model), lambda i: (i, 0)),
                   pl.BlockSpec((None, SUBLANES, d_ff), lambda i: (i, 0, 0))],
        scratch_shapes=[pltpu.VMEM((SUBLANES, d_ff), F32)],
        compiler_params=_cparams(("arbitrary",)),
        name="convffn",
    )(*_operands(x, g2, wg, wu, wd, wc, bc, init, gf))


def _ffn_step_kernel(final, x_ref, g2_ref, wg_ref, wu_ref, wd_ref, wc_ref, bc_ref,
                     p2_ref, p1_ref, gf_ref, out_ref, gp_ref):
    def prev_rows(gp, lo, hi):
        return p2_ref[:, lo:hi], p1_ref[:, lo:hi]

    acc, gps = _ffn_body(x_ref[...], g2_ref, wg_ref, wu_ref, wd_ref, wc_ref, bc_ref, prev_rows)
    out_ref[...] = _rms(acc, gf_ref[...]) if final else acc
    gp_ref[...] = jnp.concatenate(gps, axis=1)


def _ffn_step(x, g2, wg, wu, wd, wc, bc, p2, p1, gf, tm, final):
    rows, d_model = x.shape
    d_ff = wg.shape[1]
    row_spec = lambda n: pl.BlockSpec((tm, n), lambda i: (i, 0))
    return pl.pallas_call(
        functools.partial(_ffn_step_kernel, final),
        out_shape=[jax.ShapeDtypeStruct(x.shape, F32), jax.ShapeDtypeStruct((rows, d_ff), F32)],
        grid=(rows // tm,),
        in_specs=[row_spec(d_model), _const_spec(g2),
                  _const_spec(wg), _const_spec(wu), _const_spec(wd),
                  _const_spec(wc), _const_spec(bc), row_spec(d_ff), row_spec(d_ff),
                  _const_spec(gf)],
        out_specs=[row_spec(d_model), row_spec(d_ff)],
        compiler_params=_cparams(("arbitrary",)),
        name="convffn_step",
    )(*_operands(x, g2, wg, wu, wd, wc, bc, p2, p1, gf))


def _row_tile(rows, cap):
    tm = min(rows, cap)
    while rows % tm:
        tm //= 2
    return tm


def kernel(x_prompt, x_sample, state_mlstm_c, state_mlstm_n, state_mlstm_m, state_conv, cache_swa_k, cache_swa_v, state_ffn_conv, meta_tokens, norm1_g, w_in, b_igate, b_fgate, w_mlstm_out, w_dconv, b_dconv, ln_conv_g, ln_conv_b, w_conv_out, attn_sinks, w_attn_out, b_merge, w_out, norm2_g, w_ffn_gate, w_ffn_up, w_ffn_conv, b_ffn_conv, w_ffn_down, final_norm_g):
    bsz, seq, d_model = x_prompt.shape
    nseq_s = x_sample.shape[0]
    depth = w_in.shape[0]
    d_ff = w_ffn_gate.shape[2]
    nblk = seq // BLK
    assert seq % BLK == 0 and x_sample.shape[1] == 1 and nseq_s % DEC_ROWS == 0
    assert cache_swa_k.shape[2] == WINDOW
    act = BF16
    nsb = SEQ_PER_STEP if bsz % SEQ_PER_STEP == 0 else 1

    x_main = x_prompt.reshape(bsz * seq, d_model)
    x_meta = jnp.concatenate([meta_tokens.astype(F32), jnp.zeros((BLK - N_META, d_model), F32)], axis=0)
    x_smp = x_sample.reshape(nseq_s, d_model)

    tm_main = _row_tile(seq, 512)
    tm_smp = _row_tile(nseq_s, 128)

    tab_meta = _rope_tables(jnp.arange(BLK))
    tab_main = _rope_tables(N_META + jnp.arange(seq))
    tab_smp = _rope_tables(jnp.full((tm_smp,), PAST_LEN))
    def regroup_heads(w, axis):
        shape = w.shape
        w = w.reshape(shape[:axis] + (KV_A, REP_A, DH_A) + shape[axis + 1:])
        return jnp.swapaxes(w, axis, axis + 1).reshape(shape)
    iota_blk = jnp.arange(BLK, dtype=jnp.int32)[:, None]
    pos0_meta = jnp.full((BLK, 1), -1, jnp.int32)
    pos0_main = jnp.where(iota_blk < N_META, iota_blk, -1)

    zeros_c = jnp.zeros((1, NH_M, DH_M, DH_M), F32)
    zeros_nm = jnp.zeros((1, SUBLANES, LANES), F32)
    zeros_kv = jnp.zeros((1, BLK, W_KV), act)
    gf = final_norm_g.reshape(1, d_model)

    col_gates = 3 * W_M
    col_o = col_gates + 2 * NH_M
    col_qa = col_o + W_M + 2 * C_CONV
    col_g = col_qa + W_Q + 2 * W_KV
    w_pack_all = jnp.concatenate(
        [w_in[:, :, :col_gates], w_in[:, :, col_o:col_qa], regroup_heads(w_in[:, :, col_qa:col_qa + W_Q], 2),
         w_in[:, :, col_qa + W_Q:col_g], w_in[:, :, col_gates:col_o],
         jnp.zeros((depth, d_model, LANES - 2 * NH_M), F32)], axis=2).astype(BF16)
    bif_all = jnp.concatenate([b_igate, b_fgate, jnp.zeros((depth, LANES - 2 * NH_M), F32)], axis=1)[:, None]
    w_dc_all = jnp.concatenate([w_dconv, jnp.zeros((depth, HIST - CONV_W, C_CONV), F32)], axis=1)
    sinks_all = jnp.concatenate([attn_sinks, jnp.zeros((depth, LANES - H_A), F32)], axis=1)[:, None]
    w_fc_all = jnp.concatenate([w_ffn_conv, jnp.zeros((depth, SUBLANES - FFN_CONV_W, d_ff), F32)], axis=1)
    stacks = dict(
        g1=norm1_g[:, None], w_pack=w_pack_all, wgm=w_in[:, :, col_g:].astype(BF16), bm=b_merge[:, None],
        bif=bif_all,
        w_dc=w_dc_all, b_dc=b_dconv[:, None], ln_g=ln_conv_g[:, None], ln_b=ln_conv_b[:, None],
        sinks=sinks_all, wm=w_mlstm_out.astype(BF16), wc=w_conv_out.astype(BF16),
        wa=regroup_heads(w_attn_out, 1).astype(BF16), wo=w_out.astype(BF16), g2=norm2_g[:, None],
        wg=w_ffn_gate.astype(BF16), wu=w_ffn_up.astype(BF16), wd=w_ffn_down.astype(BF16),
        w_fc=w_fc_all, b_fc=b_ffn_conv[:, None])

    p_states = [[] for _ in range(7)]
    s_states = [[] for _ in range(7)]
    for l in range(depth):
        final = l == depth - 1
        prm = {name: _LayerParam(stack, l) for name, stack in stacks.items()}
        g1, w_pack, wgm, bm, bif = prm["g1"], prm["w_pack"], prm["wgm"], prm["bm"], prm["bif"]
        w_dc, b_dc, ln_g, ln_b, sinks = prm["w_dc"], prm["b_dc"], prm["ln_g"], prm["ln_b"], prm["sinks"]
        wm, wc, wa, wo, g2 = prm["wm"], prm["wc"], prm["wa"], prm["wo"], prm["g2"]
        wg, wu, wd, w_fc, b_fc = prm["wg"], prm["wu"], prm["wd"], prm["w_fc"], prm["b_fc"]

        conv_w = (w_dc, b_dc, ln_g, ln_b)
        qkvm, og, u, qa, ka, va, ifg, cv = _in_proj(
            x_meta, g1, w_pack, bif, tab_meta, BLK, act, (jnp.zeros((HIST, C_CONV), F32),) + conv_w, 1)
        hg, c_meta, n_meta, m_meta = _mlstm(qkvm[None], ifg[None], og[None], zeros_c, zeros_nm, zeros_nm,
                                            1, N_META)
        oa = _attn_blocks(qa[None], ka[None], va[None], zeros_kv, zeros_kv, pos0_meta, sinks, 1, 0)
        x1 = _merge(x_meta, hg[0], cv, oa[0], g1, wgm, bm, wm, wc, wa, wo, BLK)
        x_meta, tail_meta = _ffn_seq(x1, g2, wg, wu, wd, w_fc, b_fc, jnp.zeros((SUBLANES, d_ff), F32), gf,
                                     BLK, 1, N_META, False)
        u_meta, ka_meta, va_meta = u, ka, va

        hist0 = jnp.concatenate([jnp.zeros((HIST - N_META, C_CONV), F32), u_meta[:N_META].astype(F32)], axis=0)
        qkvm, og, u, qa, ka, va, ifg, cv = _in_proj(
            x_main, g1, w_pack, bif, tab_main, tm_main, act, (hist0,) + conv_w, seq // tm_main)
        per_seq = lambda a: a.reshape(bsz, seq, a.shape[-1])
        hg, p_c, p_n, p_m = _mlstm(per_seq(qkvm), per_seq(ifg), per_seq(og), c_meta, n_meta, m_meta, nsb, BLK)
        oa = _attn_blocks(per_seq(qa), per_seq(ka), per_seq(va), ka_meta[None], va_meta.T[None], pos0_main,
                          sinks, nsb, N_META)
        flat = lambda a: a.reshape(bsz * seq, a.shape[-1])
        x1 = _merge(x_main, flat(hg), cv, flat(oa), g1, wgm, bm, wm, wc, wa, wo, tm_main)
        x_main, tails = _ffn_seq(x1, g2, wg, wu, wd, w_fc, b_fc, tail_meta[0], gf,
                                 tm_main, seq // tm_main, tm_main, final)
        p_states[0].append(p_c)
        p_states[1].append(p_n[:, :NH_M, :])
        p_states[2].append(p_m[:, 0, :NH_M])
        p_states[3].append(per_seq(u)[:, seq - (CONV_W - 1):].astype(F32))
        last_window = lambda a: per_seq(a)[:, seq - WINDOW:].astype(F32).reshape(bsz, WINDOW, KV_A, DH_A)
        p_states[4].append(last_window(ka))
        p_states[5].append(last_window(va))
        p_states[6].append(tails.reshape(bsz, seq // tm_main, SUBLANES, d_ff)[:, -1, SUBLANES - (FFN_CONV_W - 1):])

        qkvm, og, u, qa, ka, va, ifg = _in_proj(x_smp, g1, w_pack, bif, tab_smp, tm_smp, F32)
        m_in = jnp.concatenate([state_mlstm_m[l], jnp.zeros((nseq_s, LANES - NH_M), F32)], axis=1)
        hg, s_c, s_n, s_m = _mlstm_step(qkvm, ifg, og, _LayerParam(state_mlstm_c, l),
                                        _LayerParam(state_mlstm_n, l), m_in)
        cv = _conv_step(_LayerParam(state_conv, l), u, w_dc, b_dc, ln_g, ln_b)
        oa = _attn_step(qa[:, None], ka[:, None], va[:, None],
                        cache_swa_k[l].reshape(nseq_s, WINDOW, W_KV), cache_swa_v[l].reshape(nseq_s, WINDOW, W_KV),
                        sinks, PAST_LEN)
        x1 = _merge(x_smp, hg, cv, oa[:, 0], g1, wgm, bm, wm, wc, wa, wo, tm_smp)
        x_smp, gp_s = _ffn_step(x1, g2, wg, wu, wd, w_fc, b_fc, state_ffn_conv[l][:, 0], state_ffn_conv[l][:, 1],
                                gf, tm_smp, final)
        s_states[0].append(s_c)
        s_states[1].append(s_n)
        s_states[2].append(s_m[:, :NH_M])
        s_states[3].append(u[:, None])
        s_states[4].append(ka.reshape(nseq_s, 1, KV_A, DH_A))
        s_states[5].append(va.reshape(nseq_s, 1, KV_A, DH_A))
        s_states[6].append(gp_s[:, None])

    y_prompt = x_main.reshape(bsz, seq, d_model)
    y_sample = x_smp.reshape(nseq_s, 1, d_model)
    slide = lambda old, new_rows: jnp.concatenate([old[:, :, 1:], jnp.stack(new_rows, axis=0)], axis=2)
    s_out = [jnp.stack(a, axis=0) for a in s_states[:3]] + [
        slide(old, new) for old, new in zip((state_conv, cache_swa_k, cache_swa_v, state_ffn_conv), s_states[3:])]
    return (y_prompt, y_sample, *(jnp.stack(a, axis=0) for a in p_states), *s_out)
```

```python
import functools
from typing import NamedTuple

import jax
import jax.numpy as jnp
from jax import lax
from jax.experimental import pallas as pl
from jax.experimental.pallas import tpu as pltpu

F32 = jnp.float32
BF16 = jnp.bfloat16

N_META = 16
NH_M = 4
DH_M = 128
W_M = NH_M * DH_M
C_CONV = 512
CONV_W = 31
H_A = 8
KV_A = 2
DH_A = 64
REP_A = H_A // KV_A
W_Q = H_A * DH_A
W_KV = KV_A * DH_A
ROT_DIM = DH_A // 4
ROT_HALF = ROT_DIM // 2
ROPE_THETA = 500000.0
WINDOW = 128
FFN_CONV_W = 3
PAST_LEN = 16384
RMS_EPS = 1e-6
LN_EPS = 1e-5

BLK = 128
LANES = 128
MXU_TILE = 256
SUBLANES = 8
HIST = 32
DEC_ROWS = 8
SEQ_PER_STEP = 16
VMEM_LIMIT = 52 * 1024 * 1024

OFF_QKVM = 0
OFF_O = OFF_QKVM + 3 * W_M
OFF_GLU = OFF_O + W_M
OFF_QA = OFF_GLU + 2 * C_CONV
OFF_KA = OFF_QA + W_Q
OFF_VA = OFF_KA + W_KV
OFF_G = OFF_VA + W_KV


def _cparams(sem):
    return pltpu.CompilerParams(dimension_semantics=sem, vmem_limit_bytes=VMEM_LIMIT)


class _LayerParam(NamedTuple):
    stack: jax.Array
    layer: int

    @property
    def shape(self):
        return self.stack.shape[1:]


def _const_spec(a):
    nd = len(a.shape)
    if isinstance(a, _LayerParam):
        layer = a.layer
        return pl.BlockSpec((None,) + tuple(a.shape), lambda *_: (layer,) + (0,) * nd,
                            pipeline_mode=pl.Buffered(1))
    return pl.BlockSpec(a.shape, lambda *_: (0,) * nd, pipeline_mode=pl.Buffered(1))


def _state_spec(a, rows):
    tail = tuple(a.shape[1:])
    layer = a.layer
    return pl.BlockSpec((None, rows) + tail, lambda i: (layer, i) + (0,) * len(tail))


def _operands(*args):
    return tuple(a.stack if isinstance(a, _LayerParam) else a for a in args)


def _rms(x, g):
    return x * lax.rsqrt(jnp.mean(x * x, axis=-1, keepdims=True) + RMS_EPS) * g


def _log_sigmoid(x):
    return jnp.minimum(x, 0.0) - jnp.log1p(jnp.exp(-jnp.abs(x)))


def _sigmoid(x):
    return 1.0 / (1.0 + jnp.exp(-x))


def _silu(x):
    return x * _sigmoid(x)


def _dot(a, b):
    return jnp.dot(a.astype(BF16), b.astype(BF16), preferred_element_type=F32)


def _dot_nt(a, b):
    return lax.dot_general(a.astype(BF16), b.astype(BF16), (((1,), (1,)), ((), ())),
                           preferred_element_type=F32)


def _rope(x, cos, sin_lo, sin_hi):
    width = x.shape[1]
    reps = width // LANES
    if reps > 1:
        cos, sin_lo, sin_hi = (jnp.concatenate([t] * reps, axis=1) for t in (cos, sin_lo, sin_hi))
    return (x * cos + pltpu.roll(x, width - ROT_HALF, axis=1) * sin_lo
            + pltpu.roll(x, ROT_HALF, axis=1) * sin_hi)


def _rope_tables(pos):
    inv = jnp.power(ROPE_THETA, -jnp.arange(ROT_HALF, dtype=F32) / ROT_HALF)
    ang = pos.astype(F32)[:, None] * inv[None, :]
    cos, sin = jnp.cos(ang), jnp.sin(ang)
    npos = pos.shape[0]
    ones = jnp.ones((npos, DH_A - ROT_DIM), F32)
    zeros = jnp.zeros((npos, DH_A - ROT_DIM), F32)
    zh = jnp.zeros((npos, ROT_HALF), F32)
    head = lambda parts: jnp.concatenate(parts * (LANES // DH_A), axis=1)
    return (head([cos, cos, ones]), head([-sin, zh, zeros]), head([zh, sin, zeros]))


def _dwconv_block(window, w_ref, b_ref, g_ref, beta_ref):
    acc = jnp.broadcast_to(b_ref[...], (BLK, C_CONV))
    for sub in range(SUBLANES):
        shifted = window if sub == 0 else pltpu.roll(window, HIST + BLK - sub, axis=0)
        for grp in range(HIST // SUBLANES + 1):
            tap = grp * SUBLANES + sub - (HIST - (CONV_W - 1))
            if 0 <= tap < CONV_W:
                acc = acc + w_ref[tap:tap + 1, :] * shifted[grp * SUBLANES:grp * SUBLANES + BLK, :]
    return _ln_silu(acc, g_ref[...], beta_ref[...])


def _in_proj_kernel(conv_tiles, x_ref, g_ref, w_ref, bif_ref, cos_ref, sin_lo_ref, sin_hi_ref, *refs):
    if conv_tiles:
        hist0_ref, wdc_ref, bdc_ref, lng_ref, lnb_ref = refs[:5]
        qkvm_ref, og_ref, u_ref, qa_ref, ka_ref, va_ref, ifg_ref, cv_ref, buf = refs[5:]
    else:
        qkvm_ref, og_ref, u_ref, qa_ref, ka_ref, va_ref, ifg_ref = refs
    tm, d_model = x_ref.shape
    h = _rms(x_ref[...], g_ref[...]).astype(BF16)
    act = qkvm_ref.dtype

    def proj(off, n):
        return jnp.dot(h, w_ref[:, off:off + n], preferred_element_type=F32)

    def glu_rows(rows):
        glu = jnp.dot(h[rows, :], w_ref[:, OFF_GLU:OFF_GLU + 2 * C_CONV], preferred_element_type=F32)
        u = glu[:, :C_CONV] * _sigmoid(glu[:, C_CONV:])
        u_ref[rows, :] = u.astype(act)
        return u

    if conv_tiles:
        @pl.when(pl.program_id(0) % conv_tiles == 0)
        def _():
            buf[0:HIST, :] = hist0_ref[...]
    else:
        glu_rows(slice(0, tm))

    def conv_block(blk):
        rows = slice(blk * BLK, (blk + 1) * BLK)
        buf[HIST + blk * BLK:HIST + (blk + 1) * BLK, :] = glu_rows(rows)
        window = buf[blk * BLK:blk * BLK + HIST + BLK, :]
        cv_ref[rows, :] = _dwconv_block(window, wdc_ref, bdc_ref, lng_ref, lnb_ref).astype(act)

    def mlstm_qkv():
        qkvm_ref[:, 0:W_M] = proj(OFF_QKVM, W_M).astype(act)
        qkvm_ref[:, W_M:2 * W_M] = (proj(OFF_QKVM + W_M, W_M) * DH_M ** -0.5).astype(act)
        qkvm_ref[:, 2 * W_M:3 * W_M] = proj(OFF_QKVM + 2 * W_M, W_M).astype(act)

    def out_gate():
        og_ref[...] = _sigmoid(proj(OFF_O, W_M)).astype(act)

    def attn_qkv():
        tabs = (cos_ref[...], sin_lo_ref[...], sin_hi_ref[...])
        qa_ref[...] = _rope(proj(OFF_QA, W_Q), *tabs).astype(act)
        kv = proj(OFF_KA, 2 * W_KV)
        ka_ref[...] = _rope(kv[:, :W_KV], *tabs).astype(act)
        va_ref[...] = kv[:, W_KV:].astype(act)

    def log_gates():
        z = proj(OFF_G, LANES) + bif_ref[...]
        lane = lax.broadcasted_iota(jnp.int32, z.shape, 1)
        ifg_ref[...] = jnp.where(lane < NH_M, z, _log_sigmoid(z))

    steps = [mlstm_qkv, out_gate, attn_qkv, log_gates]
    nconv = tm // BLK if conv_tiles else 0
    at_step = {len(steps) * blk // nconv: blk for blk in range(nconv)} if nconv else {}
    assert len(at_step) == nconv
    for i, step in enumerate(steps):
        if i in at_step:
            conv_block(at_step[i])
        step()
    if conv_tiles > 1:
        buf[0:HIST, :] = buf[tm:tm + HIST, :]


def _in_proj(x, g, w, bif, tabs, tm, act, conv=None, conv_tiles=0):
    rows, d_model = x.shape
    ntab = tabs[0].shape[0] // tm
    widths = (3 * W_M, W_M, C_CONV, W_Q, W_KV, W_KV, LANES) + ((C_CONV,) if conv_tiles else ())
    dtypes = (act,) * 6 + (F32,) + ((act,) if conv_tiles else ())
    conv = tuple(conv) if conv_tiles else ()
    row_spec = lambda n: pl.BlockSpec((tm, n), lambda i: (i, 0))
    tab_spec = pl.BlockSpec((tm, LANES), lambda i: (i % ntab, 0))
    return pl.pallas_call(
        functools.partial(_in_proj_kernel, conv_tiles),
        out_shape=[jax.ShapeDtypeStruct((rows, n), dt) for n, dt in zip(widths, dtypes)],
        grid=(rows // tm,),
        in_specs=[row_spec(d_model), _const_spec(g), _const_spec(w),
                  _const_spec(bif), tab_spec, tab_spec, tab_spec]
                 + [_const_spec(a) for a in conv],
        out_specs=[row_spec(n) for n in widths],
        scratch_shapes=[pltpu.VMEM((HIST + tm, C_CONV), F32)] if conv_tiles else [],
        compiler_params=_cparams(("arbitrary",)),
        name="in_proj",
    )(*_operands(x, g, w, bif, *tabs, *conv))


def _mlstm_kernel(nvalid, qkv_ref, ifg_ref, og_ref, c0_ref, n0_ref, m0_ref,
                  hg_ref, c_ref, n_ref, m_ref):
    @pl.when(pl.program_id(1) == 0)
    def _():
        c_ref[...] = jnp.broadcast_to(c0_ref[...], c_ref.shape)
        n_ref[...] = jnp.broadcast_to(n0_ref[...], n_ref.shape)
        m_ref[...] = jnp.broadcast_to(m0_ref[...], m_ref.shape)

    row = lax.broadcasted_iota(jnp.int32, (BLK, LANES), 0)
    lane = lax.broadcasted_iota(jnp.int32, (BLK, LANES), 1)
    causal = (lax.broadcasted_iota(jnp.int32, (BLK, BLK), 0)
              >= lax.broadcasted_iota(jnp.int32, (BLK, BLK), 1))

    heads = []
    for sb in range(qkv_ref.shape[0]):
        gate = jnp.where(lane < 2 * NH_M, ifg_ref[sb], 0.0)
        if nvalid < BLK:
            gate = jnp.where(row >= nvalid, jnp.where(lane < NH_M, -jnp.inf, 0.0), gate)
        csum = jnp.where(lane < NH_M, 0.0, gate)
        shift = 1
        while shift < BLK:
            csum = csum + jnp.where(row >= shift, pltpu.roll(csum, shift, axis=0), 0.0)
            shift *= 2
        b_all = pltpu.roll(csum, LANES - NH_M, axis=1)
        c_all = gate - b_all
        cmax = c_all
        shift = 1
        while shift < BLK:
            cmax = jnp.maximum(cmax, jnp.where(row >= shift, pltpu.roll(cmax, shift, axis=0), -jnp.inf))
            shift *= 2
        m_prev_all = m_ref[sb, 0:1, :]
        mm_all = jnp.maximum(cmax, m_prev_all)
        w_prev_all = jnp.exp(m_prev_all - mm_all)
        floor_all = jnp.exp(-(b_all + mm_all))
        mm_last = mm_all[BLK - 1:BLK, :]
        wts_all = jnp.exp(c_all - mm_last)
        decay_all = w_prev_all[BLK - 1:BLK, :]
        m_ref[sb] = jnp.broadcast_to(b_all[BLK - 1:BLK, :] + mm_last, (SUBLANES, LANES))
        c_rows = c_all.T

        for hd in range(NH_M):
            sl = slice(hd * DH_M, (hd + 1) * DH_M)
            q = qkv_ref[sb, :, sl]
            k = qkv_ref[sb, :, W_M + hd * DH_M:W_M + (hd + 1) * DH_M].astype(F32)
            v = qkv_ref[sb, :, 2 * W_M + hd * DH_M:2 * W_M + (hd + 1) * DH_M].astype(BF16)
            c = c_ref[sb, hd]
            n_row = n_ref[sb, hd:hd + 1, :]
            decay = decay_all[:, hd:hd + 1]

            qk = _dot_nt(q, k)
            qc = _dot(q, c)
            qn = jnp.sum(q.astype(F32) * n_row, axis=-1, keepdims=True)
            dmat = jnp.exp(jnp.where(causal, c_rows[hd:hd + 1, :] - mm_all[:, hd:hd + 1], -jnp.inf))
            kw = k * wts_all[:, hd:hd + 1]
            c_ref[sb, hd] = decay * c + lax.dot_general(
                kw.astype(BF16), v, (((0,), (0,)), ((), ())), preferred_element_type=F32)
            n_ref[sb, hd:hd + 1, :] = decay * n_row + jnp.sum(kw, axis=0, keepdims=True)
            heads.append((sb, sl, v, qk * dmat, w_prev_all[:, hd:hd + 1], qc, qn, floor_all[:, hd:hd + 1]))

    for sb, sl, v, s, w_prev, qc, qn, floor in heads:
        num = _dot(s, v) + w_prev * qc
        den = jnp.sum(s, axis=-1, keepdims=True) + w_prev * qn
        h = num / jnp.maximum(jnp.abs(den), floor)
        hg_ref[sb, :, sl] = (og_ref[sb, :, sl].astype(F32) * h).astype(hg_ref.dtype)


def _mlstm(qkvm, ifg, og, c0, n0, m0, nsb, nvalid):
    nseq, seq, _ = qkvm.shape
    blk_spec = lambda n: pl.BlockSpec((nsb, BLK, n), lambda b, j: (b, j, 0))
    return pl.pallas_call(
        functools.partial(_mlstm_kernel, nvalid),
        out_shape=[jax.ShapeDtypeStruct((nseq, seq, W_M), og.dtype),
                   jax.ShapeDtypeStruct((nseq, NH_M, DH_M, DH_M), F32),
                   jax.ShapeDtypeStruct((nseq, SUBLANES, LANES), F32),
                   jax.ShapeDtypeStruct((nseq, SUBLANES, LANES), F32)],
        grid=(nseq // nsb, seq // BLK),
        in_specs=[blk_spec(3 * W_M), blk_spec(LANES), blk_spec(W_M),
                  _const_spec(c0), _const_spec(n0), _const_spec(m0)],
        out_specs=[blk_spec(W_M),
                   pl.BlockSpec((nsb, NH_M, DH_M, DH_M), lambda b, j: (b, 0, 0, 0)),
                   pl.BlockSpec((nsb, SUBLANES, LANES), lambda b, j: (b, 0, 0)),
                   pl.BlockSpec((nsb, SUBLANES, LANES), lambda b, j: (b, 0, 0))],
        compiler_params=_cparams(("arbitrary", "arbitrary")),
        name="mlstm",
    )(qkvm, ifg, og, c0, n0, m0)


def _mlstm_step_kernel(qkv_ref, ifg_ref, og_ref, c_ref, n_ref, m_ref,
                       hg_ref, c_out, n_out, m_out):
    gate = ifg_ref[...]
    ig = gate
    lf = pltpu.roll(gate, LANES - NH_M, axis=1)
    log_prev = lf + m_ref[...]
    m_t = jnp.maximum(log_prev, ig)
    dmat = jnp.exp(ig - m_t)
    w_prev = jnp.exp(log_prev - m_t)
    floor = jnp.exp(-m_t)
    m_out[...] = m_t

    qs, kds, vs, ss = [], [], [], []
    for hd in range(NH_M):
        q = qkv_ref[:, hd * DH_M:(hd + 1) * DH_M]
        k = qkv_ref[:, W_M + hd * DH_M:W_M + (hd + 1) * DH_M]
        v = qkv_ref[:, 2 * W_M + hd * DH_M:2 * W_M + (hd + 1) * DH_M]
        d_h = dmat[:, hd:hd + 1]
        qs.append(q)
        kds.append(k * d_h)
        vs.append(v)
        ss.append(jnp.sum(q * k, axis=-1, keepdims=True) * d_h)
    stacked = jnp.concatenate(qs + kds + [jnp.zeros((BLK - 2 * NH_M * DEC_ROWS, DH_M), F32)], axis=0)
    cols = stacked.T

    for r in range(DEC_ROWS):
        for hd in range(NH_M):
            q_col = cols[:, hd * DEC_ROWS + r:hd * DEC_ROWS + r + 1]
            k_col = cols[:, (NH_M + hd) * DEC_ROWS + r:(NH_M + hd) * DEC_ROWS + r + 1]
            c = c_ref[r, hd]
            n_row = n_ref[r, hd:hd + 1, :]
            v_row = vs[hd][r:r + 1, :]
            wp = w_prev[r:r + 1, hd:hd + 1]
            s = ss[hd][r:r + 1, :]
            qc = jnp.sum(q_col * c, axis=0, keepdims=True)
            qn = jnp.sum(qs[hd][r:r + 1, :] * n_row, axis=-1, keepdims=True)
            num = s * v_row + wp * qc
            den = s + wp * qn
            h = num / jnp.maximum(jnp.abs(den), floor[r:r + 1, hd:hd + 1])
            hg_ref[r:r + 1, hd * DH_M:(hd + 1) * DH_M] = og_ref[r:r + 1, hd * DH_M:(hd + 1) * DH_M] * h
            c_out[r, hd] = wp * c + k_col * v_row
            n_out[r, hd:hd + 1, :] = wp * n_row + kds[hd][r:r + 1, :]


def _mlstm_step(qkvm, ifg, og, c, n, m):
    nseq = qkvm.shape[0]
    row_spec = lambda w: pl.BlockSpec((DEC_ROWS, w), lambda i: (i, 0))
    c_spec = pl.BlockSpec((DEC_ROWS, NH_M, DH_M, DH_M), lambda i: (i, 0, 0, 0))
    n_spec = pl.BlockSpec((DEC_ROWS, NH_M, DH_M), lambda i: (i, 0, 0))
    return pl.pallas_call(
        _mlstm_step_kernel,
        out_shape=[jax.ShapeDtypeStruct((nseq, W_M), F32),
                   jax.ShapeDtypeStruct(c.shape, F32),
                   jax.ShapeDtypeStruct(n.shape, F32),
                   jax.ShapeDtypeStruct((nseq, LANES), F32)],
        grid=(nseq // DEC_ROWS,),
        in_specs=[row_spec(3 * W_M), row_spec(LANES), row_spec(W_M),
                  _state_spec(c, DEC_ROWS), _state_spec(n, DEC_ROWS), row_spec(LANES)],
        out_specs=[row_spec(W_M), c_spec, n_spec, row_spec(LANES)],
        compiler_params=_cparams(("arbitrary",)),
        name="mlstm_step",
    )(*_operands(qkvm, ifg, og, c, n, m))


def _ln_silu(y, g, b):
    yc = y - jnp.mean(y, axis=-1, keepdims=True)
    var = jnp.mean(yc * yc, axis=-1, keepdims=True)
    return _silu(yc * lax.rsqrt(var + LN_EPS) * g + b)


def _conv_step_kernel(st_ref, u_ref, w_ref, b_ref, g_ref, beta_ref, out_ref):
    w_hist = w_ref[0:CONV_W - 1, :]
    rows = []
    for r in range(DEC_ROWS):
        rows.append(jnp.sum(st_ref[r] * w_hist, axis=0, keepdims=True))
    acc = jnp.concatenate(rows, axis=0) + w_ref[CONV_W - 1:CONV_W, :] * u_ref[...] + b_ref[...]
    out_ref[...] = _ln_silu(acc, g_ref[...], beta_ref[...])


def _conv_step(state, u, w, b, g, beta):
    nseq = u.shape[0]
    return pl.pallas_call(
        _conv_step_kernel,
        out_shape=jax.ShapeDtypeStruct(u.shape, F32),
        grid=(nseq // DEC_ROWS,),
        in_specs=[_state_spec(state, DEC_ROWS),
                  pl.BlockSpec((DEC_ROWS, C_CONV), lambda i: (i, 0)),
                  _const_spec(w), _const_spec(b), _const_spec(g), _const_spec(beta)],
        out_specs=pl.BlockSpec((DEC_ROWS, C_CONV), lambda i: (i, 0)),
        compiler_params=_cparams(("arbitrary",)),
        name="dwconv_step",
    )(*_operands(state, u, w, b, g, beta))


def _stack_heads(q, low_half):
    heads = []
    for r in range(REP_A):
        slab = q[:, r * LANES:(r + 1) * LANES]
        heads.append(jnp.where(low_half, slab, 0.0))
        heads.append(jnp.where(low_half, 0.0, slab))
    return jnp.concatenate(heads, axis=0)


def _attn_block_kernel(base, carry, q_ref, k_ref, v_ref, k0_ref, v0t_ref, pos0_ref, sink_ref,
                       o_ref, kprev, vtprev):
    j = pl.program_id(1)
    nsb = q_ref.shape[0]

    if carry:
        @pl.when(j == 0)
        def _():
            kprev[...] = jnp.broadcast_to(k0_ref[...], kprev.shape).astype(kprev.dtype)
            vtprev[...] = jnp.broadcast_to(v0t_ref[...], vtprev.shape).astype(vtprev.dtype)

    kpos_prev = jnp.where(j == 0, pos0_ref[...],
                          base + BLK * j - BLK + lax.broadcasted_iota(jnp.int32, (BLK, 1), 0))
    kpos = jnp.concatenate([kpos_prev, base + BLK * j + lax.broadcasted_iota(jnp.int32, (BLK, 1), 0)], axis=0)
    qpos = base + BLK * j + lax.broadcasted_iota(jnp.int32, (1, BLK), 1)
    bias = jnp.where((kpos >= 0) & (kpos <= qpos) & (qpos - kpos <= WINDOW), 0.0, -jnp.inf)
    low_half = lax.broadcasted_iota(jnp.int32, (BLK, LANES), 1) < DH_A
    scale = DH_A ** -0.5

    scores, vt_cats = [], []
    for sb in range(nsb):
        k = k_ref[sb]
        vt = v_ref[sb].astype(F32).T.astype(BF16)
        k_prev, vt_prev = (kprev[sb], vtprev[sb]) if carry else (k0_ref[0], v0t_ref[0])
        k_cat = jnp.concatenate([k_prev.astype(BF16), k.astype(BF16)], axis=0)
        vt_cats.append(jnp.concatenate([vt_prev.astype(BF16), vt], axis=1))
        scores.append(_dot_nt(k_cat, _stack_heads(q_ref[sb] * scale, low_half)))
        if carry:
            kprev[sb] = k
            vtprev[sb] = vt

    probs, invs = [], []
    for st in scores:
        p_heads, inv = [], []
        for hd in range(H_A):
            r, g = divmod(hd, KV_A)
            sink = sink_ref[0:1, g * REP_A + r:g * REP_A + r + 1]
            s = st[:, hd * BLK:(hd + 1) * BLK] + bias
            mx = jnp.maximum(jnp.max(s, axis=0, keepdims=True), sink)
            p = jnp.exp(s - mx)
            inv.append(1.0 / (jnp.sum(p, axis=0, keepdims=True) + jnp.exp(sink - mx)))
            p_heads.append(p.astype(BF16))
        probs.append(jnp.concatenate(p_heads, axis=1))
        invs.append(inv)

    for sb in range(nsb):
        ot = jnp.dot(vt_cats[sb], probs[sb], preferred_element_type=F32)
        out_t = jnp.concatenate(
            [ot[(hd % KV_A) * DH_A:(hd % KV_A + 1) * DH_A, hd * BLK:(hd + 1) * BLK] * invs[sb][hd]
             for hd in range(H_A)], axis=0)
        o_ref[sb] = out_t.T.astype(o_ref.dtype)


def _attn_blocks(q, k, v, k0, v0t, pos0, sinks, nsb, base):
    nseq, seq, _ = q.shape
    q_spec = pl.BlockSpec((nsb, BLK, W_Q), lambda s, j: (s, j, 0))
    k_spec = pl.BlockSpec((nsb, BLK, W_KV), lambda s, j: (s, j, 0))
    return pl.pallas_call(
        functools.partial(_attn_block_kernel, base, seq > BLK),
        out_shape=jax.ShapeDtypeStruct(q.shape, q.dtype),
        grid=(nseq // nsb, seq // BLK),
        in_specs=[q_spec, k_spec, k_spec, _const_spec(k0), _const_spec(v0t),
                  _const_spec(pos0), _const_spec(sinks)],
        out_specs=q_spec,
        scratch_shapes=[pltpu.VMEM((nsb, BLK, W_KV), k.dtype), pltpu.VMEM((nsb, W_KV, BLK), BF16)],
        compiler_params=_cparams(("arbitrary", "arbitrary")),
        name="swa",
    )(*_operands(q, k, v, k0, v0t, pos0, sinks))


def _attn_step_kernel(base, q_ref, k_ref, v_ref, kc_ref, vc_ref, sink_ref, o_ref):
    low_half = lax.broadcasted_iota(jnp.int32, (1, LANES), 1) < DH_A
    sinks = jnp.concatenate([sink_ref[0:1, g * REP_A + r:g * REP_A + r + 1]
                             for r in range(REP_A) for g in range(KV_A)], axis=0)
    kpos = base - WINDOW + lax.broadcasted_iota(jnp.int32, (1, 2 * BLK), 1)
    bias = jnp.where((kpos >= 0) & (kpos <= base) & (base - kpos <= WINDOW), 0.0, -jnp.inf)
    pad = jnp.zeros((BLK - 1, W_KV), F32)
    scale = DH_A ** -0.5
    nsb = q_ref.shape[0]
    scores = []
    for sb in range(nsb):
        k_cat = jnp.concatenate([kc_ref[sb], k_ref[sb], pad], axis=0)
        scores.append(_dot_nt(_stack_heads(q_ref[sb] * scale, low_half), k_cat) + bias)
    probs = []
    for s in scores:
        mx = jnp.maximum(jnp.max(s, axis=-1, keepdims=True), sinks)
        p = jnp.exp(s - mx)
        den = jnp.sum(p, axis=-1, keepdims=True) + jnp.exp(sinks - mx)
        probs.append(p * (1.0 / den))
    for sb in range(nsb):
        v_cat = jnp.concatenate([vc_ref[sb], v_ref[sb], pad], axis=0)
        o = _dot(probs[sb], v_cat)
        for r in range(REP_A):
            o_ref[sb, :, r * LANES:(r + 1) * LANES] = jnp.where(low_half, o[2 * r:2 * r + 1], o[2 * r + 1:2 * r + 2])


def _attn_step(q, k, v, k_cache, v_cache, sinks, base):
    nseq = q.shape[0]
    spec = lambda rows, n: pl.BlockSpec((DEC_ROWS, rows, n), lambda i: (i, 0, 0))
    return pl.pallas_call(
        functools.partial(_attn_step_kernel, base),
        out_shape=jax.ShapeDtypeStruct(q.shape, F32),
        grid=(nseq // DEC_ROWS,),
        in_specs=[spec(1, W_Q), spec(1, W_KV), spec(1, W_KV), spec(WINDOW, W_KV), spec(WINDOW, W_KV),
                  _const_spec(sinks)],
        out_specs=spec(1, W_Q),
        compiler_params=_cparams(("arbitrary",)),
        name="swa_step",
    )(*_operands(q, k, v, k_cache, v_cache, sinks))


def _merge_kernel(x_ref, hg_ref, cv_ref, oa_ref, g1_ref, wgm_ref, bm_ref, wm_ref, wc_ref, wa_ref, wo_ref,
                  out_ref):
    d_model = x_ref.shape[1]
    x = x_ref[...]
    h = _rms(x, g1_ref[...]).astype(BF16)
    mix = None
    for part, (branch_ref, w_ref) in enumerate(((hg_ref, wm_ref), (cv_ref, wc_ref), (oa_ref, wa_ref))):
        cols = slice(part * d_model, (part + 1) * d_model)
        gate = _sigmoid(jnp.dot(h, wgm_ref[:, cols], preferred_element_type=F32) + bm_ref[:, cols])
        term = gate * _dot(branch_ref[...], w_ref[...])
        mix = term if mix is None else mix + term
    out_ref[...] = x + _dot(mix, wo_ref[...])


def _merge(x, hg, cv, oa, g1, wgm, bm, wm, wc, wa, wo, tm):
    rows, d_model = x.shape
    row_spec = lambda n: pl.BlockSpec((tm, n), lambda i: (i, 0))
    return pl.pallas_call(
        _merge_kernel,
        out_shape=jax.ShapeDtypeStruct(x.shape, F32),
        grid=(rows // tm,),
        in_specs=[row_spec(d_model), row_spec(W_M), row_spec(C_CONV), row_spec(W_Q),
                  _const_spec(g1), _const_spec(wgm), _const_spec(bm),
                  _const_spec(wm), _const_spec(wc), _const_spec(wa), _const_spec(wo)],
        out_specs=row_spec(d_model),
        compiler_params=_cparams(("arbitrary",)),
        name="merge",
    )(*_operands(x, hg, cv, oa, g1, wgm, bm, wm, wc, wa, wo))


def _ffn_body(x, g2_ref, wg_ref, wu_ref, wd_ref, wc_ref, bc_ref, prev_rows):
    d_ff = wg_ref.shape[1]
    tiles = d_ff // MXU_TILE
    split = (tiles + 1) // 2 * MXU_TILE if tiles >= 2 else d_ff
    bounds = [(0, split), (split, d_ff)] if split < d_ff else [(0, d_ff)]
    h2 = _rms(x, g2_ref[...]).astype(BF16)
    gps, acts = [], []
    for lo, hi in bounds:
        gp = jnp.dot(h2, wg_ref[:, lo:hi], preferred_element_type=F32)
        up = jnp.dot(h2, wu_ref[:, lo:hi], preferred_element_type=F32)
        p2, p1 = prev_rows(gp, lo, hi)
        gc = (wc_ref[0:1, lo:hi] * p2 + wc_ref[1:2, lo:hi] * p1 + wc_ref[2:3, lo:hi] * gp
              + bc_ref[:, lo:hi])
        acts.append((_silu(gc) * up).astype(BF16))
        gps.append(gp)
    acc = x
    for (lo, hi), a in zip(bounds, acts):
        acc = acc + jnp.dot(a, wd_ref[lo:hi, :], preferred_element_type=F32)
    return acc, gps


def _ffn_seq_kernel(tiles_per_seq, tail_end, final, x_ref, g2_ref, wg_ref, wu_ref, wd_ref, wc_ref, bc_ref,
                    init_ref, gf_ref, out_ref, tail_ref, carry):
    @pl.when(pl.program_id(0) % tiles_per_seq == 0)
    def _():
        carry[...] = init_ref[...]

    tm = x_ref.shape[0]
    row = lax.broadcasted_iota(jnp.int32, (tm, 1), 0)

    def prev_rows(gp, lo, hi):
        c2 = carry[SUBLANES - 2:SUBLANES - 1, lo:hi]
        c1 = carry[SUBLANES - 1:SUBLANES, lo:hi]
        p1 = jnp.where(row == 0, c1, pltpu.roll(gp, 1, axis=0))
        p2 = jnp.where(row == 0, c2, jnp.where(row == 1, c1, pltpu.roll(gp, 2, axis=0)))
        return p2, p1

    acc, gps = _ffn_body(x_ref[...], g2_ref, wg_ref, wu_ref, wd_ref, wc_ref, bc_ref, prev_rows)
    out_ref[...] = _rms(acc, gf_ref[...]) if final else acc
    tail = jnp.concatenate([gp[tail_end - SUBLANES:tail_end, :] for gp in gps], axis=1)
    tail_ref[...] = tail
    if tiles_per_seq > 1:
        carry[...] = tail


def _ffn_seq(x, g2, wg, wu, wd, wc, bc, init, gf, tm, tiles_per_seq, tail_end, final):
    rows, d_model = x.shape
    d_ff = wg.shape[1]
    assert tiles_per_seq == 1 or tail_end == tm
    return pl.pallas_call(
        functools.partial(_ffn_seq_kernel, tiles_per_seq, tail_end, final),
        out_shape=[jax.ShapeDtypeStruct(x.shape, F32),
                   jax.ShapeDtypeStruct((rows // tm, SUBLANES, d_ff), F32)],
        grid=(rows // tm,),
        in_specs=[pl.BlockSpec((tm, d_model), lambda i: (i, 0)), _const_spec(g2),
                  _const_spec(wg), _const_spec(wu), _const_spec(wd),
                  _const_spec(wc), _const_spec(bc), _const_spec(init),
                  _const_spec(gf)],
        out_specs=[pl.BlockSpec((tm, d_model), lambda i: (i, 0)),
                   pl.BlockSpec((None, SUBLANES, d_ff), lambda i: (i, 0, 0))],
        scratch_shapes=[pltpu.VMEM((SUBLANES, d_ff), F32)],
        compiler_params=_cparams(("arbitrary",)),
        name="convffn",
    )(*_operands(x, g2, wg, wu, wd, wc, bc, init, gf))


def _ffn_step_kernel(final, x_ref, g2_ref, wg_ref, wu_ref, wd_ref, wc_ref, bc_ref,
                     p2_ref, p1_ref, gf_ref, out_ref, gp_ref):
    def prev_rows(gp, lo, hi):
        return p2_ref[:, lo:hi], p1_ref[:, lo:hi]

    acc, gps = _ffn_body(x_ref[...], g2_ref, wg_ref, wu_ref, wd_ref, wc_ref, bc_ref, prev_rows)
    out_ref[...] = _rms(acc, gf_ref[...]) if final else acc
    gp_ref[...] = jnp.concatenate(gps, axis=1)


def _ffn_step(x, g2, wg, wu, wd, wc, bc, p2, p1, gf, tm, final):
    rows, d_model = x.shape
    d_ff = wg.shape[1]
    row_spec = lambda n: pl.BlockSpec((tm, n), lambda i: (i, 0))
    return pl.pallas_call(
        functools.partial(_ffn_step_kernel, final),
        out_shape=[jax.ShapeDtypeStruct(x.shape, F32), jax.ShapeDtypeStruct((rows, d_ff), F32)],
        grid=(rows // tm,),
        in_specs=[row_spec(d_model), _const_spec(g2),
                  _const_spec(wg), _const_spec(wu), _const_spec(wd),
                  _const_spec(wc), _const_spec(bc), row_spec(d_ff), row_spec(d_ff),
                  _const_spec(gf)],
        out_specs=[row_spec(d_model), row_spec(d_ff)],
        compiler_params=_cparams(("arbitrary",)),
        name="convffn_step",
    )(*_operands(x, g2, wg, wu, wd, wc, bc, p2, p1, gf))


def _row_tile(rows, cap):
    tm = min(rows, cap)
    while rows % tm:
        tm //= 2
    return tm


def kernel(x_prompt, x_sample, state_mlstm_c, state_mlstm_n, state_mlstm_m, state_conv, cache_swa_k, cache_swa_v, state_ffn_conv, meta_tokens, norm1_g, w_in, b_igate, b_fgate, w_mlstm_out, w_dconv, b_dconv, ln_conv_g, ln_conv_b, w_conv_out, attn_sinks, w_attn_out, b_merge, w_out, norm2_g, w_ffn_gate, w_ffn_up, w_ffn_conv, b_ffn_conv, w_ffn_down, final_norm_g):
    bsz, seq, d_model = x_prompt.shape
    nseq_s = x_sample.shape[0]
    depth = w_in.shape[0]
    d_ff = w_ffn_gate.shape[2]
    nblk = seq // BLK
    assert seq % BLK == 0 and x_sample.shape[1] == 1 and nseq_s % DEC_ROWS == 0
    assert cache_swa_k.shape[2] == WINDOW
    act = BF16
    nsb = SEQ_PER_STEP if bsz % SEQ_PER_STEP == 0 else 1

    x_main = x_prompt.reshape(bsz * seq, d_model)
    x_meta = jnp.concatenate([meta_tokens.astype(F32), jnp.zeros((BLK - N_META, d_model), F32)], axis=0)
    x_smp = x_sample.reshape(nseq_s, d_model)

    tm_main = _row_tile(seq, 512)
    tm_smp = _row_tile(nseq_s, 128)

    tab_meta = _rope_tables(jnp.arange(BLK))
    tab_main = _rope_tables(N_META + jnp.arange(seq))
    tab_smp = _rope_tables(jnp.full((tm_smp,), PAST_LEN))
    def regroup_heads(w, axis):
        shape = w.shape
        w = w.reshape(shape[:axis] + (KV_A, REP_A, DH_A) + shape[axis + 1:])
        return jnp.swapaxes(w, axis, axis + 1).reshape(shape)
    iota_blk = jnp.arange(BLK, dtype=jnp.int32)[:, None]
    pos0_meta = jnp.full((BLK, 1), -1, jnp.int32)
    pos0_main = jnp.where(iota_blk < N_META, iota_blk, -1)

    zeros_c = jnp.zeros((1, NH_M, DH_M, DH_M), F32)
    zeros_nm = jnp.zeros((1, SUBLANES, LANES), F32)
    zeros_kv = jnp.zeros((1, BLK, W_KV), act)
    gf = final_norm_g.reshape(1, d_model)

    col_gates = 3 * W_M
    col_o = col_gates + 2 * NH_M
    col_qa = col_o + W_M + 2 * C_CONV
    col_g = col_qa + W_Q + 2 * W_KV
    w_pack_all = jnp.concatenate(
        [w_in[:, :, :col_gates], w_in[:, :, col_o:col_qa], regroup_heads(w_in[:, :, col_qa:col_qa + W_Q], 2),
         w_in[:, :, col_qa + W_Q:col_g], w_in[:, :, col_gates:col_o],
         jnp.zeros((depth, d_model, LANES - 2 * NH_M), F32)], axis=2).astype(BF16)
    bif_all = jnp.concatenate([b_igate, b_fgate, jnp.zeros((depth, LANES - 2 * NH_M), F32)], axis=1)[:, None]
    w_dc_all = jnp.concatenate([w_dconv, jnp.zeros((depth, HIST - CONV_W, C_CONV), F32)], axis=1)
    sinks_all = jnp.concatenate([attn_sinks, jnp.zeros((depth, LANES - H_A), F32)], axis=1)[:, None]
    w_fc_all = jnp.concatenate([w_ffn_conv, jnp.zeros((depth, SUBLANES - FFN_CONV_W, d_ff), F32)], axis=1)
    stacks = dict(
        g1=norm1_g[:, None], w_pack=w_pack_all, wgm=w_in[:, :, col_g:].astype(BF16), bm=b_merge[:, None],
        bif=bif_all,
        w_dc=w_dc_all, b_dc=b_dconv[:, None], ln_g=ln_conv_g[:, None], ln_b=ln_conv_b[:, None],
        sinks=sinks_all, wm=w_mlstm_out.astype(BF16), wc=w_conv_out.astype(BF16),
        wa=regroup_heads(w_attn_out, 1).astype(BF16), wo=w_out.astype(BF16), g2=norm2_g[:, None],
        wg=w_ffn_gate.astype(BF16), wu=w_ffn_up.astype(BF16), wd=w_ffn_down.astype(BF16),
        w_fc=w_fc_all, b_fc=b_ffn_conv[:, None])

    p_states = [[] for _ in range(7)]
    s_states = [[] for _ in range(7)]
    for l in range(depth):
        final = l == depth - 1
        prm = {name: _LayerParam(stack, l) for name, stack in stacks.items()}
        g1, w_pack, wgm, bm, bif = prm["g1"], prm["w_pack"], prm["wgm"], prm["bm"], prm["bif"]
        w_dc, b_dc, ln_g, ln_b, sinks = prm["w_dc"], prm["b_dc"], prm["ln_g"], prm["ln_b"], prm["sinks"]
        wm, wc, wa, wo, g2 = prm["wm"], prm["wc"], prm["wa"], prm["wo"], prm["g2"]
        wg, wu, wd, w_fc, b_fc = prm["wg"], prm["wu"], prm["wd"], prm["w_fc"], prm["b_fc"]

        conv_w = (w_dc, b_dc, ln_g, ln_b)
        qkvm, og, u, qa, ka, va, ifg, cv = _in_proj(
            x_meta, g1, w_pack, bif, tab_meta, BLK, act, (jnp.zeros((HIST, C_CONV), F32),) + conv_w, 1)
        hg, c_meta, n_meta, m_meta = _mlstm(qkvm[None], ifg[None], og[None], zeros_c, zeros_nm, zeros_nm,
                                            1, N_META)
        oa = _attn_blocks(qa[None], ka[None], va[None], zeros_kv, zeros_kv, pos0_meta, sinks, 1, 0)
        x1 = _merge(x_meta, hg[0], cv, oa[0], g1, wgm, bm, wm, wc, wa, wo, BLK)
        x_meta, tail_meta = _ffn_seq(x1, g2, wg, wu, wd, w_fc, b_fc, jnp.zeros((SUBLANES, d_ff), F32), gf,
                                     BLK, 1, N_META, False)
        u_meta, ka_meta, va_meta = u, ka, va

        hist0 = jnp.concatenate([jnp.zeros((HIST - N_META, C_CONV), F32), u_meta[:N_META].astype(F32)], axis=0)
        qkvm, og, u, qa, ka, va, ifg, cv = _in_proj(
            x_main, g1, w_pack, bif, tab_main, tm_main, act, (hist0,) + conv_w, seq // tm_main)
        per_seq = lambda a: a.reshape(bsz, seq, a.shape[-1])
        hg, p_c, p_n, p_m = _mlstm(per_seq(qkvm), per_seq(ifg), per_seq(og), c_meta, n_meta, m_meta, nsb, BLK)
        oa = _attn_blocks(per_seq(qa), per_seq(ka), per_seq(va), ka_meta[None], va_meta.T[None], pos0_main,
                          sinks, nsb, N_META)
        flat = lambda a: a.reshape(bsz * seq, a.shape[-1])
        x1 = _merge(x_main, flat(hg), cv, flat(oa), g1, wgm, bm, wm, wc, wa, wo, tm_main)
        x_main, tails = _ffn_seq(x1, g2, wg, wu, wd, w_fc, b_fc, tail_meta[0], gf,
                                 tm_main, seq // tm_main, tm_main, final)
        p_states[0].append(p_c)
        p_states[1].append(p_n[:, :NH_M, :])
        p_states[2].append(p_m[:, 0, :NH_M])
        p_states[3].append(per_seq(u)[:, seq - (CONV_W - 1):].astype(F32))
        last_window = lambda a: per_seq(a)[:, seq - WINDOW:].astype(F32).reshape(bsz, WINDOW, KV_A, DH_A)
        p_states[4].append(last_window(ka))
        p_states[5].append(last_window(va))
        p_states[6].append(tails.reshape(bsz, seq // tm_main, SUBLANES, d_ff)[:, -1, SUBLANES - (FFN_CONV_W - 1):])

        qkvm, og, u, qa, ka, va, ifg = _in_proj(x_smp, g1, w_pack, bif, tab_smp, tm_smp, F32)
        m_in = jnp.concatenate([state_mlstm_m[l], jnp.zeros((nseq_s, LANES - NH_M), F32)], axis=1)
        hg, s_c, s_n, s_m = _mlstm_step(qkvm, ifg, og, _LayerParam(state_mlstm_c, l),
                                        _LayerParam(state_mlstm_n, l), m_in)
        cv = _conv_step(_LayerParam(state_conv, l), u, w_dc, b_dc, ln_g, ln_b)
        oa = _attn_step(qa[:, None], ka[:, None], va[:, None],
                        cache_swa_k[l].reshape(nseq_s, WINDOW, W_KV), cache_swa_v[l].reshape(nseq_s, WINDOW, W_KV),
                        sinks, PAST_LEN)
        x1 = _merge(x_smp, hg, cv, oa[:, 0], g1, wgm, bm, wm, wc, wa, wo, tm_smp)
        x_smp, gp_s = _ffn_step(x1, g2, wg, wu, wd, w_fc, b_fc, state_ffn_conv[l][:, 0], state_ffn_conv[l][:, 1],
                                gf, tm_smp, final)
        s_states[0].append(s_c)
        s_states[1].append(s_n)
        s_states[2].append(s_m[:, :NH_M])
        s_states[3].append(u[:, None])
        s_states[4].append(ka.reshape(nseq_s, 1, KV_A, DH_A))
        s_states[5].append(va.reshape(nseq_s, 1, KV_A, DH_A))
        s_states[6].append(gp_s[:, None])

    y_prompt = x_main.reshape(bsz, seq, d_model)
    y_sample = x_smp.reshape(nseq_s, 1, d_model)
    slide = lambda old, new_rows: jnp.concatenate([old[:, :, 1:], jnp.stack(new_rows, axis=0)], axis=2)
    s_out = [jnp.stack(a, axis=0) for a in s_states[:3]] + [
        slide(old, new) for old, new in zip((state_conv, cache_swa_k, cache_swa_v, state_ffn_conv), s_states[3:])]
    return (y_prompt, y_sample, *(jnp.stack(a, axis=0) for a in p_states), *s_out)
```
